```python
import jax, jax.numpy as jnp
from jax import lax
import numpy as np

D_MODEL = 1024
BATCH = 8
SEQ = 2048
DEPTH = 4
DEC_BATCH = 32
DEC_SEQ = 4
PAST_LEN = 8192
PAGE_SIZE = 128

N_MIXERS = 3
FOX_HEADS = 16
FOX_HEAD_DIM = D_MODEL // FOX_HEADS
Q_BLOCK = 128
FORGET_BIAS = 3.0
SC_WIDTH = 3
LRU_WIDTH = D_MODEL
LRU_BLOCKS = 16
LRU_BLOCK_DIM = LRU_WIDTH // LRU_BLOCKS
LRU_CONV_WIDTH = 4
LRU_C = 8.0
FF_DENSE = 2816
N_EXPERTS = 8
TOP_K = 2
FF_EXPERT = 3584
LN_EPS = 1e-5
DEEPNORM_ALPHA = (2 * DEPTH) ** 0.25
DEEPNORM_BETA = (8 * DEPTH) ** -0.25
N_FOX_LAYERS = (DEPTH + 2) // 3
N_SC_LAYERS = (DEPTH + 1) // 3
N_LRU_LAYERS = DEPTH // 3
N_DENSE_LAYERS = (DEPTH + 1) // 2
N_MOE_LAYERS = DEPTH // 2

kernel_name = 'fox_shortconv_rglru_hybrid_step'


def layer_norm(x, g, b):
    xf = x.astype(jnp.float32)
    mu = xf.mean(-1, keepdims=True)
    var = jnp.square(xf - mu).mean(-1, keepdims=True)
    y = (xf - mu) * lax.rsqrt(var + LN_EPS) * g.astype(jnp.float32) + b.astype(jnp.float32)
    return y.astype(x.dtype)


def causal_depthwise_conv(buf, u, w):
    width = w.shape[0]
    t = u.shape[1]
    ext = jnp.concatenate([buf.astype(u.dtype), u], axis=1)
    y = sum(w[j] * ext[:, j:j + t] for j in range(width))
    return y, ext[:, ext.shape[1] - (width - 1):]


def gather_pages(pool, page_table):
    g = pool[page_table]
    return g.reshape((g.shape[0], g.shape[1] * g.shape[2]) + g.shape[3:])


def fox_block(q, fq, qpos, k, v, fk, kpos):
    s = jnp.einsum('bqhd,bkhd->bhqk', q, k).astype(jnp.float32) * (FOX_HEAD_DIM ** -0.5)
    bias = jnp.swapaxes(fq, 1, 2)[:, :, :, None] - jnp.swapaxes(fk, 1, 2)[:, :, None, :]
    causal = kpos[None, :] <= qpos[:, None]
    s = jnp.where(causal, s + bias, -jnp.inf)
    p = jax.nn.softmax(s, axis=-1)
    return jnp.einsum('bhqk,bkhd->bqhd', p.astype(v.dtype), v)


def fox_mixer(h, past_k, past_v, past_lf, w_in, b_f, w_out):
    b, t, d = h.shape
    p_len = past_k.shape[1]
    proj = h @ w_in
    q = proj[..., :d].reshape(b, t, FOX_HEADS, FOX_HEAD_DIM)
    k = proj[..., d:2 * d].reshape(b, t, FOX_HEADS, FOX_HEAD_DIM)
    v = proj[..., 2 * d:3 * d].reshape(b, t, FOX_HEADS, FOX_HEAD_DIM)
    lf = jax.nn.log_sigmoid((proj[..., 3 * d:] + b_f).astype(jnp.float32))
    k_all = jnp.concatenate([past_k.astype(k.dtype), k], axis=1)
    v_all = jnp.concatenate([past_v.astype(v.dtype), v], axis=1)
    f_all = jnp.cumsum(jnp.concatenate([past_lf.astype(jnp.float32), lf], axis=1), axis=1)
    fq = f_all[:, p_len:]
    qpos = p_len + jnp.arange(t)
    kpos = jnp.arange(p_len + t)
    if t > Q_BLOCK and t % Q_BLOCK == 0:
        nb = t // Q_BLOCK
        qb = jnp.moveaxis(q.reshape(b, nb, Q_BLOCK, FOX_HEADS, FOX_HEAD_DIM), 1, 0)
        fb = jnp.moveaxis(fq.reshape(b, nb, Q_BLOCK, FOX_HEADS), 1, 0)
        pb = qpos.reshape(nb, Q_BLOCK)
        o = lax.map(lambda a: fox_block(a[0], a[1], a[2], k_all, v_all, f_all, kpos), (qb, fb, pb))
        o = jnp.moveaxis(o, 0, 1).reshape(b, t, FOX_HEADS, FOX_HEAD_DIM)
    else:
        o = fox_block(q, fq, qpos, k_all, v_all, f_all, kpos)
    return o.reshape(b, t, d) @ w_out, k, v, lf


def short_conv_mixer(h, buf, w_in, w_conv, w_out):
    d = h.shape[-1]
    proj = h @ w_in
    y, new_buf = causal_depthwise_conv(buf, proj[..., d:2 * d] * proj[..., 2 * d:], w_conv)
    return (proj[..., :d] * y) @ w_out, new_buf


def rglru_mixer(h, buf, h0, w_in, conv_w, conv_b, w_a, b_a, w_x, b_x, lam, w_out):
    b, t, _ = h.shape
    proj = h @ w_in
    gate = jax.nn.gelu(proj[..., :LRU_WIDTH])
    xc, new_buf = causal_depthwise_conv(buf, proj[..., LRU_WIDTH:], conv_w)
    xc = xc + conv_b
    xb = xc.reshape(b, t, LRU_BLOCKS, LRU_BLOCK_DIM)
    r = jax.nn.sigmoid((jnp.einsum('btnd,nde->btne', xb, w_a).reshape(b, t, LRU_WIDTH) + b_a).astype(jnp.float32))
    i = jax.nn.sigmoid((jnp.einsum('btnd,nde->btne', xb, w_x).reshape(b, t, LRU_WIDTH) + b_x).astype(jnp.float32))
    log_a = -LRU_C * r * jax.nn.softplus(-lam.astype(jnp.float32))
    a = jnp.exp(log_a)
    u = jnp.sqrt(-jnp.expm1(2.0 * log_a)) * i * xc.astype(jnp.float32)

    def step(hc, au):
        hc = au[0] * hc + au[1]
        return hc, hc

    h_last, hs = lax.scan(step, h0.astype(jnp.float32), (jnp.swapaxes(a, 0, 1), jnp.swapaxes(u, 0, 1)))
    hs = jnp.swapaxes(hs, 0, 1).astype(h.dtype)
    return (gate * hs) @ w_out, new_buf, h_last


def swiglu(h, w_gu, w_down):
    ff = w_down.shape[0]
    gu = h @ w_gu
    return (jax.nn.silu(gu[..., :ff]) * gu[..., ff:]) @ w_down


def moe_swiglu(h, w_router, w_gu, w_down):
    logits = (h @ w_router).astype(jnp.float32)
    top_v, top_i = lax.top_k(logits, TOP_K)
    top_w = jax.nn.softmax(top_v, axis=-1)
    gates = jnp.einsum('btk,btke->bte', top_w, jax.nn.one_hot(top_i, N_EXPERTS, dtype=jnp.float32)).astype(h.dtype)
    out = jnp.zeros_like(h)
    for e in range(N_EXPERTS):
        out = out + gates[..., e:e + 1] * swiglu(h, w_gu[e], w_down[e])
    return out


def trunk(x, c, fox_past, sc_buf, lru_buf, lru_h, p):
    new_k, new_v, new_lf, new_sc, new_lb, new_lh = [], [], [], [], [], []
    c_act = jax.nn.silu(c)
    for i in range(DEPTH):
        mod = (c_act @ p['w_ada'][i] + p['b_ada'][i])[:, None, :]
        sh1, sc1, g1, sh2, sc2, g2 = jnp.split(mod, 6, axis=-1)
        hm = x * (1 + sc1) + sh1
        kind = i % N_MIXERS
        j = i // N_MIXERS
        if kind == 0:
            pk, pv, plf = fox_past(j)
            out, k, v, lf = fox_mixer(hm, pk, pv, plf, p['fox_w_in'][j], p['fox_b_f'][j], p['fox_w_out'][j])
            new_k.append(k)
            new_v.append(v)
            new_lf.append(lf)
        elif kind == 1:
            out, nb = short_conv_mixer(hm, sc_buf[j], p['sc_w_in'][j], p['sc_w_conv'][j], p['sc_w_out'][j])
            new_sc.append(nb)
        else:
            out, nb, hl = rglru_mixer(hm, lru_buf[j], lru_h[j], p['lru_w_in'][j], p['lru_conv_w'][j],
                                      p['lru_conv_b'][j], p['lru_w_a'][j], p['lru_b_a'][j], p['lru_w_x'][j],
                                      p['lru_b_x'][j], p['lru_lam'][j], p['lru_w_out'][j])
            new_lb.append(nb)
            new_lh.append(hl)
        x = layer_norm(DEEPNORM_ALPHA * x + (1 + g1) * out, p['ln_g'][i, 0], p['ln_b'][i, 0])
        hf = x * (1 + sc2) + sh2
        if i % 2 == 0:
            f = swiglu(hf, p['ffn_w_gu'][i // 2], p['ffn_w_down'][i // 2])
        else:
            f = moe_swiglu(hf, p['moe_w_router'][i // 2], p['moe_w_gu'][i // 2], p['moe_w_down'][i // 2])
        x = layer_norm(DEEPNORM_ALPHA * x + (1 + g2) * f, p['ln_g'][i, 1], p['ln_b'][i, 1])
    return (x, jnp.stack(new_k), jnp.stack(new_v), jnp.stack(new_lf),
            jnp.stack(new_sc), jnp.stack(new_lb), jnp.stack(new_lh))


def setup_inputs(seed: int = 0) -> dict:
    key = jax.random.key(seed)
    keys = jax.random.split(key, 36)
    n_pages = PAST_LEN // PAGE_SIZE
    n_used = DEC_BATCH * n_pages
    n_pool = n_used + n_used // 4
    d = D_MODEL
    hh = FOX_HEADS
    hd = FOX_HEAD_DIM
    r = LRU_WIDTH

    def nrm(idx, shape, scale=1.0):
        return scale * jax.random.normal(keys[idx], shape, jnp.float32)

    a_pow = jax.random.uniform(keys[28], (N_LRU_LAYERS, r), jnp.float32, 0.9, 0.999)
    a_base = a_pow ** (1.0 / LRU_C)
    page_table = jax.random.permutation(keys[8], n_pool)[:n_used].reshape(DEC_BATCH, n_pages).astype(jnp.int32)
    return {
        'x_prompt': nrm(0, (BATCH, SEQ, d)),
        'x_sample': nrm(1, (DEC_BATCH, DEC_SEQ, d)),
        'cache_k': nrm(2, (N_FOX_LAYERS, n_pool, PAGE_SIZE, hh, hd)),
        'cache_v': nrm(3, (N_FOX_LAYERS, n_pool, PAGE_SIZE, hh, hd)),
        'cache_lf': jax.nn.log_sigmoid(FORGET_BIAS + nrm(4, (N_FOX_LAYERS, n_pool, PAGE_SIZE, hh), 0.5)),
        'state_conv_b': nrm(5, (N_SC_LAYERS, DEC_BATCH, SC_WIDTH - 1, d)),
        'state_conv_c': nrm(6, (N_LRU_LAYERS, DEC_BATCH, LRU_CONV_WIDTH - 1, r)),
        'state_h': nrm(7, (N_LRU_LAYERS, DEC_BATCH, r), 0.5),
        'page_table': page_table,
        'c_prompt': nrm(9, (BATCH, d)),
        'c_sample': nrm(10, (DEC_BATCH, d)),
        'w_ada': nrm(11, (DEPTH, d, 6 * d), 0.5 * d ** -0.5),
        'b_ada': nrm(12, (DEPTH, 6 * d), 0.02),
        'ln_g': 1.0 + nrm(13, (DEPTH, 2, d), 0.05),
        'ln_b': nrm(14, (DEPTH, 2, d), 0.02),
        'fox_w_in': nrm(15, (N_FOX_LAYERS, d, 3 * d + hh), d ** -0.5),
        'fox_b_f': FORGET_BIAS + nrm(16, (N_FOX_LAYERS, hh), 0.5),
        'fox_w_out': nrm(17, (N_FOX_LAYERS, d, d), DEEPNORM_BETA * d ** -0.5),
        'sc_w_in': nrm(18, (N_SC_LAYERS, d, 3 * d), d ** -0.5),
        'sc_w_conv': nrm(19, (N_SC_LAYERS, SC_WIDTH, d), SC_WIDTH ** -0.5),
        'sc_w_out': nrm(20, (N_SC_LAYERS, d, d), DEEPNORM_BETA * d ** -0.5),
        'lru_w_in': nrm(21, (N_LRU_LAYERS, d, 2 * r), d ** -0.5),
        'lru_conv_w': nrm(22, (N_LRU_LAYERS, LRU_CONV_WIDTH, r), LRU_CONV_WIDTH ** -0.5),
        'lru_conv_b': nrm(23, (N_LRU_LAYERS, r), 0.02),
        'lru_w_a': nrm(24, (N_LRU_LAYERS, LRU_BLOCKS, LRU_BLOCK_DIM, LRU_BLOCK_DIM), LRU_BLOCK_DIM ** -0.5),
        'lru_b_a': nrm(25, (N_LRU_LAYERS, r), 0.1),
        'lru_w_x': nrm(26, (N_LRU_LAYERS, LRU_BLOCKS, LRU_BLOCK_DIM, LRU_BLOCK_DIM), LRU_BLOCK_DIM ** -0.5),
        'lru_b_x': nrm(27, (N_LRU_LAYERS, r), 0.1),
        'lru_lam': jnp.log(a_base) - jnp.log1p(-a_base),
        'lru_w_out': nrm(29, (N_LRU_LAYERS, r, d), DEEPNORM_BETA * r ** -0.5),
        'ffn_w_gu': nrm(30, (N_DENSE_LAYERS, d, 2 * FF_DENSE), d ** -0.5),
        'ffn_w_down': nrm(31, (N_DENSE_LAYERS, FF_DENSE, d), DEEPNORM_BETA * FF_DENSE ** -0.5),
        'moe_w_router': nrm(32, (N_MOE_LAYERS, d, N_EXPERTS), d ** -0.5),
        'moe_w_gu': nrm(33, (N_MOE_LAYERS, N_EXPERTS, d, 2 * FF_EXPERT), d ** -0.5),
        'moe_w_down': nrm(34, (N_MOE_LAYERS, N_EXPERTS, FF_EXPERT, d), DEEPNORM_BETA * FF_EXPERT ** -0.5),
    }


def reference(x_prompt, x_sample, cache_k, cache_v, cache_lf, state_conv_b, state_conv_c, state_h,
              page_table, c_prompt, c_sample, w_ada, b_ada, ln_g, ln_b, fox_w_in, fox_b_f, fox_w_out,
              sc_w_in, sc_w_conv, sc_w_out, lru_w_in, lru_conv_w, lru_conv_b, lru_w_a, lru_b_a, lru_w_x,
              lru_b_x, lru_lam, lru_w_out, ffn_w_gu, ffn_w_down, moe_w_router, moe_w_gu, moe_w_down):
    p = {'w_ada': w_ada, 'b_ada': b_ada, 'ln_g': ln_g, 'ln_b': ln_b,
         'fox_w_in': fox_w_in, 'fox_b_f': fox_b_f, 'fox_w_out': fox_w_out,
         'sc_w_in': sc_w_in, 'sc_w_conv': sc_w_conv, 'sc_w_out': sc_w_out,
         'lru_w_in': lru_w_in, 'lru_conv_w': lru_conv_w, 'lru_conv_b': lru_conv_b,
         'lru_w_a': lru_w_a, 'lru_b_a': lru_b_a, 'lru_w_x': lru_w_x, 'lru_b_x': lru_b_x,
         'lru_lam': lru_lam, 'lru_w_out': lru_w_out, 'ffn_w_gu': ffn_w_gu, 'ffn_w_down': ffn_w_down,
         'moe_w_router': moe_w_router, 'moe_w_gu': moe_w_gu, 'moe_w_down': moe_w_down}
    dt = x_prompt.dtype
    bp = x_prompt.shape[0]
    empty_kv = jnp.zeros((bp, 0, FOX_HEADS, FOX_HEAD_DIM), dt)
    empty_lf = jnp.zeros((bp, 0, FOX_HEADS), jnp.float32)
    sc0 = jnp.zeros((N_SC_LAYERS, bp, SC_WIDTH - 1, D_MODEL), dt)
    lb0 = jnp.zeros((N_LRU_LAYERS, bp, LRU_CONV_WIDTH - 1, LRU_WIDTH), dt)
    lh0 = jnp.zeros((N_LRU_LAYERS, bp, LRU_WIDTH), jnp.float32)
    y_prompt, k_p, v_p, lf_p, cb_p, cc_p, h_p = trunk(
        x_prompt, c_prompt, lambda j: (empty_kv, empty_kv, empty_lf), sc0, lb0, lh0, p)
    sample_past = lambda j: (gather_pages(cache_k[j], page_table), gather_pages(cache_v[j], page_table),
                             gather_pages(cache_lf[j], page_table))
    y_sample, k_s, v_s, lf_s, cb_s, cc_s, h_s = trunk(
        x_sample, c_sample, sample_past, state_conv_b, state_conv_c, state_h, p)
    return (y_prompt, y_sample, k_p, v_p, lf_p, cb_p, cc_p, h_p, k_s, v_s, lf_s, cb_s, cc_s, h_s)
```

```python
import functools

import jax
import jax.numpy as jnp
from jax import lax
from jax.experimental import pallas as pl
from jax.experimental.pallas import tpu as pltpu

BF = jnp.bfloat16
F32 = jnp.float32

LANE_V7X = 128
SUBLANE_V7X = 8
VMEM_LIMIT_V7X = 52 * 1024 * 1024

FOX_HEAD_DIM = 64
Q_SCALE = FOX_HEAD_DIM ** -0.5
LRU_C = 8.0
LN_EPS = 1e-5
N_MIXERS = 3
TOP_K = 2
PAGES_PER_STEP = 4


def _cparams(sem):
    return pltpu.CompilerParams(dimension_semantics=sem, vmem_limit_bytes=VMEM_LIMIT_V7X)


def _dot(a, b):
    return jnp.dot(a, b, preferred_element_type=F32)


def _dot_nt(a, b):
    return lax.dot_general(a, b, (((1,), (1,)), ((), ())), preferred_element_type=F32)


def _split(a):
    hi = a.astype(BF)
    lo = (a - hi.astype(F32)).astype(BF)
    return hi, lo


def _dot3(a, b):
    ah, al = _split(a)
    bh, bl = _split(b)
    return _dot(ah, bh) + (_dot(ah, bl) + _dot(al, bh))


def _pick(n, cands):
    for c in cands:
        if n % c == 0:
            return c
    return n


class _Group:
    def __init__(self, n_rows, tm, rows_per_seq, stride, n_seq):
        self.n = n_rows
        self.tm = tm
        self.nm = n_rows // tm
        self.rps = rows_per_seq
        self.g = 1 if stride == 1 else tm
        self.stride = stride
        self.tps = rows_per_seq // tm
        self.n_seq = n_seq

    def mod_spec(self, d, m_axis, n_axes):
        tm, rps = self.tm, self.rps

        def idx(*ids):
            return ((ids[m_axis] * tm) // rps, 0, 0)
        return pl.BlockSpec((None, self.g, d), idx)


def _ada_kernel(c_ref, w_ref, b_ref, o_ref):
    c = c_ref[...]
    a = c * jax.nn.sigmoid(c)
    o_ref[...] = _dot3(a, w_ref[...]) + b_ref[...]


def _ada(c_all, w_ada, b_ada):
    depth, d, d6 = w_ada.shape
    s = c_all.shape[0]
    tn = _pick(d6, (1024, 512, 256, 128))
    return pl.pallas_call(
        _ada_kernel,
        out_shape=jax.ShapeDtypeStruct((depth, s, d6), F32),
        grid=(depth, d6 // tn),
        in_specs=[pl.BlockSpec((s, d), lambda l, n: (0, 0)),
                  pl.BlockSpec((None, d, tn), lambda l, n: (l, 0, n)),
                  pl.BlockSpec((None, 1, tn), lambda l, n: (l, 0, n))],
        out_specs=pl.BlockSpec((None, s, tn), lambda l, n: (l, 0, n)),
        compiler_params=_cparams(("arbitrary", "arbitrary")),
        name="ada_mod",
    )(c_all, w_ada, b_ada.reshape(depth, 1, d6))


def _modulate_kernel(x_ref, sc_ref, sh_ref, o_ref):
    o_ref[...] = (x_ref[...] * (1.0 + sc_ref[...]) + sh_ref[...]).astype(BF)


def _modulate(grp, x, sc, sh):
    n, d = x.shape
    tm = grp.tm
    return pl.pallas_call(
        _modulate_kernel,
        out_shape=jax.ShapeDtypeStruct((n, d), BF),
        grid=(grp.nm,),
        in_specs=[pl.BlockSpec((tm, d), lambda m: (m, 0)),
                  grp.mod_spec(d, 0, 1), grp.mod_spec(d, 0, 1)],
        out_specs=pl.BlockSpec((tm, d), lambda m: (m, 0)),
        compiler_params=_cparams(("arbitrary",)),
        name="modulate",
    )(x, sc, sh)


def _linear_kernel(x_ref, w_ref, *refs, scale, n_out):
    outs, wb = refs[:n_out], refs[n_out]

    @pl.when(pl.program_id(1) == 0)
    def _():
        wb[...] = w_ref[...].astype(BF)

    y = _dot(x_ref[...], wb[...])
    if scale != 1.0:
        y = y * scale
    for o in outs:
        o[...] = y.astype(o.dtype)


def _linear(grp, x, w3, layer, col0, n_cols, out_dtypes, scale=1.0):
    n, k = x.shape
    tm = grp.tm
    tn = _pick(n_cols, (512, 256, 128))
    assert col0 % tn == 0
    cb0 = col0 // tn
    outs = [jax.ShapeDtypeStruct((n, n_cols), dt) for dt in out_dtypes]
    return pl.pallas_call(
        functools.partial(_linear_kernel, scale=scale, n_out=len(outs)),
        out_shape=outs,
        grid=(n_cols // tn, grp.nm),
        in_specs=[pl.BlockSpec((tm, k), lambda c, m: (m, 0)),
                  pl.BlockSpec((None, k, tn), lambda c, m: (layer, 0, cb0 + c))],
        out_specs=[pl.BlockSpec((tm, tn), lambda c, m: (m, c)) for _ in outs],
        scratch_shapes=[pltpu.VMEM((k, tn), BF)],
        compiler_params=_cparams(("arbitrary", "arbitrary")),
        name="linear",
    )(x, w3)


def _log_sigmoid(z):
    return jnp.minimum(z, 0.0) - jnp.log1p(jnp.exp(-jnp.abs(z)))


def _forget_kernel(x_ref, w_ref, b_ref, o_ref):
    z = _dot(x_ref[...], w_ref[...].astype(BF)) + b_ref[...]
    o_ref[...] = _log_sigmoid(z)


def _forget(grp, x, w_f, b_f):
    n, k = x.shape
    h = w_f.shape[1]
    tm = grp.tm
    return pl.pallas_call(
        _forget_kernel,
        out_shape=jax.ShapeDtypeStruct((n, h), F32),
        grid=(grp.nm,),
        in_specs=[pl.BlockSpec((tm, k), lambda m: (m, 0)),
                  pl.BlockSpec((k, h), lambda m: (0, 0)),
                  pl.BlockSpec((1, h), lambda m: (0, 0))],
        out_specs=pl.BlockSpec((tm, h), lambda m: (m, 0)),
        compiler_params=_cparams(("arbitrary",)),
        name="forget_gate",
    )(x, w_f, b_f.reshape(1, h))


def _prefix_lanes(x):
    n = x.shape[-1]
    lane = lax.broadcasted_iota(jnp.int32, x.shape, x.ndim - 1)
    d = 1
    while d < n:
        x = x + jnp.where(lane >= d, pltpu.roll(x, d, x.ndim - 1), 0.0)
        d *= 2
    return x


def _cumsum_kernel(x_ref, o_ref):
    o_ref[...] = _prefix_lanes(x_ref[...])


def _cumsum_time(lf_t):
    b, h, t = lf_t.shape
    return pl.pallas_call(
        _cumsum_kernel,
        out_shape=jax.ShapeDtypeStruct((b, h, t), F32),
        grid=(b,),
        in_specs=[pl.BlockSpec((None, h, t), lambda i: (i, 0, 0))],
        out_specs=pl.BlockSpec((None, h, t), lambda i: (i, 0, 0)),
        compiler_params=_cparams(("arbitrary",)),
        name="forget_cumsum",
    )(lf_t)


def _attn_kernel(q_ref, k_ref, v_ref, f_ref, o_ref, m_sc, l_sc, acc_sc, *, tq):
    qi = pl.program_id(2)
    half = LANE_V7X // 2
    lane = lax.broadcasted_iota(jnp.int32, (tq, LANE_V7X), 1)
    lo = lane < half
    q = q_ref[...]
    zero = jnp.zeros_like(q)
    qs = (jnp.where(lo, q, zero), jnp.where(lo, zero, q))
    for h in range(2):
        m_sc[h] = jnp.full((tq, LANE_V7X), -jnp.inf, F32)
        l_sc[h] = jnp.zeros((tq, LANE_V7X), F32)
        acc_sc[h] = jnp.zeros((tq, LANE_V7X), F32)

    def block(k0, masked):
        kb = k_ref[pl.ds(k0, tq), :]
        vb = v_ref[pl.ds(k0, tq), :]
        for h in range(2):
            s = _dot_nt(qs[h], kb) - f_ref[h:h + 1, pl.ds(k0, tq)]
            if masked:
                row = lax.broadcasted_iota(jnp.int32, (tq, tq), 0)
                col = lax.broadcasted_iota(jnp.int32, (tq, tq), 1)
                s = jnp.where(col <= row, s, -jnp.inf)
            m_prev = m_sc[h]
            m_new = jnp.maximum(m_prev, jnp.max(s, axis=1, keepdims=True))
            alpha = jnp.exp(m_prev - m_new)
            p = jnp.exp(s - m_new[:, 0:1])
            l_sc[h] = alpha * l_sc[h] + jnp.sum(p, axis=1, keepdims=True)
            acc_sc[h] = alpha * acc_sc[h] + _dot(p.astype(BF), vb)
            m_sc[h] = m_new

    def body(i, carry):
        block(pl.multiple_of(i * tq, tq), False)
        return carry

    lax.fori_loop(0, qi, body, 0)
    block(pl.multiple_of(qi * tq, tq), True)
    o = jnp.where(lo, acc_sc[0] / l_sc[0], acc_sc[1] / l_sc[1])
    o_ref[...] = o.astype(o_ref.dtype)


def _attention_prompt(q, k, v, fcum, b, t):
    n, d = q.shape
    hp = d // LANE_V7X
    tq = _pick(t, (512, 256, 128))
    nq = t // tq
    return pl.pallas_call(
        functools.partial(_attn_kernel, tq=tq),
        out_shape=jax.ShapeDtypeStruct((n, d), BF),
        grid=(b, hp, nq),
        in_specs=[pl.BlockSpec((tq, LANE_V7X), lambda i, j, qi: (i * nq + qi, j)),
                  pl.BlockSpec((t, LANE_V7X), lambda i, j, qi: (i, j)),
                  pl.BlockSpec((t, LANE_V7X), lambda i, j, qi: (i, j)),
                  pl.BlockSpec((None, None, 2, t), lambda i, j, qi: (i, j, 0, 0))],
        out_specs=pl.BlockSpec((tq, LANE_V7X), lambda i, j, qi: (i * nq + qi, j)),
        scratch_shapes=[pltpu.VMEM((2, tq, LANE_V7X), F32)] * 3,
        compiler_params=_cparams(("arbitrary", "arbitrary", "arbitrary")),
        name="fox_attention_prompt",
    )(q, k, v, fcum)


def _decode_kernel(pt_ref, q_ref, kn_ref, vn_ref, lfn_ref, *refs, pp, dt, nh, hd):
    k_refs = refs[:pp]
    v_refs = refs[pp:2 * pp]
    lf_refs = refs[2 * pp:3 * pp]
    o_ref, qbd, m_sc, l_sc, acc_sc, c_sc = refs[3 * pp:]
    g = pl.program_id(1)
    ng = pl.num_programs(1)
    r = dt * nh
    d = nh * hd
    head_of_row = lax.rem(lax.broadcasted_iota(jnp.int32, (r, d), 0), nh)
    head_of_lane = lax.div(lax.broadcasted_iota(jnp.int32, (r, d), 1), hd)
    own = head_of_row == head_of_lane

    @pl.when(g == 0)
    def _():
        q = q_ref[...] * Q_SCALE
        qrep = jnp.concatenate([jnp.broadcast_to(q[t:t + 1], (nh, d)) for t in range(dt)], axis=0)
        qbd[...] = jnp.where(own, qrep, 0.0).astype(BF)
        m_sc[...] = jnp.full(m_sc.shape, -jnp.inf, F32)
        l_sc[...] = jnp.zeros(l_sc.shape, F32)
        acc_sc[...] = jnp.zeros(acc_sc.shape, F32)
        c_sc[...] = jnp.zeros(c_sc.shape, F32)

    def update(s, vs):
        m_prev = m_sc[...]
        m_new = jnp.maximum(m_prev, jnp.max(s, axis=1, keepdims=True))
        alpha = jnp.exp(m_prev - m_new)
        p = jnp.exp(s - m_new[:, 0:1])
        l_sc[...] = alpha * l_sc[...] + jnp.sum(p, axis=1, keepdims=True)
        w = vs[0].shape[0]
        pv = _dot(p[:, 0:w].astype(BF), vs[0])
        for i in range(1, len(vs)):
            pv = pv + _dot(p[:, i * w:(i + 1) * w].astype(BF), vs[i])
        acc_sc[...] = alpha[:, 0:1] * acc_sc[...] + pv
        m_sc[...] = m_new

    qb = qbd[...]
    lf_all = jnp.concatenate([lf_refs[i][...] for i in range(pp)], axis=1)
    cum = _prefix_lanes(lf_all) + c_sc[:, 0:1]
    c_sc[...] = c_sc[...] + jnp.sum(lf_all, axis=1, keepdims=True)
    s = jnp.concatenate([_dot_nt(qb, k_refs[i][...].astype(BF)) for i in range(pp)], axis=1)
    s = s - jnp.concatenate([cum] * dt, axis=0)
    update(s, [v_refs[i][...].astype(BF) for i in range(pp)])

    @pl.when(g == ng - 1)
    def _():
        nk = kn_ref.shape[0]
        cum_n = _prefix_lanes(lfn_ref[...]) + c_sc[:, 0:1]
        bias = jnp.concatenate([cum_n[:, 0:nk]] * dt, axis=0)
        sn = _dot_nt(qb, kn_ref[...].astype(BF)) - bias
        trow = lax.div(lax.broadcasted_iota(jnp.int32, (r, nk), 0), nh)
        col = lax.broadcasted_iota(jnp.int32, (r, nk), 1)
        sn = jnp.where(col <= trow, sn, -jnp.inf)
        update(sn, [vn_ref[...].astype(BF)])
        o = jnp.where(own, acc_sc[...] / l_sc[:, 0:1], 0.0)
        o_ref[...] = jnp.sum(o.reshape(dt, nh, d), axis=1).astype(o_ref.dtype)


def _attention_sample(layer, page_table, q_b, k_new, v_new, lf_new_t, cache_k, cache_v, cache_lf_t):
    db, dt, d = q_b.shape
    nk = k_new.shape[1]
    page = cache_k.shape[2]
    nh = cache_lf_t.shape[2]
    n_pages = page_table.shape[1]
    pp = _pick(n_pages, (PAGES_PER_STEP, 2, 1))
    r = dt * nh

    def page_spec(shape, i):
        return pl.BlockSpec((None, None) + shape, lambda b, g, pt: (layer, pt[b, g * pp + i], 0, 0))

    in_specs = [pl.BlockSpec((None, dt, d), lambda b, g, pt: (b, 0, 0)),
                pl.BlockSpec((None, nk, d), lambda b, g, pt: (b, 0, 0)),
                pl.BlockSpec((None, nk, d), lambda b, g, pt: (b, 0, 0)),
                pl.BlockSpec((None, nh, LANE_V7X), lambda b, g, pt: (b, 0, 0))]
    in_specs += [page_spec((page, d), i) for i in range(pp)]
    in_specs += [page_spec((page, d), i) for i in range(pp)]
    in_specs += [page_spec((nh, page), i) for i in range(pp)]
    grid_spec = pltpu.PrefetchScalarGridSpec(
        num_scalar_prefetch=1,
        grid=(db, n_pages // pp),
        in_specs=in_specs,
        out_specs=pl.BlockSpec((None, dt, d), lambda b, g, pt: (b, 0, 0)),
        scratch_shapes=[pltpu.VMEM((r, d), BF),
                        pltpu.VMEM((r, LANE_V7X), F32),
                        pltpu.VMEM((r, LANE_V7X), F32),
                        pltpu.VMEM((r, d), F32),
                        pltpu.VMEM((nh, LANE_V7X), F32)])
    return pl.pallas_call(
        functools.partial(_decode_kernel, pp=pp, dt=dt, nh=nh, hd=d // nh),
        out_shape=jax.ShapeDtypeStruct((db, dt, d), BF),
        grid_spec=grid_spec,
        compiler_params=_cparams(("arbitrary", "arbitrary")),
        name="fox_attention_sample",
    )(page_table, q_b, k_new, v_new, lf_new_t,
      *([cache_k] * pp), *([cache_v] * pp), *([cache_lf_t] * pp))


def _conv_tile(e_ref, carry_ref, init_ref, tail_ref, u, w_ref, first, *, hb, stride, width):
    tm = u.shape[0]

    @pl.when(first)
    def _():
        e_ref[0:hb, :] = init_ref[...]

    @pl.when(jnp.logical_not(first))
    def _():
        e_ref[0:hb, :] = carry_ref[...]

    e_ref[hb:hb + tm, :] = u
    y = None
    for j in range(width):
        off = hb - (width - 1 - j) * stride
        term = w_ref[j:j + 1, :] * e_ref[off:off + tm, :]
        y = term if y is None else y + term
    tail = e_ref[tm:tm + hb, :]
    carry_ref[...] = tail
    tail_ref[...] = tail
    return y


def _sc_kernel(x_ref, wb_ref, wc_ref, wv_ref, wconv_ref, init_ref, mix_ref, tail_ref,
               wb, e_sc, carry, *, tps, hb, stride, width):
    m = pl.program_id(1)

    @pl.when(m == 0)
    def _():
        wb[0] = wb_ref[...].astype(BF)
        wb[1] = wc_ref[...].astype(BF)
        wb[2] = wv_ref[...].astype(BF)

    x = x_ref[...]
    g_b = _dot(x, wb[0])
    u = _dot(x, wb[1]) * _dot(x, wb[2])
    y = _conv_tile(e_sc, carry, init_ref, tail_ref, u, wconv_ref, (m % tps) == 0,
                   hb=hb, stride=stride, width=width)
    mix_ref[...] = (g_b * y).astype(mix_ref.dtype)


def _short_conv(grp, x, w_in, layer, w_conv, init):
    n, d = x.shape
    tm = grp.tm
    width = w_conv.shape[0]
    hb = init.shape[1]
    tn = _pick(d, (256, 128))
    nb = d // tn
    tps = grp.tps
    n_init = init.shape[0]

    def wspec(part):
        return pl.BlockSpec((None, d, tn), lambda c, m: (layer, 0, part * nb + c))

    def seq_of(m):
        return m // tps

    return pl.pallas_call(
        functools.partial(_sc_kernel, tps=tps, hb=hb, stride=grp.stride, width=width),
        out_shape=[jax.ShapeDtypeStruct((n, d), BF),
                   jax.ShapeDtypeStruct((grp.n_seq, hb, d), F32)],
        grid=(nb, grp.nm),
        in_specs=[pl.BlockSpec((tm, d), lambda c, m: (m, 0)),
                  wspec(0), wspec(1), wspec(2),
                  pl.BlockSpec((width, tn), lambda c, m: (0, c)),
                  pl.BlockSpec((None, hb, tn), lambda c, m: (seq_of(m) % n_init, 0, c))],
        out_specs=[pl.BlockSpec((tm, tn), lambda c, m: (m, c)),
                   pl.BlockSpec((None, hb, tn), lambda c, m: (seq_of(m), 0, c))],
        scratch_shapes=[pltpu.VMEM((3, d, tn), BF),
                        pltpu.VMEM((hb + tm, tn), F32),
                        pltpu.VMEM((hb, tn), F32)],
        compiler_params=_cparams(("arbitrary", "arbitrary")),
        name="short_conv_mixer",
    )(x, w_in, w_in, w_in, w_conv, init)


def _softplus(x):
    return jnp.maximum(x, 0.0) + jnp.log1p(jnp.exp(-jnp.abs(x)))


def _neg_expm1(x):
    th = jnp.tanh(0.5 * x)
    return -2.0 * th / (1.0 - th)


def _scan_rows(a, u):
    tm = a.shape[0]
    row = lax.broadcasted_iota(jnp.int32, a.shape, 0)
    d = 1
    while d < tm:
        keep = row >= d
        a_s = jnp.where(keep, pltpu.roll(a, d, 0), 1.0)
        u_s = jnp.where(keep, pltpu.roll(u, d, 0), 0.0)
        u = u + a * u_s
        a = a * a_s
        d *= 2
    return a, u


def _lru_kernel(x_ref, wg_ref, wr_ref, cw_ref, cb_ref, bda_ref, bdx_ref, ba_ref, bx_ref, lam_ref,
                init_ref, h0_ref, mix_ref, tail_ref, hlast_ref,
                wb, bdb, e_sc, carry, hcarry, *, tps, hb, stride, width):
    m = pl.program_id(1)
    tm = x_ref.shape[0]

    @pl.when(m == 0)
    def _():
        wb[0] = wg_ref[...].astype(BF)
        wb[1] = wr_ref[...].astype(BF)
        bdb[0] = bda_ref[...].astype(BF)
        bdb[1] = bdx_ref[...].astype(BF)

    first = (m % tps) == 0
    x = x_ref[...]
    gate = jax.nn.gelu(_dot(x, wb[0]), approximate=True)
    xr = _dot(x, wb[1])
    xc = _conv_tile(e_sc, carry, init_ref, tail_ref, xr, cw_ref, first,
                    hb=hb, stride=stride, width=width) + cb_ref[...]
    xcb = xc.astype(BF)
    r = jax.nn.sigmoid(_dot(xcb, bdb[0]) + ba_ref[...])
    i = jax.nn.sigmoid(_dot(xcb, bdb[1]) + bx_ref[...])
    log_a = (-LRU_C) * r * _softplus(-lam_ref[...])
    a = jnp.exp(log_a)
    u = jnp.sqrt(_neg_expm1(2.0 * log_a)) * i * xc

    @pl.when(first)
    def _():
        hcarry[...] = h0_ref[...]

    if stride == 1:
        a_cum, h_loc = _scan_rows(a, u)
        hs = h_loc + a_cum * hcarry[0:1, :]
        mix_ref[...] = (gate * hs).astype(mix_ref.dtype)
        h_end = jnp.broadcast_to(hs[tm - 1:tm, :], hcarry.shape)
    else:
        h = hcarry[...]
        for t in range(tm // stride):
            sl = slice(t * stride, (t + 1) * stride)
            h = a[sl] * h + u[sl]
            mix_ref[sl, :] = (gate[sl] * h).astype(mix_ref.dtype)
        h_end = h
    hcarry[...] = h_end
    hlast_ref[...] = h_end


def _block_diag(w, per):
    nblk, bd, _ = w.shape
    eye = jnp.eye(per, dtype=w.dtype)
    return jnp.einsum('cide,ik->cidke', w.reshape(nblk // per, per, bd, bd), eye).reshape(
        nblk // per, per * bd, per * bd)


def _rglru(grp, x, w_in, layer, conv_w, conv_b, w_a, b_a, w_x, b_x, lam, init, h0):
    n, d = x.shape
    rw = conv_w.shape[1]
    tm = grp.tm
    width = conv_w.shape[0]
    hb = init.shape[1]
    hr = h0.shape[1]
    bd = w_a.shape[1]
    tn = _pick(rw, (256, 128))
    nb = rw // tn
    tps = grp.tps
    n_init = init.shape[0]
    bda = _block_diag(w_a, tn // bd)
    bdx = _block_diag(w_x, tn // bd)

    def row_spec():
        return pl.BlockSpec((1, tn), lambda c, m: (0, c))

    def seq_of(m):
        return m // tps

    return pl.pallas_call(
        functools.partial(_lru_kernel, tps=tps, hb=hb, stride=grp.stride, width=width),
        out_shape=[jax.ShapeDtypeStruct((n, rw), BF),
                   jax.ShapeDtypeStruct((grp.n_seq, hb, rw), F32),
                   jax.ShapeDtypeStruct((grp.n_seq, hr, rw), F32)],
        grid=(nb, grp.nm),
        in_specs=[pl.BlockSpec((tm, d), lambda c, m: (m, 0)),
                  pl.BlockSpec((None, d, tn), lambda c, m: (layer, 0, c)),
                  pl.BlockSpec((None, d, tn), lambda c, m: (layer, 0, nb + c)),
                  pl.BlockSpec((width, tn), lambda c, m: (0, c)),
                  row_spec(),
                  pl.BlockSpec((None, tn, tn), lambda c, m: (c, 0, 0)),
                  pl.BlockSpec((None, tn, tn), lambda c, m: (c, 0, 0)),
                  row_spec(), row_spec(), row_spec(),
                  pl.BlockSpec((None, hb, tn), lambda c, m: (seq_of(m) % n_init, 0, c)),
                  pl.BlockSpec((None, hr, tn), lambda c, m: (seq_of(m) % n_init, 0, c))],
        out_specs=[pl.BlockSpec((tm, tn), lambda c, m: (m, c)),
                   pl.BlockSpec((None, hb, tn), lambda c, m: (seq_of(m), 0, c)),
                   pl.BlockSpec((None, hr, tn), lambda c, m: (seq_of(m), 0, c))],
        scratch_shapes=[pltpu.VMEM((2, d, tn), BF),
                        pltpu.VMEM((2, tn, tn), BF),
                        pltpu.VMEM((hb + tm, tn), F32),
                        pltpu.VMEM((hb, tn), F32),
                        pltpu.VMEM((hr, tn), F32)],
        compiler_params=_cparams(("arbitrary", "arbitrary")),
        name="rglru_mixer",
    )(x, w_in, w_in, conv_w, conv_b.reshape(1, rw), bda, bdx, b_a.reshape(1, rw),
      b_x.reshape(1, rw), lam.reshape(1, rw), init, h0)


def _post_ln(x, f, gate, ln_g, ln_b, alpha):
    z = alpha * x + (1.0 + gate) * f
    mu = jnp.mean(z, axis=-1, keepdims=True)
    zc = z - mu
    var = jnp.mean(zc * zc, axis=-1, keepdims=True)
    return zc * lax.rsqrt(var + LN_EPS) * ln_g + ln_b


def _top2_gates(logits):
    ne = float(logits.shape[-1])
    lane = lax.broadcasted_iota(jnp.int32, logits.shape, 1).astype(F32)
    m1 = jnp.max(logits, axis=1, keepdims=True)
    i1 = jnp.min(jnp.where(logits == m1, lane, ne), axis=1, keepdims=True)
    sel1 = lane == i1
    rest = jnp.where(sel1, -jnp.inf, logits)
    m2 = jnp.max(rest, axis=1, keepdims=True)
    i2 = jnp.min(jnp.where(rest == m2, lane, ne), axis=1, keepdims=True)
    sel2 = lane == i2
    e2 = jnp.exp(m2 - m1)
    w1 = 1.0 / (1.0 + e2)
    w2 = e2 / (1.0 + e2)
    return jnp.where(sel1, w1, 0.0) + jnp.where(sel2, w2, 0.0)


def _out_ln_kernel(a_ref, w_ref, x_ref, g_ref, sc_ref, sh_ref, lng_ref, lnb_ref, *refs,
                   alpha, has_router):
    if has_router:
        wr_ref, x1_ref, hf_ref, gates_ref, wb = refs
    else:
        x1_ref, hf_ref, wb = refs

    @pl.when(pl.program_id(0) == 0)
    def _():
        wb[...] = w_ref[...].astype(BF)

    y = _dot(a_ref[...], wb[...])
    x1 = _post_ln(x_ref[...], y, g_ref[...], lng_ref[...], lnb_ref[...], alpha)
    x1_ref[...] = x1
    hf = x1 * (1.0 + sc_ref[...]) + sh_ref[...]
    hf_ref[...] = hf.astype(hf_ref.dtype)
    if has_router:
        gates_ref[...] = _top2_gates(_dot3(hf, wr_ref[...]))


def _out_ln(grp, a, w3, layer, x, gate, sc, sh, ln_g, ln_b, alpha, w_router=None):
    n, d = x.shape
    k = a.shape[1]
    tm = grp.tm
    has_router = w_router is not None
    in_specs = [pl.BlockSpec((tm, k), lambda m: (m, 0)),
                pl.BlockSpec((None, k, d), lambda m: (layer, 0, 0)),
                pl.BlockSpec((tm, d), lambda m: (m, 0)),
                grp.mod_spec(d, 0, 1), grp.mod_spec(d, 0, 1), grp.mod_spec(d, 0, 1),
                pl.BlockSpec((1, d), lambda m: (0, 0)),
                pl.BlockSpec((1, d), lambda m: (0, 0))]
    args = [a, w3, x, gate, sc, sh, ln_g.reshape(1, d), ln_b.reshape(1, d)]
    out_shape = [jax.ShapeDtypeStruct((n, d), F32), jax.ShapeDtypeStruct((n, d), BF)]
    out_specs = [pl.BlockSpec((tm, d), lambda m: (m, 0)), pl.BlockSpec((tm, d), lambda m: (m, 0))]
    if has_router:
        ne = w_router.shape[1]
        in_specs.append(pl.BlockSpec((d, ne), lambda m: (0, 0)))
        args.append(w_router)
        out_shape.append(jax.ShapeDtypeStruct((n, ne), F32))
        out_specs.append(pl.BlockSpec((tm, ne), lambda m: (m, 0)))
    return pl.pallas_call(
        functools.partial(_out_ln_kernel, alpha=alpha, has_router=has_router),
        out_shape=out_shape,
        grid=(grp.nm,),
        in_specs=in_specs,
        out_specs=out_specs,
        scratch_shapes=[pltpu.VMEM((k, d), BF)],
        compiler_params=_cparams(("arbitrary",)),
        name="out_proj_ln",
    )(*args)


def _ffn_kernel(h_ref, wg_ref, wu_ref, wd_ref, x_ref, g_ref, sc_ref, sh_ref, lng_ref, lnb_ref, *refs,
                alpha, has_gates, has_next):
    refs = list(refs)
    gates_ref = refs.pop(0) if has_gates else None
    x2_ref = refs.pop(0)
    hn_ref = refs.pop(0) if has_next else None
    acc = refs.pop(0)
    acc_e = refs.pop(0) if has_gates else None
    e = pl.program_id(1)
    ch = pl.program_id(2)
    ne = pl.num_programs(1)
    nch = pl.num_programs(2)

    @pl.when(jnp.logical_and(e == 0, ch == 0))
    def _():
        acc[...] = jnp.zeros(acc.shape, F32)

    h = h_ref[...]
    gu = _dot(h, wg_ref[...].astype(BF))
    up = _dot(h, wu_ref[...].astype(BF))
    act = (gu * jax.nn.sigmoid(gu) * up).astype(BF)
    part = _dot(act, wd_ref[...].astype(BF))
    if has_gates:
        @pl.when(ch == 0)
        def _():
            acc_e[...] = part

        @pl.when(ch > 0)
        def _():
            acc_e[...] += part

        @pl.when(ch == nch - 1)
        def _():
            gates = gates_ref[...]
            lane = lax.broadcasted_iota(jnp.int32, gates.shape, 1)
            ge = jnp.sum(jnp.where(lane == e, gates, 0.0), axis=1, keepdims=True)
            acc[...] += ge * acc_e[...]
    else:
        acc[...] += part

    @pl.when(jnp.logical_and(e == ne - 1, ch == nch - 1))
    def _():
        x2 = _post_ln(x_ref[...], acc[...], g_ref[...], lng_ref[...], lnb_ref[...], alpha)
        x2_ref[...] = x2
        if has_next:
            hn_ref[...] = (x2 * (1.0 + sc_ref[...]) + sh_ref[...]).astype(hn_ref.dtype)


def _ffn(grp, h, w_gu, w_down, layer, x, gate, sc, sh, ln_g, ln_b, alpha, gates=None, has_next=True):
    n, d = x.shape
    tm = grp.tm
    has_gates = gates is not None
    if has_gates:
        ne, ff = w_down.shape[1], w_down.shape[2]
        tf = _pick(ff, (512, 256, 128))
    else:
        ne, ff = 1, w_down.shape[1]
        tf = _pick(ff, (256, 128))
    nch = ff // tf

    if has_gates:
        wg_spec = pl.BlockSpec((None, None, d, tf), lambda m, e, c: (layer, e, 0, c))
        wu_spec = pl.BlockSpec((None, None, d, tf), lambda m, e, c: (layer, e, 0, nch + c))
        wd_spec = pl.BlockSpec((None, None, tf, d), lambda m, e, c: (layer, e, c, 0))
    else:
        wg_spec = pl.BlockSpec((None, d, tf), lambda m, e, c: (layer, 0, c))
        wu_spec = pl.BlockSpec((None, d, tf), lambda m, e, c: (layer, 0, nch + c))
        wd_spec = pl.BlockSpec((None, tf, d), lambda m, e, c: (layer, c, 0))
    row = pl.BlockSpec((tm, d), lambda m, e, c: (m, 0))
    vec = pl.BlockSpec((1, d), lambda m, e, c: (0, 0))
    in_specs = [row, wg_spec, wu_spec, wd_spec, row,
                grp.mod_spec(d, 0, 3), grp.mod_spec(d, 0, 3), grp.mod_spec(d, 0, 3), vec, vec]
    args = [h, w_gu, w_gu, w_down, x, gate, sc, sh, ln_g.reshape(1, d), ln_b.reshape(1, d)]
    if has_gates:
        in_specs.append(pl.BlockSpec((tm, ne), lambda m, e, c: (m, 0)))
        args.append(gates)
    out_shape = [jax.ShapeDtypeStruct((n, d), F32)]
    out_specs = [row]
    if has_next:
        out_shape.append(jax.ShapeDtypeStruct((n, d), BF))
        out_specs.append(row)
    scratch = [pltpu.VMEM((tm, d), F32)]
    if has_gates:
        scratch.append(pltpu.VMEM((tm, d), F32))
    res = pl.pallas_call(
        functools.partial(_ffn_kernel, alpha=alpha, has_gates=has_gates, has_next=has_next),
        out_shape=out_shape,
        grid=(grp.nm, ne, nch),
        in_specs=in_specs,
        out_specs=out_specs,
        scratch_shapes=scratch,
        compiler_params=_cparams(("arbitrary", "arbitrary", "arbitrary")),
        name="moe_dense" if has_gates else "ffn_dense",
    )(*args)
    return (res[0], res[1]) if has_next else (res[0], None)


def _to_time_major(a):
    a = jnp.swapaxes(a, 0, 1)
    return a.reshape((a.shape[0] * a.shape[1],) + a.shape[2:])


def _from_time_major(a, db):
    a = a.reshape((a.shape[0] // db, db) + a.shape[1:])
    return jnp.swapaxes(a, 0, 1)


def kernel(x_prompt, x_sample, cache_k, cache_v, cache_lf, state_conv_b, state_conv_c, state_h, page_table, c_prompt, c_sample, w_ada, b_ada, ln_g, ln_b, fox_w_in, fox_b_f, fox_w_out, sc_w_in, sc_w_conv, sc_w_out, lru_w_in, lru_conv_w, lru_conv_b, lru_w_a, lru_b_a, lru_w_x, lru_b_x, lru_lam, lru_w_out, ffn_w_gu, ffn_w_down, moe_w_router, moe_w_gu, moe_w_down):
    b, t, d = x_prompt.shape
    db, dt, _ = x_sample.shape
    depth = w_ada.shape[0]
    nh = fox_b_f.shape[1]
    hd = d // nh
    assert hd == FOX_HEAD_DIM and d % LANE_V7X == 0 and db % SUBLANE_V7X == 0
    alpha = (2.0 * depth) ** 0.25
    n_fox = cache_k.shape[0]
    n_pool, page = cache_k.shape[1], cache_k.shape[2]
    ns = db * dt

    gp = _Group(b * t, _pick(t, (1024, 512, 256, 128)), t, 1, b)
    gs = _Group(ns, ns, ns, db, 1)
    moe_groups = (_Group(b * t, _pick(t, (512, 256, 128)), t, 1, b), gs)

    mods = _ada(jnp.concatenate([c_prompt, c_sample], axis=0), w_ada, b_ada)
    mods_p = mods[:, :b].reshape(depth, b, 6, d)
    mods_s = jnp.tile(mods[:, b:].reshape(depth, db, 6, d), (1, dt, 1, 1))

    def mod(layer, which, comp):
        if which == 0:
            return mods_p[layer, :, comp][:, None, :]
        return mods_s[layer, :, comp][None]

    cache_k4 = cache_k.reshape(n_fox, n_pool, page, d)
    cache_v4 = cache_v.reshape(n_fox, n_pool, page, d)
    cache_lf_t = jnp.swapaxes(cache_lf, 2, 3)

    groups = (gp, gs)
    xs = [x_prompt.reshape(b * t, d), _to_time_major(x_sample)]
    hm = [_modulate(groups[w], xs[w], mod(0, w, 1), mod(0, w, 0)) for w in range(2)]
    outs = [dict(k=[], v=[], lf=[], cb=[], cc=[], h=[]) for _ in range(2)]

    for i in range(depth):
        kind, j = i % N_MIXERS, i // N_MIXERS
        last = i == depth - 1
        for w in range(2):
            grp = groups[w]
            o = outs[w]
            if kind == 0:
                (qb,) = _linear(grp, hm[w], fox_w_in, j, 0, d, [F32 if w else BF], scale=1.0 if w else Q_SCALE)
                k32, kb = _linear(grp, hm[w], fox_w_in, j, d, d, [F32, BF])
                v32, vb = _linear(grp, hm[w], fox_w_in, j, 2 * d, d, [F32, BF])
                lf = _forget(grp, hm[w], fox_w_in[j][:, 3 * d:], fox_b_f[j])
                if w == 0:
                    fcum = _cumsum_time(jnp.swapaxes(lf.reshape(b, t, nh), 1, 2))
                    a_mix = _attention_prompt(qb, kb, vb, fcum.reshape(b, nh // 2, 2, t), b, t)
                    o['k'].append(k32.reshape(b, t, nh, hd))
                    o['v'].append(v32.reshape(b, t, nh, hd))
                    o['lf'].append(lf.reshape(b, t, nh))
                else:
                    nk = 16
                    k_b = _from_time_major(k32, db)
                    v_b = _from_time_major(v32, db)
                    lf_b = _from_time_major(lf, db)
                    pad = ((0, 0), (0, nk - dt), (0, 0))
                    lf_new_t = jnp.pad(jnp.swapaxes(lf_b, 1, 2), ((0, 0), (0, 0), (0, LANE_V7X - dt)))
                    att = _attention_sample(j, page_table, _from_time_major(qb, db), jnp.pad(k_b, pad),
                                            jnp.pad(v_b, pad), lf_new_t, cache_k4, cache_v4, cache_lf_t)
                    a_mix = _to_time_major(att)
                    o['k'].append(k_b.reshape(db, dt, nh, hd))
                    o['v'].append(v_b.reshape(db, dt, nh, hd))
                    o['lf'].append(lf_b)
                w_out = fox_w_out
            elif kind == 1:
                width = sc_w_conv.shape[1]
                if w == 0:
                    init = jnp.zeros((1, SUBLANE_V7X, d), F32)
                else:
                    init = _to_time_major(state_conv_b[j])[None]
                a_mix, tail = _short_conv(grp, hm[w], sc_w_in, j, sc_w_conv[j], init)
                if w == 0:
                    o['cb'].append(tail[:, SUBLANE_V7X - (width - 1):, :])
                else:
                    o['cb'].append(_from_time_major(tail[0], db))
                w_out = sc_w_out
            else:
                width = lru_conv_w.shape[1]
                if w == 0:
                    init = jnp.zeros((1, SUBLANE_V7X, d), F32)
                    h0 = jnp.zeros((1, SUBLANE_V7X, d), F32)
                else:
                    init = _to_time_major(state_conv_c[j])[None]
                    h0 = state_h[j][None]
                a_mix, tail, hl = _rglru(grp, hm[w], lru_w_in, j, lru_conv_w[j], lru_conv_b[j], lru_w_a[j],
                                         lru_b_a[j], lru_w_x[j], lru_b_x[j], lru_lam[j], init, h0)
                if w == 0:
                    o['cc'].append(tail[:, SUBLANE_V7X - (width - 1):, :])
                    o['h'].append(hl[:, 0, :])
                else:
                    o['cc'].append(_from_time_major(tail[0], db))
                    o['h'].append(hl[0])
                w_out = lru_w_out

            is_moe = i % 2 == 1
            res = _out_ln(grp, a_mix, w_out, j, xs[w], mod(i, w, 2), mod(i, w, 4), mod(i, w, 3),
                          ln_g[i, 0], ln_b[i, 0], alpha, moe_w_router[i // 2] if is_moe else None)
            x1, hf = res[0], res[1]
            nxt = (mod(i + 1, w, 1), mod(i + 1, w, 0)) if not last else (mod(i, w, 1), mod(i, w, 0))
            if is_moe:
                xs[w], hm[w] = _ffn(moe_groups[w], hf, moe_w_gu, moe_w_down, i // 2, x1, mod(i, w, 5), nxt[0], nxt[1],
                                    ln_g[i, 1], ln_b[i, 1], alpha, gates=res[2], has_next=not last)
            else:
                xs[w], hm[w] = _ffn(grp, hf, ffn_w_gu, ffn_w_down, i // 2, x1, mod(i, w, 5), nxt[0], nxt[1],
                                    ln_g[i, 1], ln_b[i, 1], alpha, has_next=not last)

    y_prompt = xs[0].reshape(b, t, d)
    y_sample = _from_time_major(xs[1], db)
    op, os_ = outs
    return (y_prompt, y_sample,
            jnp.stack(op['k']), jnp.stack(op['v']), jnp.stack(op['lf']),
            jnp.stack(op['cb']), jnp.stack(op['cc']), jnp.stack(op['h']),
            jnp.stack(os_['k']), jnp.stack(os_['v']), jnp.stack(os_['lf']),
            jnp.stack(os_['cb']), jnp.stack(os_['cc']), jnp.stack(os_['h']))
```

```python
import functools

import jax
import jax.numpy as jnp
from jax import lax
from jax.experimental import pallas as pl
from jax.experimental.pallas import tpu as pltpu

BF = jnp.bfloat16
F32 = jnp.float32

LANE_V7X = 128
SUBLANE_V7X = 8
VMEM_LIMIT_V7X = 52 * 1024 * 1024

FOX_HEAD_DIM = 64
Q_SCALE = FOX_HEAD_DIM ** -0.5
LRU_C = 8.0
LN_EPS = 1e-5
N_MIXERS = 3
TOP_K = 2
PAGES_PER_STEP = 4


def _cparams(sem):
    return pltpu.CompilerParams(dimension_semantics=sem, vmem_limit_bytes=VMEM_LIMIT_V7X)


def _dot(a, b):
    return jnp.dot(a, b, preferred_element_type=F32)


def _dot_nt(a, b):
    return lax.dot_general(a, b, (((1,), (1,)), ((), ())), preferred_element_type=F32)


def _split(a):
    hi = a.astype(BF)
    lo = (a - hi.astype(F32)).astype(BF)
    return hi, lo


def _dot3(a, b):
    ah, al = _split(a)
    bh, bl = _split(b)
    return _dot(ah, bh) + (_dot(ah, bl) + _dot(al, bh))


def _pick(n, cands):
    for c in cands:
        if n % c == 0:
            return c
    return n


class _Group:
    def __init__(self, n_rows, tm, rows_per_seq, stride, n_seq):
        self.n = n_rows
        self.tm = tm
        self.nm = n_rows // tm
        self.rps = rows_per_seq
        self.g = 1 if stride == 1 else tm
        self.stride = stride
        self.tps = rows_per_seq // tm
        self.n_seq = n_seq

    def mod_spec(self, d, m_axis, n_axes):
        tm, rps = self.tm, self.rps

        def idx(*ids):
            return ((ids[m_axis] * tm) // rps, 0, 0)
        return pl.BlockSpec((None, self.g, d), idx)


def _ada_kernel(c_ref, w_ref, b_ref, o_ref):
    c = c_ref[...]
    a = c * jax.nn.sigmoid(c)
    o_ref[...] = _dot3(a, w_ref[...]) + b_ref[...]


def _ada(c_all, w_ada, b_ada):
    depth, d, d6 = w_ada.shape
    s = c_all.shape[0]
    tn = _pick(d6, (1024, 512, 256, 128))
    return pl.pallas_call(
        _ada_kernel,
        out_shape=jax.ShapeDtypeStruct((depth, s, d6), F32),
        grid=(depth, d6 // tn),
        in_specs=[pl.BlockSpec((s, d), lambda l, n: (0, 0)),
                  pl.BlockSpec((None, d, tn), lambda l, n: (l, 0, n)),
                  pl.BlockSpec((None, 1, tn), lambda l, n: (l, 0, n))],
        out_specs=pl.BlockSpec((None, s, tn), lambda l, n: (l, 0, n)),
        compiler_params=_cparams(("arbitrary", "arbitrary")),
        name="ada_mod",
    )(c_all, w_ada, b_ada.reshape(depth, 1, d6))


def _modulate_kernel(x_ref, sc_ref, sh_ref, o_ref):
    o_ref[...] = (x_ref[...] * (1.0 + sc_ref[...]) + sh_ref[...]).astype(BF)


def _modulate(grp, x, sc, sh):
    n, d = x.shape
    tm = grp.tm
    return pl.pallas_call(
        _modulate_kernel,
        out_shape=jax.ShapeDtypeStruct((n, d), BF),
        grid=(grp.nm,),
        in_specs=[pl.BlockSpec((tm, d), lambda m: (m, 0)),
                  grp.mod_spec(d, 0, 1), grp.mod_spec(d, 0, 1)],
        out_specs=pl.BlockSpec((tm, d), lambda m: (m, 0)),
        compiler_params=_cparams(("arbitrary",)),
        name="modulate",
    )(x, sc, sh)


def _linear_kernel(x_ref, w_ref, *refs, scale, n_out):
    outs, wb = refs[:n_out], refs[n_out]

    @pl.when(pl.program_id(1) == 0)
    def _():
        wb[...] = w_ref[...].astype(BF)

    y = _dot(x_ref[...], wb[...])
    if scale != 1.0:
        y = y * scale
    for o in outs:
        o[...] = y.astype(o.dtype)


def _linear(grp, x, w3, layer, col0, n_cols, out_dtypes, scale=1.0):
    n, k = x.shape
    tm = grp.tm
    tn = _pick(n_cols, (512, 256, 128))
    assert col0 % tn == 0
    cb0 = col0 // tn
    outs = [jax.ShapeDtypeStruct((n, n_cols), dt) for dt in out_dtypes]
    return pl.pallas_call(
        functools.partial(_linear_kernel, scale=scale, n_out=len(outs)),
        out_shape=outs,
        grid=(n_cols // tn, grp.nm),
        in_specs=[pl.BlockSpec((tm, k), lambda c, m: (m, 0)),
                  pl.BlockSpec((None, k, tn), lambda c, m: (layer, 0, cb0 + c))],
        out_specs=[pl.BlockSpec((tm, tn), lambda c, m: (m, c)) for _ in outs],
        scratch_shapes=[pltpu.VMEM((k, tn), BF)],
        compiler_params=_cparams(("arbitrary", "arbitrary")),
        name="linear",
    )(x, w3)


def _log_sigmoid(z):
    return jnp.minimum(z, 0.0) - jnp.log1p(jnp.exp(-jnp.abs(z)))


def _forget_kernel(x_ref, w_ref, b_ref, o_ref):
    z = _dot(x_ref[...], w_ref[...].astype(BF)) + b_ref[...]
    o_ref[...] = _log_sigmoid(z)


def _forget(grp, x, w_f, b_f):
    n, k = x.shape
    h = w_f.shape[1]
    tm = grp.tm
    return pl.pallas_call(
        _forget_kernel,
        out_shape=jax.ShapeDtypeStruct((n, h), F32),
        grid=(grp.nm,),
        in_specs=[pl.BlockSpec((tm, k), lambda m: (m, 0)),
                  pl.BlockSpec((k, h), lambda m: (0, 0)),
                  pl.BlockSpec((1, h), lambda m: (0, 0))],
        out_specs=pl.BlockSpec((tm, h), lambda m: (m, 0)),
        compiler_params=_cparams(("arbitrary",)),
        name="forget_gate",
    )(x, w_f, b_f.reshape(1, h))


def _prefix_lanes(x):
    n = x.shape[-1]
    lane = lax.broadcasted_iota(jnp.int32, x.shape, x.ndim - 1)
    d = 1
    while d < n:
        x = x + jnp.where(lane >= d, pltpu.roll(x, d, x.ndim - 1), 0.0)
        d *= 2
    return x


def _cumsum_kernel(x_ref, o_ref):
    o_ref[...] = _prefix_lanes(x_ref[...])


def _cumsum_time(lf_t):
    b, h, t = lf_t.shape
    return pl.pallas_call(
        _cumsum_kernel,
        out_shape=jax.ShapeDtypeStruct((b, h, t), F32),
        grid=(b,),
        in_specs=[pl.BlockSpec((None, h, t), lambda i: (i, 0, 0))],
        out_specs=pl.BlockSpec((None, h, t), lambda i: (i, 0, 0)),
        compiler_params=_cparams(("arbitrary",)),
        name="forget_cumsum",
    )(lf_t)


def _attn_kernel(q_ref, k_ref, v_ref, f_ref, o_ref, m_sc, l_sc, acc_sc, *, tq):
    qi = pl.program_id(2)
    half = LANE_V7X // 2
    lane = lax.broadcasted_iota(jnp.int32, (tq, LANE_V7X), 1)
    lo = lane < half
    q = q_ref[...]
    zero = jnp.zeros_like(q)
    qs = (jnp.where(lo, q, zero), jnp.where(lo, zero, q))
    for h in range(2):
        m_sc[h] = jnp.full((tq, LANE_V7X), -jnp.inf, F32)
        l_sc[h] = jnp.zeros((tq, LANE_V7X), F32)
        acc_sc[h] = jnp.zeros((tq, LANE_V7X), F32)

    def block(k0, masked):
        kb = k_ref[pl.ds(k0, tq), :]
        vb = v_ref[pl.ds(k0, tq), :]
        for h in range(2):
            s = _dot_nt(qs[h], kb) - f_ref[h:h + 1, pl.ds(k0, tq)]
            if masked:
                row = lax.broadcasted_iota(jnp.int32, (tq, tq), 0)
                col = lax.broadcasted_iota(jnp.int32, (tq, tq), 1)
                s = jnp.where(col <= row, s, -jnp.inf)
            m_prev = m_sc[h]
            m_new = jnp.maximum(m_prev, jnp.max(s, axis=1, keepdims=True))
            alpha = jnp.exp(m_prev - m_new)
            p = jnp.exp(s - m_new[:, 0:1])
            l_sc[h] = alpha * l_sc[h] + jnp.sum(p, axis=1, keepdims=True)
            acc_sc[h] = alpha * acc_sc[h] + _dot(p.astype(BF), vb)
            m_sc[h] = m_new

    def body(i, carry):
        block(pl.multiple_of(i * tq, tq), False)
        return carry

    lax.fori_loop(0, qi, body, 0)
    block(pl.multiple_of(qi * tq, tq), True)
    o = jnp.where(lo, acc_sc[0] / l_sc[0], acc_sc[1] / l_sc[1])
    o_ref[...] = o.astype(o_ref.dtype)


def _attention_prompt(q, k, v, fcum, b, t):
    n, d = q.shape
    hp = d // LANE_V7X
    tq = _pick(t, (512, 256, 128))
    nq = t // tq
    return pl.pallas_call(
        functools.partial(_attn_kernel, tq=tq),
        out_shape=jax.ShapeDtypeStruct((n, d), BF),
        grid=(b, hp, nq),
        in_specs=[pl.BlockSpec((tq, LANE_V7X), lambda i, j, qi: (i * nq + qi, j)),
                  pl.BlockSpec((t, LANE_V7X), lambda i, j, qi: (i, j)),
                  pl.BlockSpec((t, LANE_V7X), lambda i, j, qi: (i, j)),
                  pl.BlockSpec((None, None, 2, t), lambda i, j, qi: (i, j, 0, 0))],
        out_specs=pl.BlockSpec((tq, LANE_V7X), lambda i, j, qi: (i * nq + qi, j)),
        scratch_shapes=[pltpu.VMEM((2, tq, LANE_V7X), F32)] * 3,
        compiler_params=_cparams(("arbitrary", "arbitrary", "arbitrary")),
        name="fox_attention_prompt",
    )(q, k, v, fcum)


def _split3(x):
    hi = x.astype(BF)
    r1 = x - hi.astype(F32)
    mid = r1.astype(BF)
    lo = (r1 - mid.astype(F32)).astype(BF)
    return hi, mid, lo


def _decode_kernel(pt_ref, q_ref, kn_ref, vn_ref, lfn_ref, *refs, pp, nh, page, nk):
    k_refs = refs[:pp]
    v_refs = refs[pp:2 * pp]
    lf_refs = refs[2 * pp:3 * pp]
    o_ref, m_sc, l_sc, acc_sc, c_sc = refs[3 * pp:]
    g = pl.program_id(1)
    qr = q_ref.shape[1]
    hd = q_ref.shape[2]

    @pl.when(g == 0)
    def _():
        m_sc[...] = jnp.full(m_sc.shape, -jnp.inf, F32)
        l_sc[...] = jnp.zeros(l_sc.shape, F32)
        acc_sc[...] = jnp.zeros(acc_sc.shape, F32)
        c_sc[...] = jnp.zeros(c_sc.shape, F32)

    def head_rows(ref, h, n_keys):
        return ref[pl.ds(h, n_keys, stride=nh), :].astype(BF)

    def update(s, v_blocks, width):
        m_prev = m_sc[...]
        m_new = jnp.maximum(m_prev, jnp.max(s, axis=1, keepdims=True))
        alpha = jnp.exp(m_prev - m_new)
        p = jnp.exp(s - m_new[:, 0:1])
        l_sc[...] = alpha * l_sc[...] + jnp.sum(p, axis=1, keepdims=True)
        rows = []
        for h in range(nh):
            ph = p[h * qr:(h + 1) * qr, :].astype(BF)
            pv = None
            for i, vref in enumerate(v_blocks):
                term = _dot(ph[:, i * width:(i + 1) * width], head_rows(vref, h, width))
                pv = term if pv is None else pv + term
            rows.append(pv)
        acc_sc[...] = alpha[:, 0:hd] * acc_sc[...] + jnp.concatenate(rows, axis=0)
        m_sc[...] = m_new

    def scores(k_blocks, width, cum):
        rows = []
        for h in range(nh):
            qh = (q_ref[h] * Q_SCALE).astype(BF)
            sh = jnp.concatenate([_dot_nt(qh, head_rows(kref, h, width)) for kref in k_blocks], axis=1)
            rows.append(sh - cum[h:h + 1, :])
        return jnp.concatenate(rows, axis=0)

    eye = jnp.where(lax.broadcasted_iota(jnp.int32, (nh, nh), 0) == lax.broadcasted_iota(jnp.int32, (nh, nh), 1),
                    1.0, 0.0).astype(BF)

    def heads_major(lf):
        p1, p2, p3 = _split3(lf)
        return _dot_nt(eye, p1) + _dot_nt(eye, p2) + _dot_nt(eye, p3)

    lf_t = jnp.concatenate([heads_major(lf_refs[i][...]) for i in range(pp)], axis=1)
    cum = _prefix_lanes(lf_t) + c_sc[:, 0:1]
    c_sc[...] = c_sc[...] + jnp.sum(lf_t, axis=1, keepdims=True)
    update(scores(k_refs, page, cum), v_refs, page)

    @pl.when(g == pl.num_programs(1) - 1)
    def _():
        cum_n = (_prefix_lanes(lfn_ref[...]) + c_sc[:, 0:1])[:, 0:nk]
        sn = scores([kn_ref], nk, cum_n)
        trow = lax.rem(lax.broadcasted_iota(jnp.int32, sn.shape, 0), qr)
        col = lax.broadcasted_iota(jnp.int32, sn.shape, 1)
        update(jnp.where(col <= trow, sn, -jnp.inf), [vn_ref], nk)
        o_ref[...] = acc_sc[...] / l_sc[:, 0:hd]


def _attention_sample(layer, page_table, q_hm, k_new, v_new, lf_new_t, cache_k, cache_v, cache_lf):
    db, nh, qr, hd = q_hm.shape
    nk = k_new.shape[1] // nh
    page = cache_lf.shape[2]
    n_pages = page_table.shape[1]
    pp = _pick(n_pages, (PAGES_PER_STEP, 2, 1))
    r = nh * qr

    def page_spec(shape, i):
        return pl.BlockSpec((None, None) + shape, lambda b, g, pt: (layer, pt[b, g * pp + i], 0, 0))

    in_specs = [pl.BlockSpec((None, nh, qr, hd), lambda b, g, pt: (b, 0, 0, 0)),
                pl.BlockSpec((None, nk * nh, hd), lambda b, g, pt: (b, 0, 0)),
                pl.BlockSpec((None, nk * nh, hd), lambda b, g, pt: (b, 0, 0)),
                pl.BlockSpec((None, nh, LANE_V7X), lambda b, g, pt: (b, 0, 0))]
    in_specs += [page_spec((page * nh, hd), i) for i in range(pp)]
    in_specs += [page_spec((page * nh, hd), i) for i in range(pp)]
    in_specs += [page_spec((page, nh), i) for i in range(pp)]
    grid_spec = pltpu.PrefetchScalarGridSpec(
        num_scalar_prefetch=1,
        grid=(db, n_pages // pp),
        in_specs=in_specs,
        out_specs=pl.BlockSpec((None, r, hd), lambda b, g, pt: (b, 0, 0)),
        scratch_shapes=[pltpu.VMEM((r, LANE_V7X), F32),
                        pltpu.VMEM((r, LANE_V7X), F32),
                        pltpu.VMEM((r, hd), F32),
                        pltpu.VMEM((nh, LANE_V7X), F32)])
    return pl.pallas_call(
        functools.partial(_decode_kernel, pp=pp, nh=nh, page=page, nk=nk),
        out_shape=jax.ShapeDtypeStruct((db, r, hd), F32),
        grid_spec=grid_spec,
        compiler_params=_cparams(("arbitrary", "arbitrary")),
        name="fox_attention_sample",
    )(page_table, q_hm, k_new, v_new, lf_new_t,
      *([cache_k] * pp), *([cache_v] * pp), *([cache_lf] * pp))


def _conv_tile(e_ref, carry_ref, init_ref, tail_ref, u, w_ref, first, *, hb, stride, width):
    tm = u.shape[0]

    @pl.when(first)
    def _():
        e_ref[0:hb, :] = init_ref[...]

    @pl.when(jnp.logical_not(first))
    def _():
        e_ref[0:hb, :] = carry_ref[...]

    e_ref[hb:hb + tm, :] = u
    y = None
    for j in range(width):
        off = hb - (width - 1 - j) * stride
        term = w_ref[j:j + 1, :] * e_ref[off:off + tm, :]
        y = term if y is None else y + term
    tail = e_ref[tm:tm + hb, :]
    carry_ref[...] = tail
    tail_ref[...] = tail
    return y


def _sc_kernel(x_ref, wb_ref, wc_ref, wv_ref, wconv_ref, init_ref, mix_ref, tail_ref,
               wb, e_sc, carry, *, tps, hb, stride, width):
    m = pl.program_id(1)

    @pl.when(m == 0)
    def _():
        wb[0] = wb_ref[...].astype(BF)
        wb[1] = wc_ref[...].astype(BF)
        wb[2] = wv_ref[...].astype(BF)

    x = x_ref[...]
    g_b = _dot(x, wb[0])
    u = _dot(x, wb[1]) * _dot(x, wb[2])
    y = _conv_tile(e_sc, carry, init_ref, tail_ref, u, wconv_ref, (m % tps) == 0,
                   hb=hb, stride=stride, width=width)
    mix_ref[...] = (g_b * y).astype(mix_ref.dtype)


def _short_conv(grp, x, w_in, layer, w_conv, init):
    n, d = x.shape
    tm = grp.tm
    width = w_conv.shape[0]
    hb = init.shape[1]
    tn = _pick(d, (256, 128))
    nb = d // tn
    tps = grp.tps
    n_init = init.shape[0]

    def wspec(part):
        return pl.BlockSpec((None, d, tn), lambda c, m: (layer, 0, part * nb + c))

    def seq_of(m):
        return m // tps

    return pl.pallas_call(
        functools.partial(_sc_kernel, tps=tps, hb=hb, stride=grp.stride, width=width),
        out_shape=[jax.ShapeDtypeStruct((n, d), BF),
                   jax.ShapeDtypeStruct((grp.n_seq, hb, d), F32)],
        grid=(nb, grp.nm),
        in_specs=[pl.BlockSpec((tm, d), lambda c, m: (m, 0)),
                  wspec(0), wspec(1), wspec(2),
                  pl.BlockSpec((width, tn), lambda c, m: (0, c)),
                  pl.BlockSpec((None, hb, tn), lambda c, m: (seq_of(m) % n_init, 0, c))],
        out_specs=[pl.BlockSpec((tm, tn), lambda c, m: (m, c)),
                   pl.BlockSpec((None, hb, tn), lambda c, m: (seq_of(m), 0, c))],
        scratch_shapes=[pltpu.VMEM((3, d, tn), BF),
                        pltpu.VMEM((hb + tm, tn), F32),
                        pltpu.VMEM((hb, tn), F32)],
        compiler_params=_cparams(("arbitrary", "arbitrary")),
        name="short_conv_mixer",
    )(x, w_in, w_in, w_in, w_conv, init)


def _softplus(x):
    return jnp.maximum(x, 0.0) + jnp.log1p(jnp.exp(-jnp.abs(x)))


def _neg_expm1(x):
    th = jnp.tanh(0.5 * x)
    return -2.0 * th / (1.0 - th)


def _scan_rows(a, u):
    tm = a.shape[0]
    row = lax.broadcasted_iota(jnp.int32, a.shape, 0)
    d = 1
    while d < tm:
        keep = row >= d
        a_s = jnp.where(keep, pltpu.roll(a, d, 0), 1.0)
        u_s = jnp.where(keep, pltpu.roll(u, d, 0), 0.0)
        u = u + a * u_s
        a = a * a_s
        d *= 2
    return a, u


def _lru_kernel(x_ref, wg_ref, wr_ref, cw_ref, cb_ref, bda_ref, bdx_ref, ba_ref, bx_ref, lam_ref,
                init_ref, h0_ref, mix_ref, tail_ref, hlast_ref,
                wb, bdb, e_sc, carry, hcarry, *, tps, hb, stride, width):
    m = pl.program_id(1)
    tm = x_ref.shape[0]

    @pl.when(m == 0)
    def _():
        wb[0] = wg_ref[...].astype(BF)
        wb[1] = wr_ref[...].astype(BF)
        bdb[0] = bda_ref[...].astype(BF)
        bdb[1] = bdx_ref[...].astype(BF)

    first = (m % tps) == 0
    x = x_ref[...]
    gate = jax.nn.gelu(_dot(x, wb[0]), approximate=True)
    xr = _dot(x, wb[1])
    xc = _conv_tile(e_sc, carry, init_ref, tail_ref, xr, cw_ref, first,
                    hb=hb, stride=stride, width=width) + cb_ref[...]
    xcb = xc.astype(BF)
    r = jax.nn.sigmoid(_dot(xcb, bdb[0]) + ba_ref[...])
    i = jax.nn.sigmoid(_dot(xcb, bdb[1]) + bx_ref[...])
    log_a = (-LRU_C) * r * _softplus(-lam_ref[...])
    a = jnp.exp(log_a)
    u = jnp.sqrt(_neg_expm1(2.0 * log_a)) * i * xc

    @pl.when(first)
    def _():
        hcarry[...] = h0_ref[...]

    if stride == 1:
        a_cum, h_loc = _scan_rows(a, u)
        hs = h_loc + a_cum * hcarry[0:1, :]
        mix_ref[...] = (gate * hs).astype(mix_ref.dtype)
        h_end = jnp.broadcast_to(hs[tm - 1:tm, :], hcarry.shape)
    else:
        h = hcarry[...]
        for t in range(tm // stride):
            sl = slice(t * stride, (t + 1) * stride)
            h = a[sl] * h + u[sl]
            mix_ref[sl, :] = (gate[sl] * h).astype(mix_ref.dtype)
        h_end = h
    hcarry[...] = h_end
    hlast_ref[...] = h_end


def _block_diag(w, per):
    nblk, bd, _ = w.shape
    eye = jnp.eye(per, dtype=w.dtype)
    return jnp.einsum('cide,ik->cidke', w.reshape(nblk // per, per, bd, bd), eye).reshape(
        nblk // per, per * bd, per * bd)


def _rglru(grp, x, w_in, layer, conv_w, conv_b, w_a, b_a, w_x, b_x, lam, init, h0):
    n, d = x.shape
    rw = conv_w.shape[1]
    tm = grp.tm
    width = conv_w.shape[0]
    hb = init.shape[1]
    hr = h0.shape[1]
    bd = w_a.shape[1]
    tn = _pick(rw, (256, 128))
    nb = rw // tn
    tps = grp.tps
    n_init = init.shape[0]
    bda = _block_diag(w_a, tn // bd)
    bdx = _block_diag(w_x, tn // bd)

    def row_spec():
        return pl.BlockSpec((1, tn), lambda c, m: (0, c))

    def seq_of(m):
        return m // tps

    return pl.pallas_call(
        functools.partial(_lru_kernel, tps=tps, hb=hb, stride=grp.stride, width=width),
        out_shape=[jax.ShapeDtypeStruct((n, rw), BF),
                   jax.ShapeDtypeStruct((grp.n_seq, hb, rw), F32),
                   jax.ShapeDtypeStruct((grp.n_seq, hr, rw), F32)],
        grid=(nb, grp.nm),
        in_specs=[pl.BlockSpec((tm, d), lambda c, m: (m, 0)),
                  pl.BlockSpec((None, d, tn), lambda c, m: (layer, 0, c)),
                  pl.BlockSpec((None, d, tn), lambda c, m: (layer, 0, nb + c)),
                  pl.BlockSpec((width, tn), lambda c, m: (0, c)),
                  row_spec(),
                  pl.BlockSpec((None, tn, tn), lambda c, m: (c, 0, 0)),
                  pl.BlockSpec((None, tn, tn), lambda c, m: (c, 0, 0)),
                  row_spec(), row_spec(), row_spec(),
                  pl.BlockSpec((None, hb, tn), lambda c, m: (seq_of(m) % n_init, 0, c)),
                  pl.BlockSpec((None, hr, tn), lambda c, m: (seq_of(m) % n_init, 0, c))],
        out_specs=[pl.BlockSpec((tm, tn), lambda c, m: (m, c)),
                   pl.BlockSpec((None, hb, tn), lambda c, m: (seq_of(m), 0, c)),
                   pl.BlockSpec((None, hr, tn), lambda c, m: (seq_of(m), 0, c))],
        scratch_shapes=[pltpu.VMEM((2, d, tn), BF),
                        pltpu.VMEM((2, tn, tn), BF),
                        pltpu.VMEM((hb + tm, tn), F32),
                        pltpu.VMEM((hb, tn), F32),
                        pltpu.VMEM((hr, tn), F32)],
        compiler_params=_cparams(("arbitrary", "arbitrary")),
        name="rglru_mixer",
    )(x, w_in, w_in, conv_w, conv_b.reshape(1, rw), bda, bdx, b_a.reshape(1, rw),
      b_x.reshape(1, rw), lam.reshape(1, rw), init, h0)


def _post_ln(x, f, gate, ln_g, ln_b, alpha):
    z = alpha * x + (1.0 + gate) * f
    mu = jnp.mean(z, axis=-1, keepdims=True)
    zc = z - mu
    var = jnp.mean(zc * zc, axis=-1, keepdims=True)
    return zc * lax.rsqrt(var + LN_EPS) * ln_g + ln_b


def _top2_gates_t(logits_t):
    ne = float(logits_t.shape[0])
    idx = lax.broadcasted_iota(jnp.int32, logits_t.shape, 0).astype(F32)
    m1 = jnp.max(logits_t, axis=0, keepdims=True)
    i1 = jnp.min(jnp.where(logits_t == m1, idx, ne), axis=0, keepdims=True)
    sel1 = idx == i1
    rest = jnp.where(sel1, -jnp.inf, logits_t)
    m2 = jnp.max(rest, axis=0, keepdims=True)
    i2 = jnp.min(jnp.where(rest == m2, idx, ne), axis=0, keepdims=True)
    sel2 = idx == i2
    e2 = jnp.exp(m2 - m1)
    w1 = 1.0 / (1.0 + e2)
    w2 = e2 / (1.0 + e2)
    return jnp.where(sel1, w1, 0.0) + jnp.where(sel2, w2, 0.0)


def _out_ln_kernel(a_ref, w_ref, x_ref, g_ref, sc_ref, sh_ref, lng_ref, lnb_ref, *refs,
                   alpha, has_router):
    if has_router:
        wr_ref, x1_ref, hf_ref, gates_ref, wb = refs
    else:
        x1_ref, hf_ref, wb = refs

    @pl.when(pl.program_id(0) == 0)
    def _():
        wb[...] = w_ref[...].astype(BF)

    y = _dot(a_ref[...], wb[...])
    x1 = _post_ln(x_ref[...], y, g_ref[...], lng_ref[...], lnb_ref[...], alpha)
    x1_ref[...] = x1
    hf = x1 * (1.0 + sc_ref[...]) + sh_ref[...]
    hf_ref[...] = hf.astype(hf_ref.dtype)
    if has_router:
        wh, wl = _split(wr_ref[...])
        hh, hl = _split(hf)
        logits_t = _dot_nt(wh, hh) + (_dot_nt(wh, hl) + _dot_nt(wl, hh))
        gates_ref[...] = _top2_gates_t(logits_t)


def _out_ln(grp, a, w3, layer, x, gate, sc, sh, ln_g, ln_b, alpha, w_router=None):
    n, d = x.shape
    k = a.shape[1]
    tm = grp.tm
    has_router = w_router is not None
    in_specs = [pl.BlockSpec((tm, k), lambda m: (m, 0)),
                pl.BlockSpec((None, k, d), lambda m: (layer, 0, 0)),
                pl.BlockSpec((tm, d), lambda m: (m, 0)),
                grp.mod_spec(d, 0, 1), grp.mod_spec(d, 0, 1), grp.mod_spec(d, 0, 1),
                pl.BlockSpec((1, d), lambda m: (0, 0)),
                pl.BlockSpec((1, d), lambda m: (0, 0))]
    args = [a, w3, x, gate, sc, sh, ln_g.reshape(1, d), ln_b.reshape(1, d)]
    out_shape = [jax.ShapeDtypeStruct((n, d), F32), jax.ShapeDtypeStruct((n, d), BF)]
    out_specs = [pl.BlockSpec((tm, d), lambda m: (m, 0)), pl.BlockSpec((tm, d), lambda m: (m, 0))]
    if has_router:
        ne = w_router.shape[1]
        in_specs.append(pl.BlockSpec((ne, d), lambda m: (0, 0)))
        args.append(w_router.T)
        out_shape.append(jax.ShapeDtypeStruct((ne, n), F32))
        out_specs.append(pl.BlockSpec((ne, tm), lambda m: (0, m)))
    return pl.pallas_call(
        functools.partial(_out_ln_kernel, alpha=alpha, has_router=has_router),
        out_shape=out_shape,
        grid=(grp.nm,),
        in_specs=in_specs,
        out_specs=out_specs,
        scratch_shapes=[pltpu.VMEM((k, d), BF)],
        compiler_params=_cparams(("arbitrary",)),
        name="out_proj_ln",
    )(*args)


def _swiglu_part(h, wg, wu, wd):
    gu = _dot(h, wg.astype(BF))
    up = _dot(h, wu.astype(BF))
    act = (gu * jax.nn.sigmoid(gu) * up).astype(BF)
    return _dot(act, wd.astype(BF))


def _ffn_kernel(h_ref, wg_ref, wu_ref, wd_ref, x_ref, g_ref, sc_ref, sh_ref, lng_ref, lnb_ref, *refs,
                alpha, has_next):
    if has_next:
        x2_ref, hn_ref, acc = refs
    else:
        x2_ref, acc = refs
    ch = pl.program_id(1)
    part = _swiglu_part(h_ref[...], wg_ref[...], wu_ref[...], wd_ref[...])

    @pl.when(ch == 0)
    def _():
        acc[...] = part

    @pl.when(ch > 0)
    def _():
        acc[...] += part

    @pl.when(ch == pl.num_programs(1) - 1)
    def _():
        x2 = _post_ln(x_ref[...], acc[...], g_ref[...], lng_ref[...], lnb_ref[...], alpha)
        x2_ref[...] = x2
        if has_next:
            hn_ref[...] = (x2 * (1.0 + sc_ref[...]) + sh_ref[...]).astype(hn_ref.dtype)


def _ffn(grp, h, w_gu, w_down, layer, x, gate, sc, sh, ln_g, ln_b, alpha, has_next=True):
    n, d = x.shape
    tm = grp.tm
    ff = w_down.shape[1]
    tf = _pick(ff, (256, 128))
    nch = ff // tf
    row = pl.BlockSpec((tm, d), lambda m, c: (m, 0))
    vec = pl.BlockSpec((1, d), lambda m, c: (0, 0))
    in_specs = [row,
                pl.BlockSpec((None, d, tf), lambda m, c: (layer, 0, c)),
                pl.BlockSpec((None, d, tf), lambda m, c: (layer, 0, nch + c)),
                pl.BlockSpec((None, tf, d), lambda m, c: (layer, c, 0)),
                row, grp.mod_spec(d, 0, 2), grp.mod_spec(d, 0, 2), grp.mod_spec(d, 0, 2), vec, vec]
    out_shape = [jax.ShapeDtypeStruct((n, d), F32)]
    out_specs = [row]
    if has_next:
        out_shape.append(jax.ShapeDtypeStruct((n, d), BF))
        out_specs.append(row)
    res = pl.pallas_call(
        functools.partial(_ffn_kernel, alpha=alpha, has_next=has_next),
        out_shape=out_shape,
        grid=(grp.nm, nch),
        in_specs=in_specs,
        out_specs=out_specs,
        scratch_shapes=[pltpu.VMEM((tm, d), F32)],
        compiler_params=_cparams(("arbitrary", "arbitrary")),
        name="ffn_dense",
    )(h, w_gu, w_gu, w_down, x, gate, sc, sh, ln_g.reshape(1, d), ln_b.reshape(1, d))
    return (res[0], res[1]) if has_next else (res[0], None)


MOE_TT = 512
MOE_RC = 128
MOE_TMG = 512
MOE_ALIGN = 16


def _seg_sizes(tt):
    sizes = []
    sz = tt
    while sz >= MOE_ALIGN:
        sizes.append(sz)
        sz //= 2
    return tuple(sizes)


def _segments(c16, sizes):
    off = jnp.int32(0)
    for j, sz in enumerate(sizes):
        cond = (c16 & sz) != 0
        yield cond, off, sz, j
        off = off + jnp.where(cond, sz, 0)


def _route_count_kernel(g_ref, o_ref):
    routed = jnp.where(g_ref[...] > 0.0, 1.0, 0.0)
    o_ref[...] = jnp.broadcast_to(jnp.sum(routed, axis=1, keepdims=True), o_ref.shape)


def _route_counts(gates_t, tt):
    ne, n = gates_t.shape
    nt = n // tt
    return pl.pallas_call(
        _route_count_kernel,
        out_shape=jax.ShapeDtypeStruct((nt, ne, LANE_V7X), F32),
        grid=(nt,),
        in_specs=[pl.BlockSpec((ne, tt), lambda i: (0, i))],
        out_specs=pl.BlockSpec((None, ne, LANE_V7X), lambda i: (i, 0, 0)),
        compiler_params=_cparams(("arbitrary",)),
        name="moe_route_counts",
    )(gates_t)


def _dispatch_kernel(base_ref, c16_ref, g_ref, h_ref, sorted_in_ref, sorted_ref, stage, sems,
                     *, ne, tt, rc, sizes):
    del sorted_in_ref
    i = pl.program_id(0)
    gt = g_ref[...]
    routed = gt > 0.0
    ri = lax.broadcasted_iota(jnp.int32, (tt, tt), 0)
    ci = lax.broadcasted_iota(jnp.int32, (tt, tt), 1)
    before = jnp.where(ri < ci, 1.0, 0.0).astype(BF)
    rank = _dot(jnp.where(routed, 1.0, 0.0).astype(BF), before)
    rank = jnp.where(routed, rank, -1.0)
    h = h_ref[...]

    def seg_copy(e, b0, off, sz, j):
        return pltpu.make_async_copy(
            stage.at[e, pl.ds(pl.multiple_of(off, MOE_ALIGN), sz)],
            sorted_ref.at[pl.ds(pl.multiple_of(b0 + off, MOE_ALIGN), sz)],
            sems.at[e, j])

    for e in range(ne):
        c16 = c16_ref[i * ne + e]
        b0 = base_ref[i * ne + e]
        for c in range(tt // rc):
            @pl.when(c * rc < c16)
            def _():
                slot = (lax.broadcasted_iota(jnp.int32, (rc, tt), 0) + c * rc).astype(F32)
                sel = slot == rank[e:e + 1, :]
                stage[e, c * rc:(c + 1) * rc, :] = _dot(jnp.where(sel, 1.0, 0.0).astype(BF), h).astype(BF)
        for cond, off, sz, j in _segments(c16, sizes):
            @pl.when(cond)
            def _():
                seg_copy(e, b0, off, sz, j).start()

    for e in range(ne):
        c16 = c16_ref[i * ne + e]
        b0 = base_ref[i * ne + e]
        for cond, off, sz, j in _segments(c16, sizes):
            @pl.when(cond)
            def _():
                seg_copy(e, b0, off, sz, j).wait()


def _dispatch(base, c16, gates_t, h_all, s_max, tt):
    ne, n = gates_t.shape
    d = h_all.shape[1]
    sizes = _seg_sizes(tt)
    grid_spec = pltpu.PrefetchScalarGridSpec(
        num_scalar_prefetch=2,
        grid=(n // tt,),
        in_specs=[pl.BlockSpec((ne, tt), lambda i, b, c: (0, i)),
                  pl.BlockSpec((tt, d), lambda i, b, c: (i, 0)),
                  pl.BlockSpec(memory_space=pl.ANY)],
        out_specs=pl.BlockSpec(memory_space=pl.ANY),
        scratch_shapes=[pltpu.VMEM((ne, tt, d), BF),
                        pltpu.SemaphoreType.DMA((ne, len(sizes)))])
    return pl.pallas_call(
        functools.partial(_dispatch_kernel, ne=ne, tt=tt, rc=min(MOE_RC, tt), sizes=sizes),
        out_shape=jax.ShapeDtypeStruct((s_max, d), BF),
        grid_spec=grid_spec,
        input_output_aliases={4: 0},
        compiler_params=_cparams(("arbitrary",)),
        name="moe_dispatch",
    )(base, c16, gates_t, h_all, jnp.zeros((s_max, d), BF))


def _grouped_kernel(te_ref, nu_ref, x_ref, wg_ref, wu_ref, wd_ref, y_ref, acc):
    del te_ref
    g = pl.program_id(0)
    ch = pl.program_id(1)
    last = pl.num_programs(1) - 1

    @pl.when(g < nu_ref[0])
    def _():
        part = _swiglu_part(x_ref[...], wg_ref[...], wu_ref[...], wd_ref[...])

        @pl.when(ch == 0)
        def _():
            acc[...] = part

        @pl.when(ch > 0)
        def _():
            acc[...] += part

        @pl.when(ch == last)
        def _():
            y_ref[...] = acc[...].astype(y_ref.dtype)

    @pl.when(jnp.logical_and(g >= nu_ref[0], ch == last))
    def _():
        y_ref[...] = jnp.zeros(y_ref.shape, y_ref.dtype)


def _grouped_ffn(tile_expert, n_used, xs, w_gu, w_down, layer):
    s_max, d = xs.shape
    ff = w_down.shape[2]
    tmg = MOE_TMG
    tf = _pick(ff, (512, 256, 128))
    nch = ff // tf

    def row(g, c, te, nu):
        return (jnp.minimum(g, nu[0] - 1), 0)

    def chunk(g, c, nu):
        return jnp.where(g < nu[0], c, nch - 1)

    grid_spec = pltpu.PrefetchScalarGridSpec(
        num_scalar_prefetch=2,
        grid=(s_max // tmg, nch),
        in_specs=[pl.BlockSpec((tmg, d), row),
                  pl.BlockSpec((None, None, d, tf), lambda g, c, te, nu: (layer, te[g], 0, chunk(g, c, nu))),
                  pl.BlockSpec((None, None, d, tf), lambda g, c, te, nu: (layer, te[g], 0, nch + chunk(g, c, nu))),
                  pl.BlockSpec((None, None, tf, d), lambda g, c, te, nu: (layer, te[g], chunk(g, c, nu), 0))],
        out_specs=pl.BlockSpec((tmg, d), lambda g, c, te, nu: (g, 0)),
        scratch_shapes=[pltpu.VMEM((tmg, d), F32)])
    return pl.pallas_call(
        _grouped_kernel,
        out_shape=jax.ShapeDtypeStruct((s_max, d), BF),
        grid_spec=grid_spec,
        compiler_params=_cparams(("arbitrary", "arbitrary")),
        name="moe_grouped_ffn",
    )(tile_expert, n_used, xs, w_gu, w_gu, w_down)


def _combine_kernel(base_ref, c16_ref, g_ref, y_ref, f_ref, ystage, acc, sems, *, ne, tt, rc, sizes):
    i = pl.program_id(0)

    @pl.when(i == 0)
    def _():
        ystage[...] = jnp.zeros(ystage.shape, ystage.dtype)

    def seg_copy(e, b0, off, sz, j):
        return pltpu.make_async_copy(
            y_ref.at[pl.ds(pl.multiple_of(b0 + off, MOE_ALIGN), sz)],
            ystage.at[e, pl.ds(pl.multiple_of(off, MOE_ALIGN), sz)],
            sems.at[e, j])

    for e in range(ne):
        c16 = c16_ref[i * ne + e]
        b0 = base_ref[i * ne + e]
        for cond, off, sz, j in _segments(c16, sizes):
            @pl.when(cond)
            def _():
                seg_copy(e, b0, off, sz, j).start()

    ri = lax.broadcasted_iota(jnp.int32, (tt, tt), 0)
    ci = lax.broadcasted_iota(jnp.int32, (tt, tt), 1)
    eye = jnp.where(ri == ci, 1.0, 0.0).astype(BF)
    p1, p2, p3 = _split3(g_ref[...])
    gtok = _dot_nt(eye, p1) + _dot_nt(eye, p2) + _dot_nt(eye, p3)
    routed = gtok > 0.0
    before = jnp.where(ci < ri, 1.0, 0.0).astype(BF)
    rank = _dot(before, jnp.where(routed, 1.0, 0.0).astype(BF))
    rank = jnp.where(routed, rank, -1.0)
    acc[...] = jnp.zeros(acc.shape, F32)

    for e in range(ne):
        c16 = c16_ref[i * ne + e]
        b0 = base_ref[i * ne + e]
        for cond, off, sz, j in _segments(c16, sizes):
            @pl.when(cond)
            def _():
                seg_copy(e, b0, off, sz, j).wait()
        for c in range(tt // rc):
            @pl.when(c * rc < c16)
            def _():
                slot = (lax.broadcasted_iota(jnp.int32, (tt, rc), 1) + c * rc).astype(F32)
                sel = slot == rank[:, e:e + 1]
                rows = _dot(jnp.where(sel, 1.0, 0.0).astype(BF), ystage[e, c * rc:(c + 1) * rc, :])
                acc[...] += gtok[:, e:e + 1] * rows

    f_ref[...] = acc[...]


def _combine(base, c16, gates_t, y, tt):
    ne, n = gates_t.shape
    d = y.shape[1]
    sizes = _seg_sizes(tt)
    grid_spec = pltpu.PrefetchScalarGridSpec(
        num_scalar_prefetch=2,
        grid=(n // tt,),
        in_specs=[pl.BlockSpec((ne, tt), lambda i, b, c: (0, i)),
                  pl.BlockSpec(memory_space=pl.ANY)],
        out_specs=pl.BlockSpec((tt, d), lambda i, b, c: (i, 0)),
        scratch_shapes=[pltpu.VMEM((ne, tt, d), BF),
                        pltpu.VMEM((tt, d), F32),
                        pltpu.SemaphoreType.DMA((ne, len(sizes)))])
    return pl.pallas_call(
        functools.partial(_combine_kernel, ne=ne, tt=tt, rc=min(MOE_RC, tt), sizes=sizes),
        out_shape=jax.ShapeDtypeStruct((n, d), F32),
        grid_spec=grid_spec,
        compiler_params=_cparams(("arbitrary",)),
        name="moe_combine",
    )(base, c16, gates_t, y)


def _moe(gates_t, h_all, w_gu, w_down, layer):
    ne, n = gates_t.shape
    tt, tmg = MOE_TT, MOE_TMG
    nt = n // tt
    cnt = _route_counts(gates_t, tt)[:, :, 0].astype(jnp.int32)
    c16 = (cnt + (MOE_ALIGN - 1)) // MOE_ALIGN * MOE_ALIGN
    region = (jnp.sum(c16, axis=0) + (tmg - 1)) // tmg * tmg
    reg_end = jnp.cumsum(region)
    base = (reg_end - region)[None, :] + jnp.cumsum(c16, axis=0) - c16
    s_max = -(-(TOP_K * n + (MOE_ALIGN - 1) * nt * ne + tmg * ne) // tmg) * tmg
    n_used = (reg_end[-1:] // tmg).astype(jnp.int32)
    tile_start = jnp.arange(s_max // tmg, dtype=jnp.int32) * tmg
    tile_expert = jnp.minimum(jnp.sum(tile_start[:, None] >= reg_end[None, :], axis=1), ne - 1).astype(jnp.int32)
    base = base.reshape(-1).astype(jnp.int32)
    c16 = c16.reshape(-1).astype(jnp.int32)
    xs = _dispatch(base, c16, gates_t, h_all, s_max, tt)
    y = _grouped_ffn(tile_expert, n_used, xs, w_gu, w_down, layer)
    return _combine(base, c16, gates_t, y, tt)


def _ln_res_kernel(f_ref, x_ref, g_ref, sc_ref, sh_ref, lng_ref, lnb_ref, *refs, alpha, has_next):
    x2 = _post_ln(x_ref[...], f_ref[...], g_ref[...], lng_ref[...], lnb_ref[...], alpha)
    refs[0][...] = x2
    if has_next:
        refs[1][...] = (x2 * (1.0 + sc_ref[...]) + sh_ref[...]).astype(refs[1].dtype)


def _ln_residual(grp, f_all, row0, x, gate, sc, sh, ln_g, ln_b, alpha, has_next=True):
    n, d = x.shape
    tm = grp.tm
    assert row0 % tm == 0
    blk0 = row0 // tm
    row = pl.BlockSpec((tm, d), lambda m: (m, 0))
    vec = pl.BlockSpec((1, d), lambda m: (0, 0))
    out_shape = [jax.ShapeDtypeStruct((n, d), F32)]
    out_specs = [row]
    if has_next:
        out_shape.append(jax.ShapeDtypeStruct((n, d), BF))
        out_specs.append(row)
    res = pl.pallas_call(
        functools.partial(_ln_res_kernel, alpha=alpha, has_next=has_next),
        out_shape=out_shape,
        grid=(grp.nm,),
        in_specs=[pl.BlockSpec((tm, d), lambda m: (blk0 + m, 0)), row,
                  grp.mod_spec(d, 0, 1), grp.mod_spec(d, 0, 1), grp.mod_spec(d, 0, 1), vec, vec],
        out_specs=out_specs,
        compiler_params=_cparams(("arbitrary",)),
        name="moe_ln_residual",
    )(f_all, x, gate, sc, sh, ln_g.reshape(1, d), ln_b.reshape(1, d))
    return (res[0], res[1]) if has_next else (res[0], None)


def _to_time_major(a):
    a = jnp.swapaxes(a, 0, 1)
    return a.reshape((a.shape[0] * a.shape[1],) + a.shape[2:])


def _from_time_major(a, db):
    a = a.reshape((a.shape[0] // db, db) + a.shape[1:])
    return jnp.swapaxes(a, 0, 1)


def kernel(x_prompt, x_sample, cache_k, cache_v, cache_lf, state_conv_b, state_conv_c, state_h, page_table, c_prompt, c_sample, w_ada, b_ada, ln_g, ln_b, fox_w_in, fox_b_f, fox_w_out, sc_w_in, sc_w_conv, sc_w_out, lru_w_in, lru_conv_w, lru_conv_b, lru_w_a, lru_b_a, lru_w_x, lru_b_x, lru_lam, lru_w_out, ffn_w_gu, ffn_w_down, moe_w_router, moe_w_gu, moe_w_down):
    b, t, d = x_prompt.shape
    db, dt, _ = x_sample.shape
    depth = w_ada.shape[0]
    nh = fox_b_f.shape[1]
    hd = d // nh
    assert hd == FOX_HEAD_DIM and d % LANE_V7X == 0 and db % SUBLANE_V7X == 0
    alpha = (2.0 * depth) ** 0.25
    n_fox = cache_k.shape[0]
    n_pool, page = cache_k.shape[1], cache_k.shape[2]
    ns = db * dt

    gp = _Group(b * t, _pick(t, (1024, 512, 256, 128)), t, 1, b)
    gs = _Group(ns, ns, ns, db, 1)

    mods = _ada(jnp.concatenate([c_prompt, c_sample], axis=0), w_ada, b_ada)
    mods_p = mods[:, :b].reshape(depth, b, 6, d)
    mods_s = jnp.tile(mods[:, b:].reshape(depth, db, 6, d), (1, dt, 1, 1))

    def mod(layer, which, comp):
        if which == 0:
            return mods_p[layer, :, comp][:, None, :]
        return mods_s[layer, :, comp][None]

    cache_k4 = cache_k.reshape(n_fox, n_pool, page * nh, hd)
    cache_v4 = cache_v.reshape(n_fox, n_pool, page * nh, hd)

    groups = (gp, gs)
    xs = [x_prompt.reshape(b * t, d), _to_time_major(x_sample)]
    hm = [_modulate(groups[w], xs[w], mod(0, w, 1), mod(0, w, 0)) for w in range(2)]
    outs = [dict(k=[], v=[], lf=[], cb=[], cc=[], h=[]) for _ in range(2)]
    x1, hf, gates_t = [None, None], [None, None], [None, None]

    for i in range(depth):
        kind, j = i % N_MIXERS, i // N_MIXERS
        last = i == depth - 1
        is_moe = i % 2 == 1
        for w in range(2):
            grp = groups[w]
            o = outs[w]
            if kind == 0:
                (qb,) = _linear(grp, hm[w], fox_w_in, j, 0, d, [F32 if w else BF], scale=1.0 if w else Q_SCALE)
                k32, kb = _linear(grp, hm[w], fox_w_in, j, d, d, [F32, BF])
                v32, vb = _linear(grp, hm[w], fox_w_in, j, 2 * d, d, [F32, BF])
                lf = _forget(grp, hm[w], fox_w_in[j][:, 3 * d:], fox_b_f[j])
                if w == 0:
                    fcum = _cumsum_time(jnp.swapaxes(lf.reshape(b, t, nh), 1, 2))
                    a_mix = _attention_prompt(qb, kb, vb, fcum.reshape(b, nh // 2, 2, t), b, t)
                    o['k'].append(k32.reshape(b, t, nh, hd))
                    o['v'].append(v32.reshape(b, t, nh, hd))
                    o['lf'].append(lf.reshape(b, t, nh))
                else:
                    nk = 2 * SUBLANE_V7X
                    qr = SUBLANE_V7X
                    assert dt <= qr
                    k_b = _from_time_major(k32, db)
                    v_b = _from_time_major(v32, db)
                    lf_b = _from_time_major(lf, db)
                    pad = ((0, 0), (0, (nk - dt) * nh), (0, 0))
                    q_hm = jnp.swapaxes(_from_time_major(qb, db).reshape(db, dt, nh, hd), 1, 2)
                    q_hm = jnp.pad(q_hm, ((0, 0), (0, 0), (0, qr - dt), (0, 0)))
                    lf_new_t = jnp.pad(jnp.swapaxes(lf_b, 1, 2), ((0, 0), (0, 0), (0, LANE_V7X - dt)))
                    att = _attention_sample(j, page_table, q_hm, jnp.pad(k_b.reshape(db, dt * nh, hd), pad),
                                            jnp.pad(v_b.reshape(db, dt * nh, hd), pad), lf_new_t,
                                            cache_k4, cache_v4, cache_lf)
                    att = jnp.swapaxes(att.reshape(db, nh, qr, hd)[:, :, :dt], 1, 2).reshape(db, dt, d)
                    a_mix = _to_time_major(att).astype(BF)
                    o['k'].append(k_b.reshape(db, dt, nh, hd))
                    o['v'].append(v_b.reshape(db, dt, nh, hd))
                    o['lf'].append(lf_b)
                w_out = fox_w_out
            elif kind == 1:
                width = sc_w_conv.shape[1]
                if w == 0:
                    init = jnp.zeros((1, SUBLANE_V7X, d), F32)
                else:
                    init = _to_time_major(state_conv_b[j])[None]
                a_mix, tail = _short_conv(grp, hm[w], sc_w_in, j, sc_w_conv[j], init)
                if w == 0:
                    o['cb'].append(tail[:, SUBLANE_V7X - (width - 1):, :])
                else:
                    o['cb'].append(_from_time_major(tail[0], db))
                w_out = sc_w_out
            else:
                width = lru_conv_w.shape[1]
                if w == 0:
                    init = jnp.zeros((1, SUBLANE_V7X, d), F32)
                    h0 = jnp.zeros((1, SUBLANE_V7X, d), F32)
                else:
                    init = _to_time_major(state_conv_c[j])[None]
                    h0 = state_h[j][None]
                a_mix, tail, hl = _rglru(grp, hm[w], lru_w_in, j, lru_conv_w[j], lru_conv_b[j], lru_w_a[j],
                                         lru_b_a[j], lru_w_x[j], lru_b_x[j], lru_lam[j], init, h0)
                if w == 0:
                    o['cc'].append(tail[:, SUBLANE_V7X - (width - 1):, :])
                    o['h'].append(hl[:, 0, :])
                else:
                    o['cc'].append(_from_time_major(tail[0], db))
                    o['h'].append(hl[0])
                w_out = lru_w_out

            res = _out_ln(grp, a_mix, w_out, j, xs[w], mod(i, w, 2), mod(i, w, 4), mod(i, w, 3),
                          ln_g[i, 0], ln_b[i, 0], alpha, moe_w_router[i // 2] if is_moe else None)
            x1[w], hf[w] = res[0], res[1]
            if is_moe:
                gates_t[w] = res[2]

        nxt = [(mod(i + 1, w, 1), mod(i + 1, w, 0)) if not last else (mod(i, w, 1), mod(i, w, 0)) for w in range(2)]
        if is_moe:
            n_real = b * t + ns
            n_all = -(-n_real // MOE_TT) * MOE_TT
            h_all = jnp.concatenate([hf[0], hf[1], jnp.zeros((n_all - n_real, d), BF)], axis=0)
            g_all = jnp.concatenate([gates_t[0], gates_t[1],
                                     jnp.zeros((gates_t[0].shape[0], n_all - n_real), F32)], axis=1)
            f_all = _moe(g_all, h_all, moe_w_gu, moe_w_down, i // 2)
            for w in range(2):
                xs[w], hm[w] = _ln_residual(groups[w], f_all, w * b * t, x1[w], mod(i, w, 5), nxt[w][0], nxt[w][1],
                                            ln_g[i, 1], ln_b[i, 1], alpha, has_next=not last)
        else:
            for w in range(2):
                xs[w], hm[w] = _ffn(groups[w], hf[w], ffn_w_gu, ffn_w_down, i // 2, x1[w], mod(i, w, 5),
                                    nxt[w][0], nxt[w][1], ln_g[i, 1], ln_b[i, 1], alpha, has_next=not last)

    y_prompt = xs[0].reshape(b, t, d)
    y_sample = _from_time_major(xs[1], db)
    op, os_ = outs
    return (y_prompt, y_sample,
            jnp.stack(op['k']), jnp.stack(op['v']), jnp.stack(op['lf']),
            jnp.stack(op['cb']), jnp.stack(op['cc']), jnp.stack(op['h']),
            jnp.stack(os_['k']), jnp.stack(os_['v']), jnp.stack(os_['lf']),
            jnp.stack(os_['cb']), jnp.stack(os_['cc']), jnp.stack(os_['h']))
```

```python
import functools

import jax
import jax.numpy as jnp
from jax import lax
from jax.experimental import pallas as pl
from jax.experimental.pallas import tpu as pltpu

BF = jnp.bfloat16
F32 = jnp.float32

LANE_V7X = 128
SUBLANE_V7X = 8
VMEM_LIMIT_V7X = 52 * 1024 * 1024

FOX_HEAD_DIM = 64
Q_SCALE = FOX_HEAD_DIM ** -0.5
LRU_C = 8.0
LN_EPS = 1e-5
N_MIXERS = 3
TOP_K = 2
PAGES_PER_STEP = 4


def _cparams(sem):
    return pltpu.CompilerParams(dimension_semantics=sem, vmem_limit_bytes=VMEM_LIMIT_V7X)


def _dot(a, b):
    return jnp.dot(a, b, preferred_element_type=F32)


def _dot_nt(a, b):
    return lax.dot_general(a, b, (((1,), (1,)), ((), ())), preferred_element_type=F32)


def _split(a):
    hi = a.astype(BF)
    lo = (a - hi.astype(F32)).astype(BF)
    return hi, lo


def _dot3(a, b):
    ah, al = _split(a)
    bh, bl = _split(b)
    return _dot(ah, bh) + (_dot(ah, bl) + _dot(al, bh))


def _pick(n, cands):
    for c in cands:
        if n % c == 0:
            return c
    return n


class _Group:
    def __init__(self, n_rows, tm, rows_per_seq, stride, n_seq):
        self.n = n_rows
        self.tm = tm
        self.nm = n_rows // tm
        self.rps = rows_per_seq
        self.g = 1 if stride == 1 else tm
        self.stride = stride
        self.tps = rows_per_seq // tm
        self.n_seq = n_seq

    def mod_spec(self, d, m_axis, n_axes):
        tm, rps = self.tm, self.rps

        def idx(*ids):
            return ((ids[m_axis] * tm) // rps, 0, 0)
        return pl.BlockSpec((None, self.g, d), idx)


def _ada_kernel(c_ref, w_ref, b_ref, o_ref):
    c = c_ref[...]
    a = c * jax.nn.sigmoid(c)
    o_ref[...] = _dot3(a, w_ref[...]) + b_ref[...]


def _ada(c_all, w_ada, b_ada):
    depth, d, d6 = w_ada.shape
    s = c_all.shape[0]
    tn = _pick(d6, (1024, 512, 256, 128))
    return pl.pallas_call(
        _ada_kernel,
        out_shape=jax.ShapeDtypeStruct((depth, s, d6), F32),
        grid=(depth, d6 // tn),
        in_specs=[pl.BlockSpec((s, d), lambda l, n: (0, 0)),
                  pl.BlockSpec((None, d, tn), lambda l, n: (l, 0, n)),
                  pl.BlockSpec((None, 1, tn), lambda l, n: (l, 0, n))],
        out_specs=pl.BlockSpec((None, s, tn), lambda l, n: (l, 0, n)),
        compiler_params=_cparams(("arbitrary", "arbitrary")),
        name="ada_mod",
    )(c_all, w_ada, b_ada.reshape(depth, 1, d6))


def _modulate_kernel(x_ref, sc_ref, sh_ref, o_ref):
    o_ref[...] = (x_ref[...] * (1.0 + sc_ref[...]) + sh_ref[...]).astype(BF)


def _modulate(grp, x, sc, sh):
    n, d = x.shape
    tm = grp.tm
    return pl.pallas_call(
        _modulate_kernel,
        out_shape=jax.ShapeDtypeStruct((n, d), BF),
        grid=(grp.nm,),
        in_specs=[pl.BlockSpec((tm, d), lambda m: (m, 0)),
                  grp.mod_spec(d, 0, 1), grp.mod_spec(d, 0, 1)],
        out_specs=pl.BlockSpec((tm, d), lambda m: (m, 0)),
        compiler_params=_cparams(("arbitrary",)),
        name="modulate",
    )(x, sc, sh)


def _linear_kernel(x_ref, w_ref, *refs, scale, n_out):
    outs, wb = refs[:n_out], refs[n_out]

    @pl.when(pl.program_id(1) == 0)
    def _():
        wb[...] = w_ref[...].astype(BF)

    y = _dot(x_ref[...], wb[...])
    if scale != 1.0:
        y = y * scale
    for o in outs:
        o[...] = y.astype(o.dtype)


def _linear(grp, x, w3, layer, col0, n_cols, out_dtypes, scale=1.0):
    n, k = x.shape
    tm = grp.tm
    tn = _pick(n_cols, (512, 256, 128))
    assert col0 % tn == 0
    cb0 = col0 // tn
    outs = [jax.ShapeDtypeStruct((n, n_cols), dt) for dt in out_dtypes]
    return pl.pallas_call(
        functools.partial(_linear_kernel, scale=scale, n_out=len(outs)),
        out_shape=outs,
        grid=(n_cols // tn, grp.nm),
        in_specs=[pl.BlockSpec((tm, k), lambda c, m: (m, 0)),
                  pl.BlockSpec((None, k, tn), lambda c, m: (layer, 0, cb0 + c))],
        out_specs=[pl.BlockSpec((tm, tn), lambda c, m: (m, c)) for _ in outs],
        scratch_shapes=[pltpu.VMEM((k, tn), BF)],
        compiler_params=_cparams(("arbitrary", "arbitrary")),
        name="linear",
    )(x, w3)


def _log_sigmoid(z):
    return jnp.minimum(z, 0.0) - jnp.log1p(jnp.exp(-jnp.abs(z)))


def _forget_kernel(x_ref, w_ref, b_ref, o_ref):
    z = _dot(x_ref[...], w_ref[...].astype(BF)) + b_ref[...]
    o_ref[...] = _log_sigmoid(z)


def _forget(grp, x, w_f, b_f):
    n, k = x.shape
    h = w_f.shape[1]
    tm = grp.tm
    return pl.pallas_call(
        _forget_kernel,
        out_shape=jax.ShapeDtypeStruct((n, h), F32),
        grid=(grp.nm,),
        in_specs=[pl.BlockSpec((tm, k), lambda m: (m, 0)),
                  pl.BlockSpec((k, h), lambda m: (0, 0)),
                  pl.BlockSpec((1, h), lambda m: (0, 0))],
        out_specs=pl.BlockSpec((tm, h), lambda m: (m, 0)),
        compiler_params=_cparams(("arbitrary",)),
        name="forget_gate",
    )(x, w_f, b_f.reshape(1, h))


def _prefix_lanes(x):
    n = x.shape[-1]
    lane = lax.broadcasted_iota(jnp.int32, x.shape, x.ndim - 1)
    d = 1
    while d < n:
        x = x + jnp.where(lane >= d, pltpu.roll(x, d, x.ndim - 1), 0.0)
        d *= 2
    return x


def _cumsum_kernel(x_ref, o_ref):
    o_ref[...] = _prefix_lanes(x_ref[...])


def _cumsum_time(lf_t):
    b, h, t = lf_t.shape
    return pl.pallas_call(
        _cumsum_kernel,
        out_shape=jax.ShapeDtypeStruct((b, h, t), F32),
        grid=(b,),
        in_specs=[pl.BlockSpec((None, h, t), lambda i: (i, 0, 0))],
        out_specs=pl.BlockSpec((None, h, t), lambda i: (i, 0, 0)),
        compiler_params=_cparams(("arbitrary",)),
        name="forget_cumsum",
    )(lf_t)


def _attn_kernel(q_ref, k_ref, v_ref, f_ref, o_ref, m_sc, l_sc, acc_sc, *, tq):
    qi = pl.program_id(2)
    half = LANE_V7X // 2
    lane = lax.broadcasted_iota(jnp.int32, (tq, LANE_V7X), 1)
    lo = lane < half
    q = q_ref[...]
    zero = jnp.zeros_like(q)
    qs = (jnp.where(lo, q, zero), jnp.where(lo, zero, q))
    for h in range(2):
        m_sc[h] = jnp.full((tq, LANE_V7X), -jnp.inf, F32)
        l_sc[h] = jnp.zeros((tq, LANE_V7X), F32)
        acc_sc[h] = jnp.zeros((tq, LANE_V7X), F32)

    def block(k0, masked):
        kb = k_ref[pl.ds(k0, tq), :]
        vb = v_ref[pl.ds(k0, tq), :]
        for h in range(2):
            s = _dot_nt(qs[h], kb) - f_ref[h:h + 1, pl.ds(k0, tq)]
            if masked:
                row = lax.broadcasted_iota(jnp.int32, (tq, tq), 0)
                col = lax.broadcasted_iota(jnp.int32, (tq, tq), 1)
                s = jnp.where(col <= row, s, -jnp.inf)
            m_prev = m_sc[h]
            m_new = jnp.maximum(m_prev, jnp.max(s, axis=1, keepdims=True))
            alpha = jnp.exp(m_prev - m_new)
            p = jnp.exp(s - m_new[:, 0:1])
            l_sc[h] = alpha * l_sc[h] + jnp.sum(p, axis=1, keepdims=True)
            acc_sc[h] = alpha * acc_sc[h] + _dot(p.astype(BF), vb)
            m_sc[h] = m_new

    def body(i, carry):
        block(pl.multiple_of(i * tq, tq), False)
        return carry

    lax.fori_loop(0, qi, body, 0)
    block(pl.multiple_of(qi * tq, tq), True)
    o = jnp.where(lo, acc_sc[0] / l_sc[0], acc_sc[1] / l_sc[1])
    o_ref[...] = o.astype(o_ref.dtype)


def _attention_prompt(q, k, v, fcum, b, t):
    n, d = q.shape
    hp = d // LANE_V7X
    tq = _pick(t, (512, 256, 128))
    nq = t // tq
    return pl.pallas_call(
        functools.partial(_attn_kernel, tq=tq),
        out_shape=jax.ShapeDtypeStruct((n, d), BF),
        grid=(b, hp, nq),
        in_specs=[pl.BlockSpec((tq, LANE_V7X), lambda i, j, qi: (i * nq + qi, j)),
                  pl.BlockSpec((t, LANE_V7X), lambda i, j, qi: (i, j)),
                  pl.BlockSpec((t, LANE_V7X), lambda i, j, qi: (i, j)),
                  pl.BlockSpec((None, None, 2, t), lambda i, j, qi: (i, j, 0, 0))],
        out_specs=pl.BlockSpec((tq, LANE_V7X), lambda i, j, qi: (i * nq + qi, j)),
        scratch_shapes=[pltpu.VMEM((2, tq, LANE_V7X), F32)] * 3,
        compiler_params=_cparams(("arbitrary", "arbitrary", "arbitrary")),
        name="fox_attention_prompt",
    )(q, k, v, fcum)


def _split3(x):
    hi = x.astype(BF)
    r1 = x - hi.astype(F32)
    mid = r1.astype(BF)
    lo = (r1 - mid.astype(F32)).astype(BF)
    return hi, mid, lo


def _decode_kernel(pt_ref, q_ref, kn_ref, vn_ref, lfn_ref, *refs, pp, nh, page, nk):
    k_refs = refs[:pp]
    v_refs = refs[pp:2 * pp]
    lf_refs = refs[2 * pp:3 * pp]
    o_ref, m_sc, l_sc, acc_sc, c_sc, kexp_sc, mask_sc = refs[3 * pp:]
    g = pl.program_id(1)
    r, hd = q_ref.shape
    dt = r // nh

    def head_mask(n_cols, causal):
        row = lax.broadcasted_iota(jnp.int32, (r, n_cols), 0)
        col = lax.broadcasted_iota(jnp.int32, (r, n_cols), 1)
        keep = jnp.where(lax.rem(col, nh) == lax.rem(row, nh), 0.0, -jnp.inf)
        if causal:
            keep = jnp.where(lax.div(col, nh) <= lax.div(row, nh), keep, -jnp.inf)
        return keep

    def key_expand(n_keys):
        key = lax.broadcasted_iota(jnp.int32, (n_keys, n_keys * nh), 0)
        col = lax.broadcasted_iota(jnp.int32, (n_keys, n_keys * nh), 1)
        return jnp.where(lax.div(col, nh) == key, 1.0, 0.0).astype(BF)

    @pl.when(g == 0)
    def _():
        m_sc[...] = jnp.full(m_sc.shape, -jnp.inf, F32)
        l_sc[...] = jnp.zeros(l_sc.shape, F32)
        acc_sc[...] = jnp.zeros(acc_sc.shape, F32)
        c_sc[...] = jnp.zeros(c_sc.shape, F32)
        kexp_sc[...] = key_expand(page)
        mask_sc[...] = head_mask(page * nh, False)

    qb = (q_ref[...] * Q_SCALE).astype(BF)

    def flat(ref):
        x = ref[...]
        return x.reshape(x.shape[0] * x.shape[1], x.shape[2]).astype(BF)

    def bias_tile(cum_blk, kexp, mask):
        p1, p2, p3 = _split3(cum_blk)
        per_head = _dot(p1, kexp) + _dot(p2, kexp) + _dot(p3, kexp)
        return mask - jnp.concatenate([per_head] * dt, axis=0)

    def update(s, v_blocks, width):
        m_prev = m_sc[...]
        m_new = jnp.maximum(m_prev, jnp.max(s, axis=1, keepdims=True))
        alpha = jnp.exp(m_prev - m_new)
        p = jnp.exp(s - m_new[:, 0:1])
        l_sc[...] = alpha * l_sc[...] + jnp.sum(p, axis=1, keepdims=True)
        pv = None
        for i, vb in enumerate(v_blocks):
            term = _dot(p[:, i * width:(i + 1) * width].astype(BF), vb)
            pv = term if pv is None else pv + term
        acc_sc[...] = alpha[:, 0:hd] * acc_sc[...] + pv
        m_sc[...] = m_new

    eye = jnp.where(lax.broadcasted_iota(jnp.int32, (nh, nh), 0) == lax.broadcasted_iota(jnp.int32, (nh, nh), 1),
                    1.0, 0.0).astype(BF)

    def heads_major(lf):
        p1, p2, p3 = _split3(lf)
        return _dot_nt(eye, p1) + _dot_nt(eye, p2) + _dot_nt(eye, p3)

    lf_t = jnp.concatenate([heads_major(lf_refs[i][...]) for i in range(pp)], axis=1)
    cum = _prefix_lanes(lf_t) + c_sc[:, 0:1]
    c_sc[...] = c_sc[...] + jnp.sum(lf_t, axis=1, keepdims=True)
    kexp = kexp_sc[...]
    mask = mask_sc[...]
    s = jnp.concatenate([_dot_nt(qb, flat(k_refs[i])) + bias_tile(cum[:, i * page:(i + 1) * page], kexp, mask)
                         for i in range(pp)], axis=1)
    update(s, [flat(v_refs[i]) for i in range(pp)], page * nh)

    @pl.when(g == pl.num_programs(1) - 1)
    def _():
        cum_n = (_prefix_lanes(lfn_ref[...]) + c_sc[:, 0:1])[:, 0:nk]
        sn = _dot_nt(qb, flat(kn_ref)) + bias_tile(cum_n, key_expand(nk), head_mask(nk * nh, True))
        update(sn, [flat(vn_ref)], nk * nh)
        o_ref[...] = acc_sc[...] / l_sc[:, 0:hd]


def _attention_sample(layer, page_table, q_rows, k_new, v_new, lf_new_t, cache_k, cache_v, cache_lf):
    db, r, hd = q_rows.shape
    nk, nh = k_new.shape[1], k_new.shape[2]
    page = cache_lf.shape[2]
    n_pages = page_table.shape[1]
    pp = _pick(n_pages, (PAGES_PER_STEP, 2, 1))

    def page_spec(shape, i):
        zeros = (0,) * len(shape)
        return pl.BlockSpec((None, None) + shape, lambda b, g, pt: (layer, pt[b, g * pp + i]) + zeros)

    in_specs = [pl.BlockSpec((None, r, hd), lambda b, g, pt: (b, 0, 0)),
                pl.BlockSpec((None, nk, nh, hd), lambda b, g, pt: (b, 0, 0, 0)),
                pl.BlockSpec((None, nk, nh, hd), lambda b, g, pt: (b, 0, 0, 0)),
                pl.BlockSpec((None, nh, LANE_V7X), lambda b, g, pt: (b, 0, 0))]
    in_specs += [page_spec((page, nh, hd), i) for i in range(pp)]
    in_specs += [page_spec((page, nh, hd), i) for i in range(pp)]
    in_specs += [page_spec((page, nh), i) for i in range(pp)]
    grid_spec = pltpu.PrefetchScalarGridSpec(
        num_scalar_prefetch=1,
        grid=(db, n_pages // pp),
        in_specs=in_specs,
        out_specs=pl.BlockSpec((None, r, hd), lambda b, g, pt: (b, 0, 0)),
        scratch_shapes=[pltpu.VMEM((r, LANE_V7X), F32),
                        pltpu.VMEM((r, LANE_V7X), F32),
                        pltpu.VMEM((r, hd), F32),
                        pltpu.VMEM((nh, LANE_V7X), F32),
                        pltpu.VMEM((page, page * nh), BF),
                        pltpu.VMEM((r, page * nh), F32)])
    return pl.pallas_call(
        functools.partial(_decode_kernel, pp=pp, nh=nh, page=page, nk=nk),
        out_shape=jax.ShapeDtypeStruct((db, r, hd), F32),
        grid_spec=grid_spec,
        compiler_params=_cparams(("arbitrary", "arbitrary")),
        name="fox_attention_sample",
    )(page_table, q_rows, k_new, v_new, lf_new_t,
      *([cache_k] * pp), *([cache_v] * pp), *([cache_lf] * pp))


def _conv_tile(e_ref, carry_ref, init_ref, tail_ref, u, w_ref, first, *, hb, stride, width):
    tm = u.shape[0]

    @pl.when(first)
    def _():
        e_ref[0:hb, :] = init_ref[...]

    @pl.when(jnp.logical_not(first))
    def _():
        e_ref[0:hb, :] = carry_ref[...]

    e_ref[hb:hb + tm, :] = u
    y = None
    for j in range(width):
        off = hb - (width - 1 - j) * stride
        term = w_ref[j:j + 1, :] * e_ref[off:off + tm, :]
        y = term if y is None else y + term
    tail = e_ref[tm:tm + hb, :]
    carry_ref[...] = tail
    tail_ref[...] = tail
    return y


def _sc_kernel(x_ref, wb_ref, wc_ref, wv_ref, wconv_ref, init_ref, mix_ref, tail_ref,
               wb, e_sc, carry, *, tps, hb, stride, width):
    m = pl.program_id(1)

    @pl.when(m == 0)
    def _():
        wb[0] = wb_ref[...].astype(BF)
        wb[1] = wc_ref[...].astype(BF)
        wb[2] = wv_ref[...].astype(BF)

    x = x_ref[...]
    g_b = _dot(x, wb[0])
    u = _dot(x, wb[1]) * _dot(x, wb[2])
    y = _conv_tile(e_sc, carry, init_ref, tail_ref, u, wconv_ref, (m % tps) == 0,
                   hb=hb, stride=stride, width=width)
    mix_ref[...] = (g_b * y).astype(mix_ref.dtype)


def _short_conv(grp, x, w_in, layer, w_conv, init):
    n, d = x.shape
    tm = grp.tm
    width = w_conv.shape[0]
    hb = init.shape[1]
    tn = _pick(d, (256, 128))
    nb = d // tn
    tps = grp.tps
    n_init = init.shape[0]

    def wspec(part):
        return pl.BlockSpec((None, d, tn), lambda c, m: (layer, 0, part * nb + c))

    def seq_of(m):
        return m // tps

    return pl.pallas_call(
        functools.partial(_sc_kernel, tps=tps, hb=hb, stride=grp.stride, width=width),
        out_shape=[jax.ShapeDtypeStruct((n, d), BF),
                   jax.ShapeDtypeStruct((grp.n_seq, hb, d), F32)],
        grid=(nb, grp.nm),
        in_specs=[pl.BlockSpec((tm, d), lambda c, m: (m, 0)),
                  wspec(0), wspec(1), wspec(2),
                  pl.BlockSpec((width, tn), lambda c, m: (0, c)),
                  pl.BlockSpec((None, hb, tn), lambda c, m: (seq_of(m) % n_init, 0, c))],
        out_specs=[pl.BlockSpec((tm, tn), lambda c, m: (m, c)),
                   pl.BlockSpec((None, hb, tn), lambda c, m: (seq_of(m), 0, c))],
        scratch_shapes=[pltpu.VMEM((3, d, tn), BF),
                        pltpu.VMEM((hb + tm, tn), F32),
                        pltpu.VMEM((hb, tn), F32)],
        compiler_params=_cparams(("arbitrary", "arbitrary")),
        name="short_conv_mixer",
    )(x, w_in, w_in, w_in, w_conv, init)


def _softplus(x):
    return jnp.maximum(x, 0.0) + jnp.log1p(jnp.exp(-jnp.abs(x)))


def _neg_expm1(x):
    th = jnp.tanh(0.5 * x)
    return -2.0 * th / (1.0 - th)


def _scan_rows(a, u):
    tm = a.shape[0]
    row = lax.broadcasted_iota(jnp.int32, a.shape, 0)
    d = 1
    while d < tm:
        keep = row >= d
        a_s = jnp.where(keep, pltpu.roll(a, d, 0), 1.0)
        u_s = jnp.where(keep, pltpu.roll(u, d, 0), 0.0)
        u = u + a * u_s
        a = a * a_s
        d *= 2
    return a, u


def _lru_kernel(x_ref, wg_ref, wr_ref, cw_ref, cb_ref, bda_ref, bdx_ref, ba_ref, bx_ref, lam_ref,
                init_ref, h0_ref, mix_ref, tail_ref, hlast_ref,
                wb, bdb, e_sc, carry, hcarry, *, tps, hb, stride, width):
    m = pl.program_id(1)
    tm = x_ref.shape[0]

    @pl.when(m == 0)
    def _():
        wb[0] = wg_ref[...].astype(BF)
        wb[1] = wr_ref[...].astype(BF)
        bdb[0] = bda_ref[...].astype(BF)
        bdb[1] = bdx_ref[...].astype(BF)

    first = (m % tps) == 0
    x = x_ref[...]
    gate = jax.nn.gelu(_dot(x, wb[0]), approximate=True)
    xr = _dot(x, wb[1])
    xc = _conv_tile(e_sc, carry, init_ref, tail_ref, xr, cw_ref, first,
                    hb=hb, stride=stride, width=width) + cb_ref[...]
    xcb = xc.astype(BF)
    r = jax.nn.sigmoid(_dot(xcb, bdb[0]) + ba_ref[...])
    i = jax.nn.sigmoid(_dot(xcb, bdb[1]) + bx_ref[...])
    log_a = (-LRU_C) * r * _softplus(-lam_ref[...])
    a = jnp.exp(log_a)
    u = jnp.sqrt(_neg_expm1(2.0 * log_a)) * i * xc

    @pl.when(first)
    def _():
        hcarry[...] = h0_ref[...]

    if stride == 1:
        a_cum, h_loc = _scan_rows(a, u)
        hs = h_loc + a_cum * hcarry[0:1, :]
        mix_ref[...] = (gate * hs).astype(mix_ref.dtype)
        h_end = jnp.broadcast_to(hs[tm - 1:tm, :], hcarry.shape)
    else:
        h = hcarry[...]
        for t in range(tm // stride):
            sl = slice(t * stride, (t + 1) * stride)
            h = a[sl] * h + u[sl]
            mix_ref[sl, :] = (gate[sl] * h).astype(mix_ref.dtype)
        h_end = h
    hcarry[...] = h_end
    hlast_ref[...] = h_end


def _block_diag(w, per):
    nblk, bd, _ = w.shape
    eye = jnp.eye(per, dtype=w.dtype)
    return jnp.einsum('cide,ik->cidke', w.reshape(nblk // per, per, bd, bd), eye).reshape(
        nblk // per, per * bd, per * bd)


def _rglru(grp, x, w_in, layer, conv_w, conv_b, w_a, b_a, w_x, b_x, lam, init, h0):
    n, d = x.shape
    rw = conv_w.shape[1]
    tm = grp.tm
    width = conv_w.shape[0]
    hb = init.shape[1]
    hr = h0.shape[1]
    bd = w_a.shape[1]
    tn = _pick(rw, (256, 128))
    nb = rw // tn
    tps = grp.tps
    n_init = init.shape[0]
    bda = _block_diag(w_a, tn // bd)
    bdx = _block_diag(w_x, tn // bd)

    def row_spec():
        return pl.BlockSpec((1, tn), lambda c, m: (0, c))

    def seq_of(m):
        return m // tps

    return pl.pallas_call(
        functools.partial(_lru_kernel, tps=tps, hb=hb, stride=grp.stride, width=width),
        out_shape=[jax.ShapeDtypeStruct((n, rw), BF),
                   jax.ShapeDtypeStruct((grp.n_seq, hb, rw), F32),
                   jax.ShapeDtypeStruct((grp.n_seq, hr, rw), F32)],
        grid=(nb, grp.nm),
        in_specs=[pl.BlockSpec((tm, d), lambda c, m: (m, 0)),
                  pl.BlockSpec((None, d, tn), lambda c, m: (layer, 0, c)),
                  pl.BlockSpec((None, d, tn), lambda c, m: (layer, 0, nb + c)),
                  pl.BlockSpec((width, tn), lambda c, m: (0, c)),
                  row_spec(),
                  pl.BlockSpec((None, tn, tn), lambda c, m: (c, 0, 0)),
                  pl.BlockSpec((None, tn, tn), lambda c, m: (c, 0, 0)),
                  row_spec(), row_spec(), row_spec(),
                  pl.BlockSpec((None, hb, tn), lambda c, m: (seq_of(m) % n_init, 0, c)),
                  pl.BlockSpec((None, hr, tn), lambda c, m: (seq_of(m) % n_init, 0, c))],
        out_specs=[pl.BlockSpec((tm, tn), lambda c, m: (m, c)),
                   pl.BlockSpec((None, hb, tn), lambda c, m: (seq_of(m), 0, c)),
                   pl.BlockSpec((None, hr, tn), lambda c, m: (seq_of(m), 0, c))],
        scratch_shapes=[pltpu.VMEM((2, d, tn), BF),
                        pltpu.VMEM((2, tn, tn), BF),
                        pltpu.VMEM((hb + tm, tn), F32),
                        pltpu.VMEM((hb, tn), F32),
                        pltpu.VMEM((hr, tn), F32)],
        compiler_params=_cparams(("arbitrary", "arbitrary")),
        name="rglru_mixer",
    )(x, w_in, w_in, conv_w, conv_b.reshape(1, rw), bda, bdx, b_a.reshape(1, rw),
      b_x.reshape(1, rw), lam.reshape(1, rw), init, h0)


def _post_ln(x, f, gate, ln_g, ln_b, alpha):
    z = alpha * x + (1.0 + gate) * f
    mu = jnp.mean(z, axis=-1, keepdims=True)
    zc = z - mu
    var = jnp.mean(zc * zc, axis=-1, keepdims=True)
    return zc * lax.rsqrt(var + LN_EPS) * ln_g + ln_b


def _top2_gates_t(logits_t):
    ne = float(logits_t.shape[0])
    idx = lax.broadcasted_iota(jnp.int32, logits_t.shape, 0).astype(F32)
    m1 = jnp.max(logits_t, axis=0, keepdims=True)
    i1 = jnp.min(jnp.where(logits_t == m1, idx, ne), axis=0, keepdims=True)
    sel1 = idx == i1
    rest = jnp.where(sel1, -jnp.inf, logits_t)
    m2 = jnp.max(rest, axis=0, keepdims=True)
    i2 = jnp.min(jnp.where(rest == m2, idx, ne), axis=0, keepdims=True)
    sel2 = idx == i2
    e2 = jnp.exp(m2 - m1)
    w1 = 1.0 / (1.0 + e2)
    w2 = e2 / (1.0 + e2)
    return jnp.where(sel1, w1, 0.0) + jnp.where(sel2, w2, 0.0)


def _out_ln_kernel(a_ref, w_ref, x_ref, g_ref, sc_ref, sh_ref, lng_ref, lnb_ref, *refs,
                   alpha, has_router):
    if has_router:
        wr_ref, x1_ref, hf_ref, gates_ref, wb = refs
    else:
        x1_ref, hf_ref, wb = refs

    @pl.when(pl.program_id(0) == 0)
    def _():
        wb[...] = w_ref[...].astype(BF)

    y = _dot(a_ref[...], wb[...])
    x1 = _post_ln(x_ref[...], y, g_ref[...], lng_ref[...], lnb_ref[...], alpha)
    x1_ref[...] = x1
    hf = x1 * (1.0 + sc_ref[...]) + sh_ref[...]
    hf_ref[...] = hf.astype(hf_ref.dtype)
    if has_router:
        wh, wl = _split(wr_ref[...])
        hh, hl = _split(hf)
        logits_t = _dot_nt(wh, hh) + (_dot_nt(wh, hl) + _dot_nt(wl, hh))
        gates_ref[...] = _top2_gates_t(logits_t)


def _out_ln(grp, a, w3, layer, x, gate, sc, sh, ln_g, ln_b, alpha, w_router=None):
    n, d = x.shape
    k = a.shape[1]
    tm = grp.tm
    has_router = w_router is not None
    in_specs = [pl.BlockSpec((tm, k), lambda m: (m, 0)),
                pl.BlockSpec((None, k, d), lambda m: (layer, 0, 0)),
                pl.BlockSpec((tm, d), lambda m: (m, 0)),
                grp.mod_spec(d, 0, 1), grp.mod_spec(d, 0, 1), grp.mod_spec(d, 0, 1),
                pl.BlockSpec((1, d), lambda m: (0, 0)),
                pl.BlockSpec((1, d), lambda m: (0, 0))]
    args = [a, w3, x, gate, sc, sh, ln_g.reshape(1, d), ln_b.reshape(1, d)]
    out_shape = [jax.ShapeDtypeStruct((n, d), F32), jax.ShapeDtypeStruct((n, d), BF)]
    out_specs = [pl.BlockSpec((tm, d), lambda m: (m, 0)), pl.BlockSpec((tm, d), lambda m: (m, 0))]
    if has_router:
        ne = w_router.shape[1]
        in_specs.append(pl.BlockSpec((ne, d), lambda m: (0, 0)))
        args.append(w_router.T)
        out_shape.append(jax.ShapeDtypeStruct((ne, n), F32))
        out_specs.append(pl.BlockSpec((ne, tm), lambda m: (0, m)))
    return pl.pallas_call(
        functools.partial(_out_ln_kernel, alpha=alpha, has_router=has_router),
        out_shape=out_shape,
        grid=(grp.nm,),
        in_specs=in_specs,
        out_specs=out_specs,
        scratch_shapes=[pltpu.VMEM((k, d), BF)],
        compiler_params=_cparams(("arbitrary",)),
        name="out_proj_ln",
    )(*args)


def _swiglu_part(h, wg, wu, wd):
    gu = _dot(h, wg.astype(BF))
    up = _dot(h, wu.astype(BF))
    act = (gu * jax.nn.sigmoid(gu) * up).astype(BF)
    return _dot(act, wd.astype(BF))


def _ffn_kernel(h_ref, wg_ref, wu_ref, wd_ref, x_ref, g_ref, sc_ref, sh_ref, lng_ref, lnb_ref, *refs,
                alpha, has_next):
    if has_next:
        x2_ref, hn_ref, acc = refs
    else:
        x2_ref, acc = refs
    ch = pl.program_id(1)
    part = _swiglu_part(h_ref[...], wg_ref[...], wu_ref[...], wd_ref[...])

    @pl.when(ch == 0)
    def _():
        acc[...] = part

    @pl.when(ch > 0)
    def _():
        acc[...] += part

    @pl.when(ch == pl.num_programs(1) - 1)
    def _():
        x2 = _post_ln(x_ref[...], acc[...], g_ref[...], lng_ref[...], lnb_ref[...], alpha)
        x2_ref[...] = x2
        if has_next:
            hn_ref[...] = (x2 * (1.0 + sc_ref[...]) + sh_ref[...]).astype(hn_ref.dtype)


def _ffn(grp, h, w_gu, w_down, layer, x, gate, sc, sh, ln_g, ln_b, alpha, has_next=True):
    n, d = x.shape
    tm = grp.tm
    ff = w_down.shape[1]
    tf = _pick(ff, (256, 128))
    nch = ff // tf
    row = pl.BlockSpec((tm, d), lambda m, c: (m, 0))
    vec = pl.BlockSpec((1, d), lambda m, c: (0, 0))
    in_specs = [row,
                pl.BlockSpec((None, d, tf), lambda m, c: (layer, 0, c)),
                pl.BlockSpec((None, d, tf), lambda m, c: (layer, 0, nch + c)),
                pl.BlockSpec((None, tf, d), lambda m, c: (layer, c, 0)),
                row, grp.mod_spec(d, 0, 2), grp.mod_spec(d, 0, 2), grp.mod_spec(d, 0, 2), vec, vec]
    out_shape = [jax.ShapeDtypeStruct((n, d), F32)]
    out_specs = [row]
    if has_next:
        out_shape.append(jax.ShapeDtypeStruct((n, d), BF))
        out_specs.append(row)
    res = pl.pallas_call(
        functools.partial(_ffn_kernel, alpha=alpha, has_next=has_next),
        out_shape=out_shape,
        grid=(grp.nm, nch),
        in_specs=in_specs,
        out_specs=out_specs,
        scratch_shapes=[pltpu.VMEM((tm, d), F32)],
        compiler_params=_cparams(("arbitrary", "arbitrary")),
        name="ffn_dense",
    )(h, w_gu, w_gu, w_down, x, gate, sc, sh, ln_g.reshape(1, d), ln_b.reshape(1, d))
    return (res[0], res[1]) if has_next else (res[0], None)


MOE_TT = 512
MOE_RC = 128
MOE_TMG = 512
MOE_ALIGN = 16


def _seg_sizes(tt):
    sizes = []
    sz = tt
    while sz >= MOE_ALIGN:
        sizes.append(sz)
        sz //= 2
    return tuple(sizes)


def _segments(c16, sizes):
    off = jnp.int32(0)
    for j, sz in enumerate(sizes):
        cond = (c16 & sz) != 0
        yield cond, off, sz, j
        off = off + jnp.where(cond, sz, 0)


def _route_count_kernel(g_ref, o_ref):
    routed = jnp.where(g_ref[...] > 0.0, 1.0, 0.0)
    o_ref[...] = jnp.broadcast_to(jnp.sum(routed, axis=1, keepdims=True), o_ref.shape)


def _route_counts(gates_t, tt):
    ne, n = gates_t.shape
    nt = n // tt
    return pl.pallas_call(
        _route_count_kernel,
        out_shape=jax.ShapeDtypeStruct((nt, ne, LANE_V7X), F32),
        grid=(nt,),
        in_specs=[pl.BlockSpec((ne, tt), lambda i: (0, i))],
        out_specs=pl.BlockSpec((None, ne, LANE_V7X), lambda i: (i, 0, 0)),
        compiler_params=_cparams(("arbitrary",)),
        name="moe_route_counts",
    )(gates_t)


def _dispatch_kernel(base_ref, c16_ref, g_ref, h_ref, sorted_in_ref, sorted_ref, stage, sems,
                     *, ne, tt, rc, sizes):
    del sorted_in_ref
    i = pl.program_id(0)
    gt = g_ref[...]
    routed = gt > 0.0
    ri = lax.broadcasted_iota(jnp.int32, (tt, tt), 0)
    ci = lax.broadcasted_iota(jnp.int32, (tt, tt), 1)
    before = jnp.where(ri < ci, 1.0, 0.0).astype(BF)
    rank = _dot(jnp.where(routed, 1.0, 0.0).astype(BF), before)
    rank = jnp.where(routed, rank, -1.0)
    h = h_ref[...]

    def seg_copy(e, b0, off, sz, j):
        return pltpu.make_async_copy(
            stage.at[e, pl.ds(pl.multiple_of(off, MOE_ALIGN), sz)],
            sorted_ref.at[pl.ds(pl.multiple_of(b0 + off, MOE_ALIGN), sz)],
            sems.at[e, j])

    for e in range(ne):
        c16 = c16_ref[i * ne + e]
        b0 = base_ref[i * ne + e]
        for c in range(tt // rc):
            @pl.when(c * rc < c16)
            def _():
                slot = (lax.broadcasted_iota(jnp.int32, (rc, tt), 0) + c * rc).astype(F32)
                sel = slot == rank[e:e + 1, :]
                stage[e, c * rc:(c + 1) * rc, :] = _dot(jnp.where(sel, 1.0, 0.0).astype(BF), h).astype(BF)
        for cond, off, sz, j in _segments(c16, sizes):
            @pl.when(cond)
            def _():
                seg_copy(e, b0, off, sz, j).start()

    for e in range(ne):
        c16 = c16_ref[i * ne + e]
        b0 = base_ref[i * ne + e]
        for cond, off, sz, j in _segments(c16, sizes):
            @pl.when(cond)
            def _():
                seg_copy(e, b0, off, sz, j).wait()


def _dispatch(base, c16, gates_t, h_all, s_max, tt):
    ne, n = gates_t.shape
    d = h_all.shape[1]
    sizes = _seg_sizes(tt)
    grid_spec = pltpu.PrefetchScalarGridSpec(
        num_scalar_prefetch=2,
        grid=(n // tt,),
        in_specs=[pl.BlockSpec((ne, tt), lambda i, b, c: (0, i)),
                  pl.BlockSpec((tt, d), lambda i, b, c: (i, 0)),
                  pl.BlockSpec(memory_space=pl.ANY)],
        out_specs=pl.BlockSpec(memory_space=pl.ANY),
        scratch_shapes=[pltpu.VMEM((ne, tt, d), BF),
                        pltpu.SemaphoreType.DMA((ne, len(sizes)))])
    return pl.pallas_call(
        functools.partial(_dispatch_kernel, ne=ne, tt=tt, rc=min(MOE_RC, tt), sizes=sizes),
        out_shape=jax.ShapeDtypeStruct((s_max, d), BF),
        grid_spec=grid_spec,
        input_output_aliases={4: 0},
        compiler_params=_cparams(("arbitrary",)),
        name="moe_dispatch",
    )(base, c16, gates_t, h_all, jnp.zeros((s_max, d), BF))


def _grouped_kernel(te_ref, nu_ref, x_ref, wg_ref, wu_ref, wd_ref, y_ref, acc):
    del te_ref
    g = pl.program_id(0)
    ch = pl.program_id(1)
    last = pl.num_programs(1) - 1

    @pl.when(g < nu_ref[0])
    def _():
        part = _swiglu_part(x_ref[...], wg_ref[...], wu_ref[...], wd_ref[...])

        @pl.when(ch == 0)
        def _():
            acc[...] = part

        @pl.when(ch > 0)
        def _():
            acc[...] += part

        @pl.when(ch == last)
        def _():
            y_ref[...] = acc[...].astype(y_ref.dtype)

    @pl.when(jnp.logical_and(g >= nu_ref[0], ch == last))
    def _():
        y_ref[...] = jnp.zeros(y_ref.shape, y_ref.dtype)


def _grouped_ffn(tile_expert, n_used, xs, w_gu, w_down, layer):
    s_max, d = xs.shape
    ff = w_down.shape[2]
    tmg = MOE_TMG
    tf = _pick(ff, (512, 256, 128))
    nch = ff // tf

    def row(g, c, te, nu):
        return (jnp.minimum(g, nu[0] - 1), 0)

    def chunk(g, c, nu):
        return jnp.where(g < nu[0], c, nch - 1)

    grid_spec = pltpu.PrefetchScalarGridSpec(
        num_scalar_prefetch=2,
        grid=(s_max // tmg, nch),
        in_specs=[pl.BlockSpec((tmg, d), row),
                  pl.BlockSpec((None, None, d, tf), lambda g, c, te, nu: (layer, te[g], 0, chunk(g, c, nu))),
                  pl.BlockSpec((None, None, d, tf), lambda g, c, te, nu: (layer, te[g], 0, nch + chunk(g, c, nu))),
                  pl.BlockSpec((None, None, tf, d), lambda g, c, te, nu: (layer, te[g], chunk(g, c, nu), 0))],
        out_specs=pl.BlockSpec((tmg, d), lambda g, c, te, nu: (g, 0)),
        scratch_shapes=[pltpu.VMEM((tmg, d), F32)])
    return pl.pallas_call(
        _grouped_kernel,
        out_shape=jax.ShapeDtypeStruct((s_max, d), BF),
        grid_spec=grid_spec,
        compiler_params=_cparams(("arbitrary", "arbitrary")),
        name="moe_grouped_ffn",
    )(tile_expert, n_used, xs, w_gu, w_gu, w_down)


def _combine_kernel(base_ref, c16_ref, g_ref, y_ref, f_ref, ystage, acc, sems, *, ne, tt, rc, sizes):
    i = pl.program_id(0)

    @pl.when(i == 0)
    def _():
        ystage[...] = jnp.zeros(ystage.shape, ystage.dtype)

    def seg_copy(e, b0, off, sz, j):
        return pltpu.make_async_copy(
            y_ref.at[pl.ds(pl.multiple_of(b0 + off, MOE_ALIGN), sz)],
            ystage.at[e, pl.ds(pl.multiple_of(off, MOE_ALIGN), sz)],
            sems.at[e, j])

    for e in range(ne):
        c16 = c16_ref[i * ne + e]
        b0 = base_ref[i * ne + e]
        for cond, off, sz, j in _segments(c16, sizes):
            @pl.when(cond)
            def _():
                seg_copy(e, b0, off, sz, j).start()

    ri = lax.broadcasted_iota(jnp.int32, (tt, tt), 0)
    ci = lax.broadcasted_iota(jnp.int32, (tt, tt), 1)
    eye = jnp.where(ri == ci, 1.0, 0.0).astype(BF)
    p1, p2, p3 = _split3(g_ref[...])
    gtok = _dot_nt(eye, p1) + _dot_nt(eye, p2) + _dot_nt(eye, p3)
    routed = gtok > 0.0
    before = jnp.where(ci < ri, 1.0, 0.0).astype(BF)
    rank = _dot(before, jnp.where(routed, 1.0, 0.0).astype(BF))
    rank = jnp.where(routed, rank, -1.0)
    acc[...] = jnp.zeros(acc.shape, F32)

    for e in range(ne):
        c16 = c16_ref[i * ne + e]
        b0 = base_ref[i * ne + e]
        for cond, off, sz, j in _segments(c16, sizes):
            @pl.when(cond)
            def _():
                seg_copy(e, b0, off, sz, j).wait()
        for c in range(tt // rc):
            @pl.when(c * rc < c16)
            def _():
                slot = (lax.broadcasted_iota(jnp.int32, (tt, rc), 1) + c * rc).astype(F32)
                sel = slot == rank[:, e:e + 1]
                rows = _dot(jnp.where(sel, 1.0, 0.0).astype(BF), ystage[e, c * rc:(c + 1) * rc, :])
                acc[...] += gtok[:, e:e + 1] * rows

    f_ref[...] = acc[...]


def _combine(base, c16, gates_t, y, tt):
    ne, n = gates_t.shape
    d = y.shape[1]
    sizes = _seg_sizes(tt)
    grid_spec = pltpu.PrefetchScalarGridSpec(
        num_scalar_prefetch=2,
        grid=(n // tt,),
        in_specs=[pl.BlockSpec((ne, tt), lambda i, b, c: (0, i)),
                  pl.BlockSpec(memory_space=pl.ANY)],
        out_specs=pl.BlockSpec((tt, d), lambda i, b, c: (i, 0)),
        scratch_shapes=[pltpu.VMEM((ne, tt, d), BF),
                        pltpu.VMEM((tt, d), F32),
                        pltpu.SemaphoreType.DMA((ne, len(sizes)))])
    return pl.pallas_call(
        functools.partial(_combine_kernel, ne=ne, tt=tt, rc=min(MOE_RC, tt), sizes=sizes),
        out_shape=jax.ShapeDtypeStruct((n, d), F32),
        grid_spec=grid_spec,
        compiler_params=_cparams(("arbitrary",)),
        name="moe_combine",
    )(base, c16, gates_t, y)


def _moe(gates_t, h_all, w_gu, w_down, layer):
    ne, n = gates_t.shape
    tt, tmg = MOE_TT, MOE_TMG
    nt = n // tt
    cnt = _route_counts(gates_t, tt)[:, :, 0].astype(jnp.int32)
    c16 = (cnt + (MOE_ALIGN - 1)) // MOE_ALIGN * MOE_ALIGN
    region = (jnp.sum(c16, axis=0) + (tmg - 1)) // tmg * tmg
    reg_end = jnp.cumsum(region)
    base = (reg_end - region)[None, :] + jnp.cumsum(c16, axis=0) - c16
    s_max = -(-(TOP_K * n + (MOE_ALIGN - 1) * nt * ne + tmg * ne) // tmg) * tmg
    n_used = (reg_end[-1:] // tmg).astype(jnp.int32)
    tile_start = jnp.arange(s_max // tmg, dtype=jnp.int32) * tmg
    tile_expert = jnp.minimum(jnp.sum(tile_start[:, None] >= reg_end[None, :], axis=1), ne - 1).astype(jnp.int32)
    base = base.reshape(-1).astype(jnp.int32)
    c16 = c16.reshape(-1).astype(jnp.int32)
    xs = _dispatch(base, c16, gates_t, h_all, s_max, tt)
    y = _grouped_ffn(tile_expert, n_used, xs, w_gu, w_down, layer)
    return _combine(base, c16, gates_t, y, tt)


def _ln_res_kernel(f_ref, x_ref, g_ref, sc_ref, sh_ref, lng_ref, lnb_ref, *refs, alpha, has_next):
    x2 = _post_ln(x_ref[...], f_ref[...], g_ref[...], lng_ref[...], lnb_ref[...], alpha)
    refs[0][...] = x2
    if has_next:
        refs[1][...] = (x2 * (1.0 + sc_ref[...]) + sh_ref[...]).astype(refs[1].dtype)


def _ln_residual(grp, f_all, row0, x, gate, sc, sh, ln_g, ln_b, alpha, has_next=True):
    n, d = x.shape
    tm = grp.tm
    assert row0 % tm == 0
    blk0 = row0 // tm
    row = pl.BlockSpec((tm, d), lambda m: (m, 0))
    vec = pl.BlockSpec((1, d), lambda m: (0, 0))
    out_shape = [jax.ShapeDtypeStruct((n, d), F32)]
    out_specs = [row]
    if has_next:
        out_shape.append(jax.ShapeDtypeStruct((n, d), BF))
        out_specs.append(row)
    res = pl.pallas_call(
        functools.partial(_ln_res_kernel, alpha=alpha, has_next=has_next),
        out_shape=out_shape,
        grid=(grp.nm,),
        in_specs=[pl.BlockSpec((tm, d), lambda m: (blk0 + m, 0)), row,
                  grp.mod_spec(d, 0, 1), grp.mod_spec(d, 0, 1), grp.mod_spec(d, 0, 1), vec, vec],
        out_specs=out_specs,
        compiler_params=_cparams(("arbitrary",)),
        name="moe_ln_residual",
    )(f_all, x, gate, sc, sh, ln_g.reshape(1, d), ln_b.reshape(1, d))
    return (res[0], res[1]) if has_next else (res[0], None)


def _to_time_major(a):
    a = jnp.swapaxes(a, 0, 1)
    return a.reshape((a.shape[0] * a.shape[1],) + a.shape[2:])


def _from_time_major(a, db):
    a = a.reshape((a.shape[0] // db, db) + a.shape[1:])
    return jnp.swapaxes(a, 0, 1)


def kernel(x_prompt, x_sample, cache_k, cache_v, cache_lf, state_conv_b, state_conv_c, state_h, page_table, c_prompt, c_sample, w_ada, b_ada, ln_g, ln_b, fox_w_in, fox_b_f, fox_w_out, sc_w_in, sc_w_conv, sc_w_out, lru_w_in, lru_conv_w, lru_conv_b, lru_w_a, lru_b_a, lru_w_x, lru_b_x, lru_lam, lru_w_out, ffn_w_gu, ffn_w_down, moe_w_router, moe_w_gu, moe_w_down):
    b, t, d = x_prompt.shape
    db, dt, _ = x_sample.shape
    depth = w_ada.shape[0]
    nh = fox_b_f.shape[1]
    hd = d // nh
    assert hd == FOX_HEAD_DIM and d % LANE_V7X == 0 and db % SUBLANE_V7X == 0
    alpha = (2.0 * depth) ** 0.25
    n_fox = cache_k.shape[0]
    n_pool, page = cache_k.shape[1], cache_k.shape[2]
    ns = db * dt

    gp = _Group(b * t, _pick(t, (1024, 512, 256, 128)), t, 1, b)
    gs = _Group(ns, ns, ns, db, 1)

    mods = _ada(jnp.concatenate([c_prompt, c_sample], axis=0), w_ada, b_ada)
    mods_p = mods[:, :b].reshape(depth, b, 6, d)
    mods_s = jnp.tile(mods[:, b:].reshape(depth, db, 6, d), (1, dt, 1, 1))

    def mod(layer, which, comp):
        if which == 0:
            return mods_p[layer, :, comp][:, None, :]
        return mods_s[layer, :, comp][None]

    groups = (gp, gs)
    xs = [x_prompt.reshape(b * t, d), _to_time_major(x_sample)]
    hm = [_modulate(groups[w], xs[w], mod(0, w, 1), mod(0, w, 0)) for w in range(2)]
    outs = [dict(k=[], v=[], lf=[], cb=[], cc=[], h=[]) for _ in range(2)]
    x1, hf, gates_t = [None, None], [None, None], [None, None]

    for i in range(depth):
        kind, j = i % N_MIXERS, i // N_MIXERS
        last = i == depth - 1
        is_moe = i % 2 == 1
        for w in range(2):
            grp = groups[w]
            o = outs[w]
            if kind == 0:
                (qb,) = _linear(grp, hm[w], fox_w_in, j, 0, d, [F32 if w else BF], scale=1.0 if w else Q_SCALE)
                k32, kb = _linear(grp, hm[w], fox_w_in, j, d, d, [F32, BF])
                v32, vb = _linear(grp, hm[w], fox_w_in, j, 2 * d, d, [F32, BF])
                lf = _forget(grp, hm[w], fox_w_in[j][:, 3 * d:], fox_b_f[j])
                if w == 0:
                    fcum = _cumsum_time(jnp.swapaxes(lf.reshape(b, t, nh), 1, 2))
                    a_mix = _attention_prompt(qb, kb, vb, fcum.reshape(b, nh // 2, 2, t), b, t)
                    o['k'].append(k32.reshape(b, t, nh, hd))
                    o['v'].append(v32.reshape(b, t, nh, hd))
                    o['lf'].append(lf.reshape(b, t, nh))
                else:
                    nk = 2 * SUBLANE_V7X
                    assert dt <= nk
                    k_b = _from_time_major(k32, db)
                    v_b = _from_time_major(v32, db)
                    lf_b = _from_time_major(lf, db)
                    pad = ((0, 0), (0, nk - dt), (0, 0), (0, 0))
                    lf_new_t = jnp.pad(jnp.swapaxes(lf_b, 1, 2), ((0, 0), (0, 0), (0, LANE_V7X - dt)))
                    att = _attention_sample(j, page_table, _from_time_major(qb, db).reshape(db, dt * nh, hd),
                                            jnp.pad(k_b.reshape(db, dt, nh, hd), pad),
                                            jnp.pad(v_b.reshape(db, dt, nh, hd), pad), lf_new_t,
                                            cache_k, cache_v, cache_lf)
                    a_mix = _to_time_major(att.reshape(db, dt, d)).astype(BF)
                    o['k'].append(k_b.reshape(db, dt, nh, hd))
                    o['v'].append(v_b.reshape(db, dt, nh, hd))
                    o['lf'].append(lf_b)
                w_out = fox_w_out
            elif kind == 1:
                width = sc_w_conv.shape[1]
                if w == 0:
                    init = jnp.zeros((1, SUBLANE_V7X, d), F32)
                else:
                    init = _to_time_major(state_conv_b[j])[None]
                a_mix, tail = _short_conv(grp, hm[w], sc_w_in, j, sc_w_conv[j], init)
                if w == 0:
                    o['cb'].append(tail[:, SUBLANE_V7X - (width - 1):, :])
                else:
                    o['cb'].append(_from_time_major(tail[0], db))
                w_out = sc_w_out
            else:
                width = lru_conv_w.shape[1]
                if w == 0:
                    init = jnp.zeros((1, SUBLANE_V7X, d), F32)
                    h0 = jnp.zeros((1, SUBLANE_V7X, d), F32)
                else:
                    init = _to_time_major(state_conv_c[j])[None]
                    h0 = state_h[j][None]
                a_mix, tail, hl = _rglru(grp, hm[w], lru_w_in, j, lru_conv_w[j], lru_conv_b[j], lru_w_a[j],
                                         lru_b_a[j], lru_w_x[j], lru_b_x[j], lru_lam[j], init, h0)
                if w == 0:
                    o['cc'].append(tail[:, SUBLANE_V7X - (width - 1):, :])
                    o['h'].append(hl[:, 0, :])
                else:
                    o['cc'].append(_from_time_major(tail[0], db))
                    o['h'].append(hl[0])
                w_out = lru_w_out

            res = _out_ln(grp, a_mix, w_out, j, xs[w], mod(i, w, 2), mod(i, w, 4), mod(i, w, 3),
                          ln_g[i, 0], ln_b[i, 0], alpha, moe_w_router[i // 2] if is_moe else None)
            x1[w], hf[w] = res[0], res[1]
            if is_moe:
                gates_t[w] = res[2]

        nxt = [(mod(i + 1, w, 1), mod(i + 1, w, 0)) if not last else (mod(i, w, 1), mod(i, w, 0)) for w in range(2)]
        if is_moe:
            n_real = b * t + ns
            n_all = -(-n_real // MOE_TT) * MOE_TT
            h_all = jnp.concatenate([hf[0], hf[1], jnp.zeros((n_all - n_real, d), BF)], axis=0)
            g_all = jnp.concatenate([gates_t[0], gates_t[1],
                                     jnp.zeros((gates_t[0].shape[0], n_all - n_real), F32)], axis=1)
            f_all = _moe(g_all, h_all, moe_w_gu, moe_w_down, i // 2)
            for w in range(2):
                xs[w], hm[w] = _ln_residual(groups[w], f_all, w * b * t, x1[w], mod(i, w, 5), nxt[w][0], nxt[w][1],
                                            ln_g[i, 1], ln_b[i, 1], alpha, has_next=not last)
        else:
            for w in range(2):
                xs[w], hm[w] = _ffn(groups[w], hf[w], ffn_w_gu, ffn_w_down, i // 2, x1[w], mod(i, w, 5),
                                    nxt[w][0], nxt[w][1], ln_g[i, 1], ln_b[i, 1], alpha, has_next=not last)

    y_prompt = xs[0].reshape(b, t, d)
    y_sample = _from_time_major(xs[1], db)
    op, os_ = outs
    return (y_prompt, y_sample,
            jnp.stack(op['k']), jnp.stack(op['v']), jnp.stack(op['lf']),
            jnp.stack(op['cb']), jnp.stack(op['cc']), jnp.stack(op['h']),
            jnp.stack(os_['k']), jnp.stack(os_['v']), jnp.stack(os_['lf']),
            jnp.stack(os_['cb']), jnp.stack(os_['cc']), jnp.stack(os_['h']))
```

```python
import functools

import jax
import jax.numpy as jnp
from jax import lax
from jax.experimental import pallas as pl
from jax.experimental.pallas import tpu as pltpu

BF = jnp.bfloat16
F32 = jnp.float32

LANE_V7X = 128
SUBLANE_V7X = 8
VMEM_LIMIT_V7X = 52 * 1024 * 1024

FOX_HEAD_DIM = 64
Q_SCALE = FOX_HEAD_DIM ** -0.5
LRU_C = 8.0
LN_EPS = 1e-5
N_MIXERS = 3
TOP_K = 2
PAGES_PER_STEP = 8


def _cparams(sem):
    return pltpu.CompilerParams(dimension_semantics=sem, vmem_limit_bytes=VMEM_LIMIT_V7X)


def _dot(a, b):
    return jnp.dot(a, b, preferred_element_type=F32)


def _dot_nt(a, b):
    return lax.dot_general(a, b, (((1,), (1,)), ((), ())), preferred_element_type=F32)


def _split(a):
    hi = a.astype(BF)
    lo = (a - hi.astype(F32)).astype(BF)
    return hi, lo


def _dot3(a, b):
    ah, al = _split(a)
    bh, bl = _split(b)
    return _dot(ah, bh) + (_dot(ah, bl) + _dot(al, bh))


def _pick(n, cands):
    for c in cands:
        if n % c == 0:
            return c
    return n


class _Group:
    def __init__(self, n_rows, tm, rows_per_seq, stride, n_seq):
        self.n = n_rows
        self.tm = tm
        self.nm = n_rows // tm
        self.rps = rows_per_seq
        self.g = 1 if stride == 1 else tm
        self.stride = stride
        self.tps = rows_per_seq // tm
        self.n_seq = n_seq

    def mod_spec(self, d, m_axis, n_axes):
        tm, rps = self.tm, self.rps

        def idx(*ids):
            return ((ids[m_axis] * tm) // rps, 0, 0)
        return pl.BlockSpec((None, self.g, d), idx)


def _ada_kernel(c_ref, w_ref, b_ref, o_ref):
    c = c_ref[...]
    a = c * jax.nn.sigmoid(c)
    o_ref[...] = _dot3(a, w_ref[...]) + b_ref[...]


def _ada(c_all, w_ada, b_ada):
    depth, d, d6 = w_ada.shape
    s = c_all.shape[0]
    tn = _pick(d6, (1024, 512, 256, 128))
    return pl.pallas_call(
        _ada_kernel,
        out_shape=jax.ShapeDtypeStruct((depth, s, d6), F32),
        grid=(depth, d6 // tn),
        in_specs=[pl.BlockSpec((s, d), lambda l, n: (0, 0)),
                  pl.BlockSpec((None, d, tn), lambda l, n: (l, 0, n)),
                  pl.BlockSpec((None, 1, tn), lambda l, n: (l, 0, n))],
        out_specs=pl.BlockSpec((None, s, tn), lambda l, n: (l, 0, n)),
        compiler_params=_cparams(("arbitrary", "arbitrary")),
        name="ada_mod",
    )(c_all, w_ada, b_ada.reshape(depth, 1, d6))


def _modulate_kernel(x_ref, sc_ref, sh_ref, o_ref):
    o_ref[...] = (x_ref[...] * (1.0 + sc_ref[...]) + sh_ref[...]).astype(BF)


def _modulate(grp, x, sc, sh):
    n, d = x.shape
    tm = grp.tm
    return pl.pallas_call(
        _modulate_kernel,
        out_shape=jax.ShapeDtypeStruct((n, d), BF),
        grid=(grp.nm,),
        in_specs=[pl.BlockSpec((tm, d), lambda m: (m, 0)),
                  grp.mod_spec(d, 0, 1), grp.mod_spec(d, 0, 1)],
        out_specs=pl.BlockSpec((tm, d), lambda m: (m, 0)),
        compiler_params=_cparams(("arbitrary",)),
        name="modulate",
    )(x, sc, sh)


def _linear_kernel(x_ref, w_ref, *refs, scale, n_out):
    outs, wb = refs[:n_out], refs[n_out]

    @pl.when(pl.program_id(1) == 0)
    def _():
        wb[...] = w_ref[...].astype(BF)

    y = _dot(x_ref[...], wb[...])
    if scale != 1.0:
        y = y * scale
    for o in outs:
        o[...] = y.astype(o.dtype)


def _linear(grp, x, w3, layer, col0, n_cols, out_dtypes, scale=1.0):
    n, k = x.shape
    tm = grp.tm
    tn = _pick(n_cols, (512, 256, 128))
    assert col0 % tn == 0
    cb0 = col0 // tn
    outs = [jax.ShapeDtypeStruct((n, n_cols), dt) for dt in out_dtypes]
    return pl.pallas_call(
        functools.partial(_linear_kernel, scale=scale, n_out=len(outs)),
        out_shape=outs,
        grid=(n_cols // tn, grp.nm),
        in_specs=[pl.BlockSpec((tm, k), lambda c, m: (m, 0)),
                  pl.BlockSpec((None, k, tn), lambda c, m: (layer, 0, cb0 + c))],
        out_specs=[pl.BlockSpec((tm, tn), lambda c, m: (m, c)) for _ in outs],
        scratch_shapes=[pltpu.VMEM((k, tn), BF)],
        compiler_params=_cparams(("arbitrary", "arbitrary")),
        name="linear",
    )(x, w3)


def _log_sigmoid(z):
    return jnp.minimum(z, 0.0) - jnp.log1p(jnp.exp(-jnp.abs(z)))


def _forget_kernel(x_ref, w_ref, b_ref, o_ref):
    z = _dot(x_ref[...], w_ref[...].astype(BF)) + b_ref[...]
    o_ref[...] = _log_sigmoid(z)


def _forget(grp, x, w_f, b_f):
    n, k = x.shape
    h = w_f.shape[1]
    tm = grp.tm
    return pl.pallas_call(
        _forget_kernel,
        out_shape=jax.ShapeDtypeStruct((n, h), F32),
        grid=(grp.nm,),
        in_specs=[pl.BlockSpec((tm, k), lambda m: (m, 0)),
                  pl.BlockSpec((k, h), lambda m: (0, 0)),
                  pl.BlockSpec((1, h), lambda m: (0, 0))],
        out_specs=pl.BlockSpec((tm, h), lambda m: (m, 0)),
        compiler_params=_cparams(("arbitrary",)),
        name="forget_gate",
    )(x, w_f, b_f.reshape(1, h))


def _prefix_lanes(x):
    n = x.shape[-1]
    lane = lax.broadcasted_iota(jnp.int32, x.shape, x.ndim - 1)
    d = 1
    while d < n:
        x = x + jnp.where(lane >= d, pltpu.roll(x, d, x.ndim - 1), 0.0)
        d *= 2
    return x


def _cumsum_kernel(x_ref, o_ref):
    o_ref[...] = _prefix_lanes(x_ref[...])


def _cumsum_time(lf_t):
    b, h, t = lf_t.shape
    return pl.pallas_call(
        _cumsum_kernel,
        out_shape=jax.ShapeDtypeStruct((b, h, t), F32),
        grid=(b,),
        in_specs=[pl.BlockSpec((None, h, t), lambda i: (i, 0, 0))],
        out_specs=pl.BlockSpec((None, h, t), lambda i: (i, 0, 0)),
        compiler_params=_cparams(("arbitrary",)),
        name="forget_cumsum",
    )(lf_t)


def _attn_kernel(q_ref, k_ref, v_ref, f_ref, o_ref, m_sc, l_sc, acc_sc, *, tq):
    qi = pl.program_id(2)
    half = LANE_V7X // 2
    lane = lax.broadcasted_iota(jnp.int32, (tq, LANE_V7X), 1)
    lo = lane < half
    q = q_ref[...]
    zero = jnp.zeros_like(q)
    qs = (jnp.where(lo, q, zero), jnp.where(lo, zero, q))
    for h in range(2):
        m_sc[h] = jnp.full((tq, LANE_V7X), -jnp.inf, F32)
        l_sc[h] = jnp.zeros((tq, LANE_V7X), F32)
        acc_sc[h] = jnp.zeros((tq, LANE_V7X), F32)

    def block(k0, masked):
        kb = k_ref[pl.ds(k0, tq), :]
        vb = v_ref[pl.ds(k0, tq), :]
        for h in range(2):
            s = _dot_nt(qs[h], kb) - f_ref[h:h + 1, pl.ds(k0, tq)]
            if masked:
                row = lax.broadcasted_iota(jnp.int32, (tq, tq), 0)
                col = lax.broadcasted_iota(jnp.int32, (tq, tq), 1)
                s = jnp.where(col <= row, s, -jnp.inf)
            m_prev = m_sc[h]
            m_new = jnp.maximum(m_prev, jnp.max(s, axis=1, keepdims=True))
            alpha = jnp.exp(m_prev - m_new)
            p = jnp.exp(s - m_new[:, 0:1])
            l_sc[h] = alpha * l_sc[h] + jnp.sum(p, axis=1, keepdims=True)
            acc_sc[h] = alpha * acc_sc[h] + _dot(p.astype(BF), vb)
            m_sc[h] = m_new

    def body(i, carry):
        block(pl.multiple_of(i * tq, tq), False)
        return carry

    lax.fori_loop(0, qi, body, 0)
    block(pl.multiple_of(qi * tq, tq), True)
    o = jnp.where(lo, acc_sc[0] / l_sc[0], acc_sc[1] / l_sc[1])
    o_ref[...] = o.astype(o_ref.dtype)


def _attention_prompt(q, k, v, fcum, b, t):
    n, d = q.shape
    hp = d // LANE_V7X
    tq = _pick(t, (512, 256, 128))
    nq = t // tq
    return pl.pallas_call(
        functools.partial(_attn_kernel, tq=tq),
        out_shape=jax.ShapeDtypeStruct((n, d), BF),
        grid=(b, hp, nq),
        in_specs=[pl.BlockSpec((tq, LANE_V7X), lambda i, j, qi: (i * nq + qi, j)),
                  pl.BlockSpec((t, LANE_V7X), lambda i, j, qi: (i, j)),
                  pl.BlockSpec((t, LANE_V7X), lambda i, j, qi: (i, j)),
                  pl.BlockSpec((None, None, 2, t), lambda i, j, qi: (i, j, 0, 0))],
        out_specs=pl.BlockSpec((tq, LANE_V7X), lambda i, j, qi: (i * nq + qi, j)),
        scratch_shapes=[pltpu.VMEM((2, tq, LANE_V7X), F32)] * 3,
        compiler_params=_cparams(("arbitrary", "arbitrary", "arbitrary")),
        name="fox_attention_prompt",
    )(q, k, v, fcum)


def _split3(x):
    hi = x.astype(BF)
    r1 = x - hi.astype(F32)
    mid = r1.astype(BF)
    lo = (r1 - mid.astype(F32)).astype(BF)
    return hi, mid, lo


def _decode_kernel(pt_ref, q_ref, kn_ref, vn_ref, lfn_ref, *refs, pp, nh, nk):
    k_refs = refs[:pp]
    v_refs = refs[pp:2 * pp]
    lf_refs = refs[2 * pp:3 * pp]
    o_ref, qbd, m_sc, l_sc, acc_sc, c_sc = refs[3 * pp:]
    g = pl.program_id(1)
    dt, d = q_ref.shape
    hd = d // nh
    r = dt * nh
    own = (lax.rem(lax.broadcasted_iota(jnp.int32, (r, d), 0), nh)
           == lax.div(lax.broadcasted_iota(jnp.int32, (r, d), 1), hd))

    @pl.when(g == 0)
    def _():
        q = q_ref[...] * Q_SCALE
        qrep = jnp.concatenate([jnp.broadcast_to(q[t:t + 1], (nh, d)) for t in range(dt)], axis=0)
        qbd[...] = jnp.where(own, qrep, 0.0).astype(BF)
        m_sc[...] = jnp.full(m_sc.shape, -jnp.inf, F32)
        l_sc[...] = jnp.zeros(l_sc.shape, F32)
        acc_sc[...] = jnp.zeros(acc_sc.shape, F32)
        c_sc[...] = jnp.zeros(c_sc.shape, F32)

    def softmax_step(s):
        m_prev = m_sc[...]
        m_new = jnp.maximum(m_prev, jnp.max(s, axis=1, keepdims=True))
        alpha = jnp.exp(m_prev - m_new)
        p = jnp.exp(s - m_new[:, 0:1])
        l_sc[...] = alpha * l_sc[...] + jnp.sum(p, axis=1, keepdims=True)
        m_sc[...] = m_new
        return alpha[:, 0:1], p.astype(BF)

    def d_by_keys(ref):
        x = ref[...]
        return x.reshape(x.shape[0] * x.shape[1], x.shape[2]).astype(BF)

    qb = qbd[...]
    lf_all = jnp.concatenate([lf_refs[i][...] for i in range(pp)], axis=1)
    cum = _prefix_lanes(lf_all) + c_sc[:, 0:1]
    c_sc[...] = c_sc[...] + jnp.sum(lf_all, axis=1, keepdims=True)
    kt = jnp.concatenate([d_by_keys(k_refs[i]) for i in range(pp)], axis=1)
    vt = jnp.concatenate([d_by_keys(v_refs[i]) for i in range(pp)], axis=1)
    alpha, p = softmax_step(_dot(qb, kt) - jnp.concatenate([cum] * dt, axis=0))
    acc_sc[...] = alpha * acc_sc[...] + _dot_nt(p, vt)

    @pl.when(g == pl.num_programs(1) - 1)
    def _():
        cum_n = (_prefix_lanes(lfn_ref[...]) + c_sc[:, 0:1])[:, 0:nk]
        sn = _dot_nt(qb, kn_ref[...].astype(BF)) - jnp.concatenate([cum_n] * dt, axis=0)
        trow = lax.div(lax.broadcasted_iota(jnp.int32, (r, nk), 0), nh)
        col = lax.broadcasted_iota(jnp.int32, (r, nk), 1)
        alpha_n, pn = softmax_step(jnp.where(col <= trow, sn, -jnp.inf))
        acc = alpha_n * acc_sc[...] + _dot(pn, vn_ref[...].astype(BF))
        o = jnp.where(own, acc / l_sc[:, 0:1], 0.0)
        o_ref[...] = jnp.sum(o.reshape(dt, nh, d), axis=1)


def _attention_sample(layer, page_table, q_b, k_new, v_new, lf_new_t, cache_kt, cache_vt, cache_lft):
    db, dt, d = q_b.shape
    nk = k_new.shape[1]
    nh, hd, page = cache_kt.shape[2:]
    n_pages = page_table.shape[1]
    pp = _pick(n_pages, (PAGES_PER_STEP, 4, 2, 1))
    r = dt * nh

    def page_spec(shape, i):
        zeros = (0,) * len(shape)
        return pl.BlockSpec((None, None) + shape, lambda b, g, pt: (layer, pt[b, g * pp + i]) + zeros)

    in_specs = [pl.BlockSpec((None, dt, d), lambda b, g, pt: (b, 0, 0)),
                pl.BlockSpec((None, nk, d), lambda b, g, pt: (b, 0, 0)),
                pl.BlockSpec((None, nk, d), lambda b, g, pt: (b, 0, 0)),
                pl.BlockSpec((None, nh, LANE_V7X), lambda b, g, pt: (b, 0, 0))]
    in_specs += [page_spec((nh, hd, page), i) for i in range(pp)]
    in_specs += [page_spec((nh, hd, page), i) for i in range(pp)]
    in_specs += [page_spec((nh, page), i) for i in range(pp)]
    grid_spec = pltpu.PrefetchScalarGridSpec(
        num_scalar_prefetch=1,
        grid=(db, n_pages // pp),
        in_specs=in_specs,
        out_specs=pl.BlockSpec((None, dt, d), lambda b, g, pt: (b, 0, 0)),
        scratch_shapes=[pltpu.VMEM((r, d), BF),
                        pltpu.VMEM((r, LANE_V7X), F32),
                        pltpu.VMEM((r, LANE_V7X), F32),
                        pltpu.VMEM((r, d), F32),
                        pltpu.VMEM((nh, LANE_V7X), F32)])
    return pl.pallas_call(
        functools.partial(_decode_kernel, pp=pp, nh=nh, nk=nk),
        out_shape=jax.ShapeDtypeStruct((db, dt, d), F32),
        grid_spec=grid_spec,
        compiler_params=_cparams(("arbitrary", "arbitrary")),
        name="fox_attention_sample",
    )(page_table, q_b, k_new, v_new, lf_new_t,
      *([cache_kt] * pp), *([cache_vt] * pp), *([cache_lft] * pp))


def _conv_tile(e_ref, carry_ref, init_ref, tail_ref, u, w_ref, first, *, hb, stride, width):
    tm = u.shape[0]

    @pl.when(first)
    def _():
        e_ref[0:hb, :] = init_ref[...]

    @pl.when(jnp.logical_not(first))
    def _():
        e_ref[0:hb, :] = carry_ref[...]

    e_ref[hb:hb + tm, :] = u
    y = None
    for j in range(width):
        off = hb - (width - 1 - j) * stride
        term = w_ref[j:j + 1, :] * e_ref[off:off + tm, :]
        y = term if y is None else y + term
    tail = e_ref[tm:tm + hb, :]
    carry_ref[...] = tail
    tail_ref[...] = tail
    return y


def _sc_kernel(x_ref, wb_ref, wc_ref, wv_ref, wconv_ref, init_ref, mix_ref, tail_ref,
               wb, e_sc, carry, *, tps, hb, stride, width):
    m = pl.program_id(1)

    @pl.when(m == 0)
    def _():
        wb[0] = wb_ref[...].astype(BF)
        wb[1] = wc_ref[...].astype(BF)
        wb[2] = wv_ref[...].astype(BF)

    x = x_ref[...]
    g_b = _dot(x, wb[0])
    u = _dot(x, wb[1]) * _dot(x, wb[2])
    y = _conv_tile(e_sc, carry, init_ref, tail_ref, u, wconv_ref, (m % tps) == 0,
                   hb=hb, stride=stride, width=width)
    mix_ref[...] = (g_b * y).astype(mix_ref.dtype)


def _short_conv(grp, x, w_in, layer, w_conv, init):
    n, d = x.shape
    tm = grp.tm
    width = w_conv.shape[0]
    hb = init.shape[1]
    tn = _pick(d, (256, 128))
    nb = d // tn
    tps = grp.tps
    n_init = init.shape[0]

    def wspec(part):
        return pl.BlockSpec((None, d, tn), lambda c, m: (layer, 0, part * nb + c))

    def seq_of(m):
        return m // tps

    return pl.pallas_call(
        functools.partial(_sc_kernel, tps=tps, hb=hb, stride=grp.stride, width=width),
        out_shape=[jax.ShapeDtypeStruct((n, d), BF),
                   jax.ShapeDtypeStruct((grp.n_seq, hb, d), F32)],
        grid=(nb, grp.nm),
        in_specs=[pl.BlockSpec((tm, d), lambda c, m: (m, 0)),
                  wspec(0), wspec(1), wspec(2),
                  pl.BlockSpec((width, tn), lambda c, m: (0, c)),
                  pl.BlockSpec((None, hb, tn), lambda c, m: (seq_of(m) % n_init, 0, c))],
        out_specs=[pl.BlockSpec((tm, tn), lambda c, m: (m, c)),
                   pl.BlockSpec((None, hb, tn), lambda c, m: (seq_of(m), 0, c))],
        scratch_shapes=[pltpu.VMEM((3, d, tn), BF),
                        pltpu.VMEM((hb + tm, tn), F32),
                        pltpu.VMEM((hb, tn), F32)],
        compiler_params=_cparams(("arbitrary", "arbitrary")),
        name="short_conv_mixer",
    )(x, w_in, w_in, w_in, w_conv, init)


def _softplus(x):
    return jnp.maximum(x, 0.0) + jnp.log1p(jnp.exp(-jnp.abs(x)))


def _neg_expm1(x):
    th = jnp.tanh(0.5 * x)
    return -2.0 * th / (1.0 - th)


def _scan_rows(a, u):
    tm = a.shape[0]
    row = lax.broadcasted_iota(jnp.int32, a.shape, 0)
    d = 1
    while d < tm:
        keep = row >= d
        a_s = jnp.where(keep, pltpu.roll(a, d, 0), 1.0)
        u_s = jnp.where(keep, pltpu.roll(u, d, 0), 0.0)
        u = u + a * u_s
        a = a * a_s
        d *= 2
    return a, u


def _lru_kernel(x_ref, wg_ref, wr_ref, cw_ref, cb_ref, bda_ref, bdx_ref, ba_ref, bx_ref, lam_ref,
                init_ref, h0_ref, mix_ref, tail_ref, hlast_ref,
                wb, bdb, e_sc, carry, hcarry, *, tps, hb, stride, width):
    m = pl.program_id(1)
    tm = x_ref.shape[0]

    @pl.when(m == 0)
    def _():
        wb[0] = wg_ref[...].astype(BF)
        wb[1] = wr_ref[...].astype(BF)
        bdb[0] = bda_ref[...].astype(BF)
        bdb[1] = bdx_ref[...].astype(BF)

    first = (m % tps) == 0
    x = x_ref[...]
    gate = jax.nn.gelu(_dot(x, wb[0]), approximate=True)
    xr = _dot(x, wb[1])
    xc = _conv_tile(e_sc, carry, init_ref, tail_ref, xr, cw_ref, first,
                    hb=hb, stride=stride, width=width) + cb_ref[...]
    xcb = xc.astype(BF)
    r = jax.nn.sigmoid(_dot(xcb, bdb[0]) + ba_ref[...])
    i = jax.nn.sigmoid(_dot(xcb, bdb[1]) + bx_ref[...])
    log_a = (-LRU_C) * r * _softplus(-lam_ref[...])
    a = jnp.exp(log_a)
    u = jnp.sqrt(_neg_expm1(2.0 * log_a)) * i * xc

    @pl.when(first)
    def _():
        hcarry[...] = h0_ref[...]

    if stride == 1:
        a_cum, h_loc = _scan_rows(a, u)
        hs = h_loc + a_cum * hcarry[0:1, :]
        mix_ref[...] = (gate * hs).astype(mix_ref.dtype)
        h_end = jnp.broadcast_to(hs[tm - 1:tm, :], hcarry.shape)
    else:
        h = hcarry[...]
        for t in range(tm // stride):
            sl = slice(t * stride, (t + 1) * stride)
            h = a[sl] * h + u[sl]
            mix_ref[sl, :] = (gate[sl] * h).astype(mix_ref.dtype)
        h_end = h
    hcarry[...] = h_end
    hlast_ref[...] = h_end


def _block_diag(w, per):
    nblk, bd, _ = w.shape
    eye = jnp.eye(per, dtype=w.dtype)
    return jnp.einsum('cide,ik->cidke', w.reshape(nblk // per, per, bd, bd), eye).reshape(
        nblk // per, per * bd, per * bd)


def _rglru(grp, x, w_in, layer, conv_w, conv_b, w_a, b_a, w_x, b_x, lam, init, h0):
    n, d = x.shape
    rw = conv_w.shape[1]
    tm = grp.tm
    width = conv_w.shape[0]
    hb = init.shape[1]
    hr = h0.shape[1]
    bd = w_a.shape[1]
    tn = _pick(rw, (256, 128))
    nb = rw // tn
    tps = grp.tps
    n_init = init.shape[0]
    bda = _block_diag(w_a, tn // bd)
    bdx = _block_diag(w_x, tn // bd)

    def row_spec():
        return pl.BlockSpec((1, tn), lambda c, m: (0, c))

    def seq_of(m):
        return m // tps

    return pl.pallas_call(
        functools.partial(_lru_kernel, tps=tps, hb=hb, stride=grp.stride, width=width),
        out_shape=[jax.ShapeDtypeStruct((n, rw), BF),
                   jax.ShapeDtypeStruct((grp.n_seq, hb, rw), F32),
                   jax.ShapeDtypeStruct((grp.n_seq, hr, rw), F32)],
        grid=(nb, grp.nm),
        in_specs=[pl.BlockSpec((tm, d), lambda c, m: (m, 0)),
                  pl.BlockSpec((None, d, tn), lambda c, m: (layer, 0, c)),
                  pl.BlockSpec((None, d, tn), lambda c, m: (layer, 0, nb + c)),
                  pl.BlockSpec((width, tn), lambda c, m: (0, c)),
                  row_spec(),
                  pl.BlockSpec((None, tn, tn), lambda c, m: (c, 0, 0)),
                  pl.BlockSpec((None, tn, tn), lambda c, m: (c, 0, 0)),
                  row_spec(), row_spec(), row_spec(),
                  pl.BlockSpec((None, hb, tn), lambda c, m: (seq_of(m) % n_init, 0, c)),
                  pl.BlockSpec((None, hr, tn), lambda c, m: (seq_of(m) % n_init, 0, c))],
        out_specs=[pl.BlockSpec((tm, tn), lambda c, m: (m, c)),
                   pl.BlockSpec((None, hb, tn), lambda c, m: (seq_of(m), 0, c)),
                   pl.BlockSpec((None, hr, tn), lambda c, m: (seq_of(m), 0, c))],
        scratch_shapes=[pltpu.VMEM((2, d, tn), BF),
                        pltpu.VMEM((2, tn, tn), BF),
                        pltpu.VMEM((hb + tm, tn), F32),
                        pltpu.VMEM((hb, tn), F32),
                        pltpu.VMEM((hr, tn), F32)],
        compiler_params=_cparams(("arbitrary", "arbitrary")),
        name="rglru_mixer",
    )(x, w_in, w_in, conv_w, conv_b.reshape(1, rw), bda, bdx, b_a.reshape(1, rw),
      b_x.reshape(1, rw), lam.reshape(1, rw), init, h0)


def _post_ln(x, f, gate, ln_g, ln_b, alpha):
    z = alpha * x + (1.0 + gate) * f
    mu = jnp.mean(z, axis=-1, keepdims=True)
    zc = z - mu
    var = jnp.mean(zc * zc, axis=-1, keepdims=True)
    return zc * lax.rsqrt(var + LN_EPS) * ln_g + ln_b


def _top2_gates_t(logits_t):
    ne = float(logits_t.shape[0])
    idx = lax.broadcasted_iota(jnp.int32, logits_t.shape, 0).astype(F32)
    m1 = jnp.max(logits_t, axis=0, keepdims=True)
    i1 = jnp.min(jnp.where(logits_t == m1, idx, ne), axis=0, keepdims=True)
    sel1 = idx == i1
    rest = jnp.where(sel1, -jnp.inf, logits_t)
    m2 = jnp.max(rest, axis=0, keepdims=True)
    i2 = jnp.min(jnp.where(rest == m2, idx, ne), axis=0, keepdims=True)
    sel2 = idx == i2
    e2 = jnp.exp(m2 - m1)
    w1 = 1.0 / (1.0 + e2)
    w2 = e2 / (1.0 + e2)
    return jnp.where(sel1, w1, 0.0) + jnp.where(sel2, w2, 0.0)


def _out_ln_kernel(a_ref, w_ref, x_ref, g_ref, sc_ref, sh_ref, lng_ref, lnb_ref, *refs,
                   alpha, has_router):
    if has_router:
        wr_ref, x1_ref, hf_ref, gates_ref, wb = refs
    else:
        x1_ref, hf_ref, wb = refs

    @pl.when(pl.program_id(0) == 0)
    def _():
        wb[...] = w_ref[...].astype(BF)

    y = _dot(a_ref[...], wb[...])
    x1 = _post_ln(x_ref[...], y, g_ref[...], lng_ref[...], lnb_ref[...], alpha)
    x1_ref[...] = x1
    hf = x1 * (1.0 + sc_ref[...]) + sh_ref[...]
    hf_ref[...] = hf.astype(hf_ref.dtype)
    if has_router:
        wh, wl = _split(wr_ref[...])
        hh, hl = _split(hf)
        logits_t = _dot_nt(wh, hh) + (_dot_nt(wh, hl) + _dot_nt(wl, hh))
        gates_ref[...] = _top2_gates_t(logits_t)


def _out_ln(grp, a, w3, layer, x, gate, sc, sh, ln_g, ln_b, alpha, w_router=None):
    n, d = x.shape
    k = a.shape[1]
    tm = grp.tm
    has_router = w_router is not None
    in_specs = [pl.BlockSpec((tm, k), lambda m: (m, 0)),
                pl.BlockSpec((None, k, d), lambda m: (layer, 0, 0)),
                pl.BlockSpec((tm, d), lambda m: (m, 0)),
                grp.mod_spec(d, 0, 1), grp.mod_spec(d, 0, 1), grp.mod_spec(d, 0, 1),
                pl.BlockSpec((1, d), lambda m: (0, 0)),
                pl.BlockSpec((1, d), lambda m: (0, 0))]
    args = [a, w3, x, gate, sc, sh, ln_g.reshape(1, d), ln_b.reshape(1, d)]
    out_shape = [jax.ShapeDtypeStruct((n, d), F32), jax.ShapeDtypeStruct((n, d), BF)]
    out_specs = [pl.BlockSpec((tm, d), lambda m: (m, 0)), pl.BlockSpec((tm, d), lambda m: (m, 0))]
    if has_router:
        ne = w_router.shape[1]
        in_specs.append(pl.BlockSpec((ne, d), lambda m: (0, 0)))
        args.append(w_router.T)
        out_shape.append(jax.ShapeDtypeStruct((ne, n), F32))
        out_specs.append(pl.BlockSpec((ne, tm), lambda m: (0, m)))
    return pl.pallas_call(
        functools.partial(_out_ln_kernel, alpha=alpha, has_router=has_router),
        out_shape=out_shape,
        grid=(grp.nm,),
        in_specs=in_specs,
        out_specs=out_specs,
        scratch_shapes=[pltpu.VMEM((k, d), BF)],
        compiler_params=_cparams(("arbitrary",)),
        name="out_proj_ln",
    )(*args)


def _swiglu_part(h, wg, wu, wd):
    gu = _dot(h, wg.astype(BF))
    up = _dot(h, wu.astype(BF))
    act = (gu * jax.nn.sigmoid(gu) * up).astype(BF)
    return _dot(act, wd.astype(BF))


def _ffn_kernel(h_ref, wg_ref, wu_ref, wd_ref, x_ref, g_ref, sc_ref, sh_ref, lng_ref, lnb_ref, *refs,
                alpha, has_next):
    if has_next:
        x2_ref, hn_ref, acc = refs
    else:
        x2_ref, acc = refs
    ch = pl.program_id(1)
    part = _swiglu_part(h_ref[...], wg_ref[...], wu_ref[...], wd_ref[...])

    @pl.when(ch == 0)
    def _():
        acc[...] = part

    @pl.when(ch > 0)
    def _():
        acc[...] += part

    @pl.when(ch == pl.num_programs(1) - 1)
    def _():
        x2 = _post_ln(x_ref[...], acc[...], g_ref[...], lng_ref[...], lnb_ref[...], alpha)
        x2_ref[...] = x2
        if has_next:
            hn_ref[...] = (x2 * (1.0 + sc_ref[...]) + sh_ref[...]).astype(hn_ref.dtype)


def _ffn(grp, h, w_gu, w_down, layer, x, gate, sc, sh, ln_g, ln_b, alpha, has_next=True):
    n, d = x.shape
    tm = grp.tm
    ff = w_down.shape[1]
    tf = _pick(ff, (256, 128))
    nch = ff // tf
    row = pl.BlockSpec((tm, d), lambda m, c: (m, 0))
    vec = pl.BlockSpec((1, d), lambda m, c: (0, 0))
    in_specs = [row,
                pl.BlockSpec((None, d, tf), lambda m, c: (layer, 0, c)),
                pl.BlockSpec((None, d, tf), lambda m, c: (layer, 0, nch + c)),
                pl.BlockSpec((None, tf, d), lambda m, c: (layer, c, 0)),
                row, grp.mod_spec(d, 0, 2), grp.mod_spec(d, 0, 2), grp.mod_spec(d, 0, 2), vec, vec]
    out_shape = [jax.ShapeDtypeStruct((n, d), F32)]
    out_specs = [row]
    if has_next:
        out_shape.append(jax.ShapeDtypeStruct((n, d), BF))
        out_specs.append(row)
    res = pl.pallas_call(
        functools.partial(_ffn_kernel, alpha=alpha, has_next=has_next),
        out_shape=out_shape,
        grid=(grp.nm, nch),
        in_specs=in_specs,
        out_specs=out_specs,
        scratch_shapes=[pltpu.VMEM((tm, d), F32)],
        compiler_params=_cparams(("arbitrary", "arbitrary")),
        name="ffn_dense",
    )(h, w_gu, w_gu, w_down, x, gate, sc, sh, ln_g.reshape(1, d), ln_b.reshape(1, d))
    return (res[0], res[1]) if has_next else (res[0], None)


MOE_TT = 512
MOE_RC = 128
MOE_TMG = 512
MOE_ALIGN = 16


def _seg_sizes(tt):
    sizes = []
    sz = tt
    while sz >= MOE_ALIGN:
        sizes.append(sz)
        sz //= 2
    return tuple(sizes)


def _segments(c16, sizes):
    off = jnp.int32(0)
    for j, sz in enumerate(sizes):
        cond = (c16 & sz) != 0
        yield cond, off, sz, j
        off = off + jnp.where(cond, sz, 0)


def _route_count_kernel(g_ref, o_ref):
    routed = jnp.where(g_ref[...] > 0.0, 1.0, 0.0)
    o_ref[...] = jnp.broadcast_to(jnp.sum(routed, axis=1, keepdims=True), o_ref.shape)


def _route_counts(gates_t, tt):
    ne, n = gates_t.shape
    nt = n // tt
    return pl.pallas_call(
        _route_count_kernel,
        out_shape=jax.ShapeDtypeStruct((nt, ne, LANE_V7X), F32),
        grid=(nt,),
        in_specs=[pl.BlockSpec((ne, tt), lambda i: (0, i))],
        out_specs=pl.BlockSpec((None, ne, LANE_V7X), lambda i: (i, 0, 0)),
        compiler_params=_cparams(("arbitrary",)),
        name="moe_route_counts",
    )(gates_t)


def _dispatch_kernel(base_ref, c16_ref, g_ref, h_ref, sorted_in_ref, sorted_ref, stage, sems,
                     *, ne, tt, rc, sizes):
    del sorted_in_ref
    i = pl.program_id(0)
    gt = g_ref[...]
    routed = gt > 0.0
    ri = lax.broadcasted_iota(jnp.int32, (tt, tt), 0)
    ci = lax.broadcasted_iota(jnp.int32, (tt, tt), 1)
    before = jnp.where(ri < ci, 1.0, 0.0).astype(BF)
    rank = _dot(jnp.where(routed, 1.0, 0.0).astype(BF), before)
    rank = jnp.where(routed, rank, -1.0)
    h = h_ref[...]

    def seg_copy(e, b0, off, sz, j):
        return pltpu.make_async_copy(
            stage.at[e, pl.ds(pl.multiple_of(off, MOE_ALIGN), sz)],
            sorted_ref.at[pl.ds(pl.multiple_of(b0 + off, MOE_ALIGN), sz)],
            sems.at[e, j])

    for e in range(ne):
        c16 = c16_ref[i * ne + e]
        b0 = base_ref[i * ne + e]
        for c in range(tt // rc):
            @pl.when(c * rc < c16)
            def _():
                slot = (lax.broadcasted_iota(jnp.int32, (rc, tt), 0) + c * rc).astype(F32)
                sel = slot == rank[e:e + 1, :]
                stage[e, c * rc:(c + 1) * rc, :] = _dot(jnp.where(sel, 1.0, 0.0).astype(BF), h).astype(BF)
        for cond, off, sz, j in _segments(c16, sizes):
            @pl.when(cond)
            def _():
                seg_copy(e, b0, off, sz, j).start()

    for e in range(ne):
        c16 = c16_ref[i * ne + e]
        b0 = base_ref[i * ne + e]
        for cond, off, sz, j in _segments(c16, sizes):
            @pl.when(cond)
            def _():
                seg_copy(e, b0, off, sz, j).wait()


def _dispatch(base, c16, gates_t, h_all, s_max, tt):
    ne, n = gates_t.shape
    d = h_all.shape[1]
    sizes = _seg_sizes(tt)
    grid_spec = pltpu.PrefetchScalarGridSpec(
        num_scalar_prefetch=2,
        grid=(n // tt,),
        in_specs=[pl.BlockSpec((ne, tt), lambda i, b, c: (0, i)),
                  pl.BlockSpec((tt, d), lambda i, b, c: (i, 0)),
                  pl.BlockSpec(memory_space=pl.ANY)],
        out_specs=pl.BlockSpec(memory_space=pl.ANY),
        scratch_shapes=[pltpu.VMEM((ne, tt, d), BF),
                        pltpu.SemaphoreType.DMA((ne, len(sizes)))])
    return pl.pallas_call(
        functools.partial(_dispatch_kernel, ne=ne, tt=tt, rc=min(MOE_RC, tt), sizes=sizes),
        out_shape=jax.ShapeDtypeStruct((s_max, d), BF),
        grid_spec=grid_spec,
        input_output_aliases={4: 0},
        compiler_params=_cparams(("arbitrary",)),
        name="moe_dispatch",
    )(base, c16, gates_t, h_all, jnp.zeros((s_max, d), BF))


def _grouped_kernel(te_ref, nu_ref, x_ref, wg_ref, wu_ref, wd_ref, y_ref, acc):
    del te_ref
    g = pl.program_id(0)
    ch = pl.program_id(1)
    last = pl.num_programs(1) - 1

    @pl.when(g < nu_ref[0])
    def _():
        part = _swiglu_part(x_ref[...], wg_ref[...], wu_ref[...], wd_ref[...])

        @pl.when(ch == 0)
        def _():
            acc[...] = part

        @pl.when(ch > 0)
        def _():
            acc[...] += part

        @pl.when(ch == last)
        def _():
            y_ref[...] = acc[...].astype(y_ref.dtype)

    @pl.when(jnp.logical_and(g >= nu_ref[0], ch == last))
    def _():
        y_ref[...] = jnp.zeros(y_ref.shape, y_ref.dtype)


def _grouped_ffn(tile_expert, n_used, xs, w_gu, w_down, layer):
    s_max, d = xs.shape
    ff = w_down.shape[2]
    tmg = MOE_TMG
    tf = _pick(ff, (512, 256, 128))
    nch = ff // tf

    def row(g, c, te, nu):
        return (jnp.minimum(g, nu[0] - 1), 0)

    def chunk(g, c, nu):
        return jnp.where(g < nu[0], c, nch - 1)

    grid_spec = pltpu.PrefetchScalarGridSpec(
        num_scalar_prefetch=2,
        grid=(s_max // tmg, nch),
        in_specs=[pl.BlockSpec((tmg, d), row),
                  pl.BlockSpec((None, None, d, tf), lambda g, c, te, nu: (layer, te[g], 0, chunk(g, c, nu))),
                  pl.BlockSpec((None, None, d, tf), lambda g, c, te, nu: (layer, te[g], 0, nch + chunk(g, c, nu))),
                  pl.BlockSpec((None, None, tf, d), lambda g, c, te, nu: (layer, te[g], chunk(g, c, nu), 0))],
        out_specs=pl.BlockSpec((tmg, d), lambda g, c, te, nu: (g, 0)),
        scratch_shapes=[pltpu.VMEM((tmg, d), F32)])
    return pl.pallas_call(
        _grouped_kernel,
        out_shape=jax.ShapeDtypeStruct((s_max, d), BF),
        grid_spec=grid_spec,
        compiler_params=_cparams(("arbitrary", "arbitrary")),
        name="moe_grouped_ffn",
    )(tile_expert, n_used, xs, w_gu, w_gu, w_down)


def _combine_kernel(base_ref, c16_ref, g_ref, y_ref, f_ref, ystage, acc, sems, *, ne, tt, rc, sizes):
    i = pl.program_id(0)

    @pl.when(i == 0)
    def _():
        ystage[...] = jnp.zeros(ystage.shape, ystage.dtype)

    def seg_copy(e, b0, off, sz, j):
        return pltpu.make_async_copy(
            y_ref.at[pl.ds(pl.multiple_of(b0 + off, MOE_ALIGN), sz)],
            ystage.at[e, pl.ds(pl.multiple_of(off, MOE_ALIGN), sz)],
            sems.at[e, j])

    for e in range(ne):
        c16 = c16_ref[i * ne + e]
        b0 = base_ref[i * ne + e]
        for cond, off, sz, j in _segments(c16, sizes):
            @pl.when(cond)
            def _():
                seg_copy(e, b0, off, sz, j).start()

    ri = lax.broadcasted_iota(jnp.int32, (tt, tt), 0)
    ci = lax.broadcasted_iota(jnp.int32, (tt, tt), 1)
    eye = jnp.where(ri == ci, 1.0, 0.0).astype(BF)
    p1, p2, p3 = _split3(g_ref[...])
    gtok = _dot_nt(eye, p1) + _dot_nt(eye, p2) + _dot_nt(eye, p3)
    routed = gtok > 0.0
    before = jnp.where(ci < ri, 1.0, 0.0).astype(BF)
    rank = _dot(before, jnp.where(routed, 1.0, 0.0).astype(BF))
    rank = jnp.where(routed, rank, -1.0)
    acc[...] = jnp.zeros(acc.shape, F32)

    for e in range(ne):
        c16 = c16_ref[i * ne + e]
        b0 = base_ref[i * ne + e]
        for cond, off, sz, j in _segments(c16, sizes):
            @pl.when(cond)
            def _():
                seg_copy(e, b0, off, sz, j).wait()
        for c in range(tt // rc):
            @pl.when(c * rc < c16)
            def _():
                slot = (lax.broadcasted_iota(jnp.int32, (tt, rc), 1) + c * rc).astype(F32)
                sel = slot == rank[:, e:e + 1]
                rows = _dot(jnp.where(sel, 1.0, 0.0).astype(BF), ystage[e, c * rc:(c + 1) * rc, :])
                acc[...] += gtok[:, e:e + 1] * rows

    f_ref[...] = acc[...]


def _combine(base, c16, gates_t, y, tt):
    ne, n = gates_t.shape
    d = y.shape[1]
    sizes = _seg_sizes(tt)
    grid_spec = pltpu.PrefetchScalarGridSpec(
        num_scalar_prefetch=2,
        grid=(n // tt,),
        in_specs=[pl.BlockSpec((ne, tt), lambda i, b, c: (0, i)),
                  pl.BlockSpec(memory_space=pl.ANY)],
        out_specs=pl.BlockSpec((tt, d), lambda i, b, c: (i, 0)),
        scratch_shapes=[pltpu.VMEM((ne, tt, d), BF),
                        pltpu.VMEM((tt, d), F32),
                        pltpu.SemaphoreType.DMA((ne, len(sizes)))])
    return pl.pallas_call(
        functools.partial(_combine_kernel, ne=ne, tt=tt, rc=min(MOE_RC, tt), sizes=sizes),
        out_shape=jax.ShapeDtypeStruct((n, d), F32),
        grid_spec=grid_spec,
        compiler_params=_cparams(("arbitrary",)),
        name="moe_combine",
    )(base, c16, gates_t, y)


def _moe(gates_t, h_all, w_gu, w_down, layer):
    ne, n = gates_t.shape
    tt, tmg = MOE_TT, MOE_TMG
    nt = n // tt
    cnt = _route_counts(gates_t, tt)[:, :, 0].astype(jnp.int32)
    c16 = (cnt + (MOE_ALIGN - 1)) // MOE_ALIGN * MOE_ALIGN
    region = (jnp.sum(c16, axis=0) + (tmg - 1)) // tmg * tmg
    reg_end = jnp.cumsum(region)
    base = (reg_end - region)[None, :] + jnp.cumsum(c16, axis=0) - c16
    s_max = -(-(TOP_K * n + (MOE_ALIGN - 1) * nt * ne + tmg * ne) // tmg) * tmg
    n_used = (reg_end[-1:] // tmg).astype(jnp.int32)
    tile_start = jnp.arange(s_max // tmg, dtype=jnp.int32) * tmg
    tile_expert = jnp.minimum(jnp.sum(tile_start[:, None] >= reg_end[None, :], axis=1), ne - 1).astype(jnp.int32)
    base = base.reshape(-1).astype(jnp.int32)
    c16 = c16.reshape(-1).astype(jnp.int32)
    xs = _dispatch(base, c16, gates_t, h_all, s_max, tt)
    y = _grouped_ffn(tile_expert, n_used, xs, w_gu, w_down, layer)
    return _combine(base, c16, gates_t, y, tt)


def _ln_res_kernel(f_ref, x_ref, g_ref, sc_ref, sh_ref, lng_ref, lnb_ref, *refs, alpha, has_next):
    x2 = _post_ln(x_ref[...], f_ref[...], g_ref[...], lng_ref[...], lnb_ref[...], alpha)
    refs[0][...] = x2
    if has_next:
        refs[1][...] = (x2 * (1.0 + sc_ref[...]) + sh_ref[...]).astype(refs[1].dtype)


def _ln_residual(grp, f_all, row0, x, gate, sc, sh, ln_g, ln_b, alpha, has_next=True):
    n, d = x.shape
    tm = grp.tm
    assert row0 % tm == 0
    blk0 = row0 // tm
    row = pl.BlockSpec((tm, d), lambda m: (m, 0))
    vec = pl.BlockSpec((1, d), lambda m: (0, 0))
    out_shape = [jax.ShapeDtypeStruct((n, d), F32)]
    out_specs = [row]
    if has_next:
        out_shape.append(jax.ShapeDtypeStruct((n, d), BF))
        out_specs.append(row)
    res = pl.pallas_call(
        functools.partial(_ln_res_kernel, alpha=alpha, has_next=has_next),
        out_shape=out_shape,
        grid=(grp.nm,),
        in_specs=[pl.BlockSpec((tm, d), lambda m: (blk0 + m, 0)), row,
                  grp.mod_spec(d, 0, 1), grp.mod_spec(d, 0, 1), grp.mod_spec(d, 0, 1), vec, vec],
        out_specs=out_specs,
        compiler_params=_cparams(("arbitrary",)),
        name="moe_ln_residual",
    )(f_all, x, gate, sc, sh, ln_g.reshape(1, d), ln_b.reshape(1, d))
    return (res[0], res[1]) if has_next else (res[0], None)


def _to_time_major(a):
    a = jnp.swapaxes(a, 0, 1)
    return a.reshape((a.shape[0] * a.shape[1],) + a.shape[2:])


def _from_time_major(a, db):
    a = a.reshape((a.shape[0] // db, db) + a.shape[1:])
    return jnp.swapaxes(a, 0, 1)


def kernel(x_prompt, x_sample, cache_k, cache_v, cache_lf, state_conv_b, state_conv_c, state_h, page_table, c_prompt, c_sample, w_ada, b_ada, ln_g, ln_b, fox_w_in, fox_b_f, fox_w_out, sc_w_in, sc_w_conv, sc_w_out, lru_w_in, lru_conv_w, lru_conv_b, lru_w_a, lru_b_a, lru_w_x, lru_b_x, lru_lam, lru_w_out, ffn_w_gu, ffn_w_down, moe_w_router, moe_w_gu, moe_w_down):
    b, t, d = x_prompt.shape
    db, dt, _ = x_sample.shape
    depth = w_ada.shape[0]
    nh = fox_b_f.shape[1]
    hd = d // nh
    assert hd == FOX_HEAD_DIM and d % LANE_V7X == 0 and db % SUBLANE_V7X == 0
    alpha = (2.0 * depth) ** 0.25
    n_fox = cache_k.shape[0]
    n_pool, page = cache_k.shape[1], cache_k.shape[2]
    ns = db * dt

    gp = _Group(b * t, _pick(t, (1024, 512, 256, 128)), t, 1, b)
    gs = _Group(ns, ns, ns, db, 1)

    mods = _ada(jnp.concatenate([c_prompt, c_sample], axis=0), w_ada, b_ada)
    mods_p = mods[:, :b].reshape(depth, b, 6, d)
    mods_s = jnp.tile(mods[:, b:].reshape(depth, db, 6, d), (1, dt, 1, 1))

    def mod(layer, which, comp):
        if which == 0:
            return mods_p[layer, :, comp][:, None, :]
        return mods_s[layer, :, comp][None]

    cache_kt = jnp.transpose(cache_k, (0, 1, 3, 4, 2))
    cache_vt = jnp.transpose(cache_v, (0, 1, 3, 4, 2))
    cache_lft = jnp.transpose(cache_lf, (0, 1, 3, 2))

    groups = (gp, gs)
    xs = [x_prompt.reshape(b * t, d), _to_time_major(x_sample)]
    hm = [_modulate(groups[w], xs[w], mod(0, w, 1), mod(0, w, 0)) for w in range(2)]
    outs = [dict(k=[], v=[], lf=[], cb=[], cc=[], h=[]) for _ in range(2)]
    x1, hf, gates_t = [None, None], [None, None], [None, None]

    for i in range(depth):
        kind, j = i % N_MIXERS, i // N_MIXERS
        last = i == depth - 1
        is_moe = i % 2 == 1
        for w in range(2):
            grp = groups[w]
            o = outs[w]
            if kind == 0:
                (qb,) = _linear(grp, hm[w], fox_w_in, j, 0, d, [F32 if w else BF], scale=1.0 if w else Q_SCALE)
                k32, kb = _linear(grp, hm[w], fox_w_in, j, d, d, [F32, BF])
                v32, vb = _linear(grp, hm[w], fox_w_in, j, 2 * d, d, [F32, BF])
                lf = _forget(grp, hm[w], fox_w_in[j][:, 3 * d:], fox_b_f[j])
                if w == 0:
                    fcum = _cumsum_time(jnp.swapaxes(lf.reshape(b, t, nh), 1, 2))
                    a_mix = _attention_prompt(qb, kb, vb, fcum.reshape(b, nh // 2, 2, t), b, t)
                    o['k'].append(k32.reshape(b, t, nh, hd))
                    o['v'].append(v32.reshape(b, t, nh, hd))
                    o['lf'].append(lf.reshape(b, t, nh))
                else:
                    nk = 2 * SUBLANE_V7X
                    assert dt <= nk
                    k_b = _from_time_major(k32, db)
                    v_b = _from_time_major(v32, db)
                    lf_b = _from_time_major(lf, db)
                    pad = ((0, 0), (0, nk - dt), (0, 0))
                    lf_new_t = jnp.pad(jnp.swapaxes(lf_b, 1, 2), ((0, 0), (0, 0), (0, LANE_V7X - dt)))
                    att = _attention_sample(j, page_table, _from_time_major(qb, db), jnp.pad(k_b, pad),
                                            jnp.pad(v_b, pad), lf_new_t, cache_kt, cache_vt, cache_lft)
                    a_mix = _to_time_major(att).astype(BF)
                    o['k'].append(k_b.reshape(db, dt, nh, hd))
                    o['v'].append(v_b.reshape(db, dt, nh, hd))
                    o['lf'].append(lf_b)
                w_out = fox_w_out
            elif kind == 1:
                width = sc_w_conv.shape[1]
                if w == 0:
                    init = jnp.zeros((1, SUBLANE_V7X, d), F32)
                else:
                    init = _to_time_major(state_conv_b[j])[None]
                a_mix, tail = _short_conv(grp, hm[w], sc_w_in, j, sc_w_conv[j], init)
                if w == 0:
                    o['cb'].append(tail[:, SUBLANE_V7X - (width - 1):, :])
                else:
                    o['cb'].append(_from_time_major(tail[0], db))
                w_out = sc_w_out
            else:
                width = lru_conv_w.shape[1]
                if w == 0:
                    init = jnp.zeros((1, SUBLANE_V7X, d), F32)
                    h0 = jnp.zeros((1, SUBLANE_V7X, d), F32)
                else:
                    init = _to_time_major(state_conv_c[j])[None]
                    h0 = state_h[j][None]
                a_mix, tail, hl = _rglru(grp, hm[w], lru_w_in, j, lru_conv_w[j], lru_conv_b[j], lru_w_a[j],
                                         lru_b_a[j], lru_w_x[j], lru_b_x[j], lru_lam[j], init, h0)
                if w == 0:
                    o['cc'].append(tail[:, SUBLANE_V7X - (width - 1):, :])
                    o['h'].append(hl[:, 0, :])
                else:
                    o['cc'].append(_from_time_major(tail[0], db))
                    o['h'].append(hl[0])
                w_out = lru_w_out

            res = _out_ln(grp, a_mix, w_out, j, xs[w], mod(i, w, 2), mod(i, w, 4), mod(i, w, 3),
                          ln_g[i, 0], ln_b[i, 0], alpha, moe_w_router[i // 2] if is_moe else None)
            x1[w], hf[w] = res[0], res[1]
            if is_moe:
                gates_t[w] = res[2]

        nxt = [(mod(i + 1, w, 1), mod(i + 1, w, 0)) if not last else (mod(i, w, 1), mod(i, w, 0)) for w in range(2)]
        if is_moe:
            n_real = b * t + ns
            n_all = -(-n_real // MOE_TT) * MOE_TT
            h_all = jnp.concatenate([hf[0], hf[1], jnp.zeros((n_all - n_real, d), BF)], axis=0)
            g_all = jnp.concatenate([gates_t[0], gates_t[1],
                                     jnp.zeros((gates_t[0].shape[0], n_all - n_real), F32)], axis=1)
            f_all = _moe(g_all, h_all, moe_w_gu, moe_w_down, i // 2)
            for w in range(2):
                xs[w], hm[w] = _ln_residual(groups[w], f_all, w * b * t, x1[w], mod(i, w, 5), nxt[w][0], nxt[w][1],
                                            ln_g[i, 1], ln_b[i, 1], alpha, has_next=not last)
        else:
            for w in range(2):
                xs[w], hm[w] = _ffn(groups[w], hf[w], ffn_w_gu, ffn_w_down, i // 2, x1[w], mod(i, w, 5),
                                    nxt[w][0], nxt[w][1], ln_g[i, 1], ln_b[i, 1], alpha, has_next=not last)

    y_prompt = xs[0].reshape(b, t, d)
    y_sample = _from_time_major(xs[1], db)
    op, os_ = outs
    return (y_prompt, y_sample,
            jnp.stack(op['k']), jnp.stack(op['v']), jnp.stack(op['lf']),
            jnp.stack(op['cb']), jnp.stack(op['cc']), jnp.stack(op['h']),
            jnp.stack(os_['k']), jnp.stack(os_['v']), jnp.stack(os_['lf']),
            jnp.stack(os_['cb']), jnp.stack(os_['cc']), jnp.stack(os_['h']))
```

```python
import functools

import jax
import jax.numpy as jnp
from jax import lax
from jax.experimental import pallas as pl
from jax.experimental.pallas import tpu as pltpu

BF = jnp.bfloat16
F32 = jnp.float32

LANE_V7X = 128
SUBLANE_V7X = 8
VMEM_LIMIT_V7X = 52 * 1024 * 1024

FOX_HEAD_DIM = 64
Q_SCALE = FOX_HEAD_DIM ** -0.5
LRU_C = 8.0
LN_EPS = 1e-5
N_MIXERS = 3
TOP_K = 2
PAGES_PER_STEP = 8


def _cparams(sem):
    return pltpu.CompilerParams(dimension_semantics=sem, vmem_limit_bytes=VMEM_LIMIT_V7X)


def _dot(a, b):
    return jnp.dot(a, b, preferred_element_type=F32)


def _dot_nt(a, b):
    return lax.dot_general(a, b, (((1,), (1,)), ((), ())), preferred_element_type=F32)


def _split(a):
    hi = a.astype(BF)
    lo = (a - hi.astype(F32)).astype(BF)
    return hi, lo


def _dot3(a, b):
    ah, al = _split(a)
    bh, bl = _split(b)
    return _dot(ah, bh) + (_dot(ah, bl) + _dot(al, bh))


def _pick(n, cands):
    for c in cands:
        if n % c == 0:
            return c
    return n


class _Group:
    def __init__(self, n_rows, tm, rows_per_seq, stride, n_seq):
        self.n = n_rows
        self.tm = tm
        self.nm = n_rows // tm
        self.rps = rows_per_seq
        self.g = 1 if stride == 1 else tm
        self.stride = stride
        self.tps = rows_per_seq // tm
        self.n_seq = n_seq

    def mod_spec(self, d, m_axis, n_axes):
        tm, rps = self.tm, self.rps

        def idx(*ids):
            return ((ids[m_axis] * tm) // rps, 0, 0)
        return pl.BlockSpec((None, self.g, d), idx)


def _ada_kernel(c_ref, w_ref, b_ref, o_ref):
    c = c_ref[...]
    a = c * jax.nn.sigmoid(c)
    o_ref[...] = _dot3(a, w_ref[...]) + b_ref[...]


def _ada(c_all, w_ada, b_ada):
    depth, d, d6 = w_ada.shape
    s = c_all.shape[0]
    tn = _pick(d6, (1024, 512, 256, 128))
    return pl.pallas_call(
        _ada_kernel,
        out_shape=jax.ShapeDtypeStruct((depth, s, d6), F32),
        grid=(depth, d6 // tn),
        in_specs=[pl.BlockSpec((s, d), lambda l, n: (0, 0)),
                  pl.BlockSpec((None, d, tn), lambda l, n: (l, 0, n)),
                  pl.BlockSpec((None, 1, tn), lambda l, n: (l, 0, n))],
        out_specs=pl.BlockSpec((None, s, tn), lambda l, n: (l, 0, n)),
        compiler_params=_cparams(("arbitrary", "arbitrary")),
        name="ada_mod",
    )(c_all, w_ada, b_ada.reshape(depth, 1, d6))


def _modulate_kernel(x_ref, sc_ref, sh_ref, o_ref):
    o_ref[...] = (x_ref[...] * (1.0 + sc_ref[...]) + sh_ref[...]).astype(BF)


def _modulate(grp, x, sc, sh):
    n, d = x.shape
    tm = grp.tm
    return pl.pallas_call(
        _modulate_kernel,
        out_shape=jax.ShapeDtypeStruct((n, d), BF),
        grid=(grp.nm,),
        in_specs=[pl.BlockSpec((tm, d), lambda m: (m, 0)),
                  grp.mod_spec(d, 0, 1), grp.mod_spec(d, 0, 1)],
        out_specs=pl.BlockSpec((tm, d), lambda m: (m, 0)),
        compiler_params=_cparams(("arbitrary",)),
        name="modulate",
    )(x, sc, sh)


def _linear_kernel(x_ref, w_ref, *refs, scale, n_out):
    outs, wb = refs[:n_out], refs[n_out]

    @pl.when(pl.program_id(1) == 0)
    def _():
        wb[...] = w_ref[...].astype(BF)

    y = _dot(x_ref[...], wb[...])
    if scale != 1.0:
        y = y * scale
    for o in outs:
        o[...] = y.astype(o.dtype)


def _linear(grp, x, w3, layer, col0, n_cols, out_dtypes, scale=1.0):
    n, k = x.shape
    tm = grp.tm
    tn = _pick(n_cols, (512, 256, 128))
    assert col0 % tn == 0
    cb0 = col0 // tn
    outs = [jax.ShapeDtypeStruct((n, n_cols), dt) for dt in out_dtypes]
    return pl.pallas_call(
        functools.partial(_linear_kernel, scale=scale, n_out=len(outs)),
        out_shape=outs,
        grid=(n_cols // tn, grp.nm),
        in_specs=[pl.BlockSpec((tm, k), lambda c, m: (m, 0)),
                  pl.BlockSpec((None, k, tn), lambda c, m: (layer, 0, cb0 + c))],
        out_specs=[pl.BlockSpec((tm, tn), lambda c, m: (m, c)) for _ in outs],
        scratch_shapes=[pltpu.VMEM((k, tn), BF)],
        compiler_params=_cparams(("arbitrary", "arbitrary")),
        name="linear",
    )(x, w3)


def _linear_t_kernel(x_ref, wt_ref, dst_in_ref, yt32_ref, yb_ref, wb, *, rows_out):
    del dst_in_ref

    @pl.when(pl.program_id(1) == 0)
    def _():
        wb[...] = wt_ref[...].astype(BF)

    x = x_ref[...]
    yt = _dot_nt(wb[...], x)
    yt32_ref[...] = yt
    if rows_out:
        yb_ref[...] = _dot_nt(x, wb[...]).astype(yb_ref.dtype)
    else:
        yb_ref[...] = yt.astype(yb_ref.dtype)


def _linear_t(grp, x, wt, dst, layer, rows_out):
    n, k = x.shape
    nl, nb, f, t = dst.shape
    tm = grp.tm
    tps = grp.tps
    tn = _pick(f, (512, 256, 128))
    if rows_out:
        yb_shape = jax.ShapeDtypeStruct((n, f), BF)
        yb_spec = pl.BlockSpec((tm, tn), lambda c, m: (m, c))
    else:
        yb_shape = jax.ShapeDtypeStruct((nb, f, t), BF)
        yb_spec = pl.BlockSpec((None, tn, tm), lambda c, m: (m // tps, c, m % tps))
    return pl.pallas_call(
        functools.partial(_linear_t_kernel, rows_out=rows_out),
        out_shape=[jax.ShapeDtypeStruct(dst.shape, F32), yb_shape],
        grid=(f // tn, grp.nm),
        in_specs=[pl.BlockSpec((tm, k), lambda c, m: (m, 0)),
                  pl.BlockSpec((tn, k), lambda c, m: (c, 0)),
                  pl.BlockSpec(memory_space=pl.ANY)],
        out_specs=[pl.BlockSpec((None, None, tn, tm), lambda c, m: (layer, m // tps, c, m % tps)), yb_spec],
        scratch_shapes=[pltpu.VMEM((tn, k), BF)],
        input_output_aliases={2: 0},
        compiler_params=_cparams(("arbitrary", "arbitrary")),
        name="linear_t",
    )(x, wt, dst)


def _log_sigmoid(z):
    return jnp.minimum(z, 0.0) - jnp.log1p(jnp.exp(-jnp.abs(z)))


def _forget_kernel(x_ref, w_ref, b_ref, o_ref):
    z = _dot(x_ref[...], w_ref[...].astype(BF)) + b_ref[...]
    o_ref[...] = _log_sigmoid(z)


def _forget(grp, x, w_f, b_f):
    n, k = x.shape
    h = w_f.shape[1]
    tm = grp.tm
    return pl.pallas_call(
        _forget_kernel,
        out_shape=jax.ShapeDtypeStruct((n, h), F32),
        grid=(grp.nm,),
        in_specs=[pl.BlockSpec((tm, k), lambda m: (m, 0)),
                  pl.BlockSpec((k, h), lambda m: (0, 0)),
                  pl.BlockSpec((1, h), lambda m: (0, 0))],
        out_specs=pl.BlockSpec((tm, h), lambda m: (m, 0)),
        compiler_params=_cparams(("arbitrary",)),
        name="forget_gate",
    )(x, w_f, b_f.reshape(1, h))


def _prefix_lanes(x):
    n = x.shape[-1]
    lane = lax.broadcasted_iota(jnp.int32, x.shape, x.ndim - 1)
    d = 1
    while d < n:
        x = x + jnp.where(lane >= d, pltpu.roll(x, d, x.ndim - 1), 0.0)
        d *= 2
    return x


def _cumsum_kernel(x_ref, o_ref):
    o_ref[...] = _prefix_lanes(x_ref[...])


def _cumsum_time(lf_t):
    b, h, t = lf_t.shape
    return pl.pallas_call(
        _cumsum_kernel,
        out_shape=jax.ShapeDtypeStruct((b, h, t), F32),
        grid=(b,),
        in_specs=[pl.BlockSpec((None, h, t), lambda i: (i, 0, 0))],
        out_specs=pl.BlockSpec((None, h, t), lambda i: (i, 0, 0)),
        compiler_params=_cparams(("arbitrary",)),
        name="forget_cumsum",
    )(lf_t)


def _attn_kernel(q_ref, k_ref, v_ref, f_ref, o_ref, m_sc, l_sc, acc_sc, *, tq):
    qi = pl.program_id(2)
    half = LANE_V7X // 2
    lane = lax.broadcasted_iota(jnp.int32, (tq, LANE_V7X), 1)
    lo = lane < half
    q = q_ref[...]
    zero = jnp.zeros_like(q)
    qs = (jnp.where(lo, q, zero), jnp.where(lo, zero, q))
    for h in range(2):
        m_sc[h] = jnp.full((tq, LANE_V7X), -jnp.inf, F32)
        l_sc[h] = jnp.zeros((tq, LANE_V7X), F32)
        acc_sc[h] = jnp.zeros((tq, LANE_V7X), F32)

    def block(k0, masked):
        kb = k_ref[:, pl.ds(k0, tq)]
        vb = v_ref[pl.ds(k0, tq), :]
        for h in range(2):
            s = _dot(qs[h], kb) - f_ref[h:h + 1, pl.ds(k0, tq)]
            if masked:
                row = lax.broadcasted_iota(jnp.int32, (tq, tq), 0)
                col = lax.broadcasted_iota(jnp.int32, (tq, tq), 1)
                s = jnp.where(col <= row, s, -jnp.inf)
            m_prev = m_sc[h]
            m_new = jnp.maximum(m_prev, jnp.max(s, axis=1, keepdims=True))
            alpha = jnp.exp(m_prev - m_new)
            p = jnp.exp(s - m_new[:, 0:1])
            l_sc[h] = alpha * l_sc[h] + jnp.sum(p, axis=1, keepdims=True)
            acc_sc[h] = alpha * acc_sc[h] + _dot(p.astype(BF), vb)
            m_sc[h] = m_new

    def body(i, carry):
        block(pl.multiple_of(i * tq, tq), False)
        return carry

    lax.fori_loop(0, qi, body, 0)
    block(pl.multiple_of(qi * tq, tq), True)
    o = jnp.where(lo, acc_sc[0] / l_sc[0], acc_sc[1] / l_sc[1])
    o_ref[...] = o.astype(o_ref.dtype)


def _attention_prompt(q, kt, v, fcum, b, t):
    n, d = q.shape
    hp = d // LANE_V7X
    tq = _pick(t, (512, 256, 128))
    nq = t // tq
    return pl.pallas_call(
        functools.partial(_attn_kernel, tq=tq),
        out_shape=jax.ShapeDtypeStruct((n, d), BF),
        grid=(b, hp, nq),
        in_specs=[pl.BlockSpec((tq, LANE_V7X), lambda i, j, qi: (i * nq + qi, j)),
                  pl.BlockSpec((None, LANE_V7X, t), lambda i, j, qi: (i, j, 0)),
                  pl.BlockSpec((t, LANE_V7X), lambda i, j, qi: (i, j)),
                  pl.BlockSpec((None, None, 2, t), lambda i, j, qi: (i, j, 0, 0))],
        out_specs=pl.BlockSpec((tq, LANE_V7X), lambda i, j, qi: (i * nq + qi, j)),
        scratch_shapes=[pltpu.VMEM((2, tq, LANE_V7X), F32)] * 3,
        compiler_params=_cparams(("arbitrary", "arbitrary", "arbitrary")),
        name="fox_attention_prompt",
    )(q, kt, v, fcum)


def _split3(x):
    hi = x.astype(BF)
    r1 = x - hi.astype(F32)
    mid = r1.astype(BF)
    lo = (r1 - mid.astype(F32)).astype(BF)
    return hi, mid, lo


def _decode_kernel(pt_ref, q_ref, kn_ref, vn_ref, lfn_ref, *refs, pp, nh, nk):
    k_refs = refs[:pp]
    v_refs = refs[pp:2 * pp]
    lf_refs = refs[2 * pp:3 * pp]
    o_ref, qbd, m_sc, l_sc, acc_sc, c_sc = refs[3 * pp:]
    g = pl.program_id(1)
    dt, d = q_ref.shape
    hd = d // nh
    r = dt * nh
    own = (lax.rem(lax.broadcasted_iota(jnp.int32, (r, d), 0), nh)
           == lax.div(lax.broadcasted_iota(jnp.int32, (r, d), 1), hd))

    @pl.when(g == 0)
    def _():
        q = q_ref[...] * Q_SCALE
        qrep = jnp.concatenate([jnp.broadcast_to(q[t:t + 1], (nh, d)) for t in range(dt)], axis=0)
        qbd[...] = jnp.where(own, qrep, 0.0).astype(BF)
        m_sc[...] = jnp.full(m_sc.shape, -jnp.inf, F32)
        l_sc[...] = jnp.zeros(l_sc.shape, F32)
        acc_sc[...] = jnp.zeros(acc_sc.shape, F32)
        c_sc[...] = jnp.zeros(c_sc.shape, F32)

    def softmax_step(s):
        m_prev = m_sc[...]
        m_new = jnp.maximum(m_prev, jnp.max(s, axis=1, keepdims=True))
        alpha = jnp.exp(m_prev - m_new)
        p = jnp.exp(s - m_new[:, 0:1])
        l_sc[...] = alpha * l_sc[...] + jnp.sum(p, axis=1, keepdims=True)
        m_sc[...] = m_new
        return alpha[:, 0:1], p.astype(BF)

    def d_by_keys(ref):
        x = ref[...]
        return x.reshape(x.shape[0] * x.shape[1], x.shape[2]).astype(BF)

    qb = qbd[...]
    lf_all = jnp.concatenate([lf_refs[i][...] for i in range(pp)], axis=1)
    cum = _prefix_lanes(lf_all) + c_sc[:, 0:1]
    c_sc[...] = c_sc[...] + jnp.sum(lf_all, axis=1, keepdims=True)
    kt = jnp.concatenate([d_by_keys(k_refs[i]) for i in range(pp)], axis=1)
    vt = jnp.concatenate([d_by_keys(v_refs[i]) for i in range(pp)], axis=1)
    alpha, p = softmax_step(_dot(qb, kt) - jnp.concatenate([cum] * dt, axis=0))
    acc_sc[...] = alpha * acc_sc[...] + _dot_nt(p, vt)

    @pl.when(g == pl.num_programs(1) - 1)
    def _():
        cum_n = (_prefix_lanes(lfn_ref[...]) + c_sc[:, 0:1])[:, 0:nk]
        sn = _dot_nt(qb, kn_ref[...].astype(BF)) - jnp.concatenate([cum_n] * dt, axis=0)
        trow = lax.div(lax.broadcasted_iota(jnp.int32, (r, nk), 0), nh)
        col = lax.broadcasted_iota(jnp.int32, (r, nk), 1)
        alpha_n, pn = softmax_step(jnp.where(col <= trow, sn, -jnp.inf))
        acc = alpha_n * acc_sc[...] + _dot(pn, vn_ref[...].astype(BF))
        o = jnp.where(own, acc / l_sc[:, 0:1], 0.0)
        o_ref[...] = jnp.sum(o.reshape(dt, nh, d), axis=1)


def _attention_sample(layer, page_table, q_b, k_new, v_new, lf_new_t, cache_kt, cache_vt, cache_lft):
    db, dt, d = q_b.shape
    nk = k_new.shape[1]
    nh, hd, page = cache_kt.shape[2:]
    n_pages = page_table.shape[1]
    pp = _pick(n_pages, (PAGES_PER_STEP, 4, 2, 1))
    r = dt * nh

    def page_spec(shape, i):
        zeros = (0,) * len(shape)
        return pl.BlockSpec((None, None) + shape, lambda b, g, pt: (layer, pt[b, g * pp + i]) + zeros)

    in_specs = [pl.BlockSpec((None, dt, d), lambda b, g, pt: (b, 0, 0)),
                pl.BlockSpec((None, nk, d), lambda b, g, pt: (b, 0, 0)),
                pl.BlockSpec((None, nk, d), lambda b, g, pt: (b, 0, 0)),
                pl.BlockSpec((None, nh, LANE_V7X), lambda b, g, pt: (b, 0, 0))]
    in_specs += [page_spec((nh, hd, page), i) for i in range(pp)]
    in_specs += [page_spec((nh, hd, page), i) for i in range(pp)]
    in_specs += [page_spec((nh, page), i) for i in range(pp)]
    grid_spec = pltpu.PrefetchScalarGridSpec(
        num_scalar_prefetch=1,
        grid=(db, n_pages // pp),
        in_specs=in_specs,
        out_specs=pl.BlockSpec((None, dt, d), lambda b, g, pt: (b, 0, 0)),
        scratch_shapes=[pltpu.VMEM((r, d), BF),
                        pltpu.VMEM((r, LANE_V7X), F32),
                        pltpu.VMEM((r, LANE_V7X), F32),
                        pltpu.VMEM((r, d), F32),
                        pltpu.VMEM((nh, LANE_V7X), F32)])
    return pl.pallas_call(
        functools.partial(_decode_kernel, pp=pp, nh=nh, nk=nk),
        out_shape=jax.ShapeDtypeStruct((db, dt, d), F32),
        grid_spec=grid_spec,
        compiler_params=_cparams(("arbitrary", "arbitrary")),
        name="fox_attention_sample",
    )(page_table, q_b, k_new, v_new, lf_new_t,
      *([cache_kt] * pp), *([cache_vt] * pp), *([cache_lft] * pp))


def _conv_tile(e_ref, carry_ref, init_ref, tail_ref, u, w_ref, first, *, hb, stride, width):
    tm = u.shape[0]

    @pl.when(first)
    def _():
        e_ref[0:hb, :] = init_ref[...]

    @pl.when(jnp.logical_not(first))
    def _():
        e_ref[0:hb, :] = carry_ref[...]

    e_ref[hb:hb + tm, :] = u
    y = None
    for j in range(width):
        off = hb - (width - 1 - j) * stride
        term = w_ref[j:j + 1, :] * e_ref[off:off + tm, :]
        y = term if y is None else y + term
    tail = e_ref[tm:tm + hb, :]
    carry_ref[...] = tail
    tail_ref[...] = tail
    return y


def _sc_kernel(x_ref, wb_ref, wc_ref, wv_ref, wconv_ref, init_ref, mix_ref, tail_ref,
               wb, e_sc, carry, *, tps, hb, stride, width):
    m = pl.program_id(1)

    @pl.when(m == 0)
    def _():
        wb[0] = wb_ref[...].astype(BF)
        wb[1] = wc_ref[...].astype(BF)
        wb[2] = wv_ref[...].astype(BF)

    x = x_ref[...]
    g_b = _dot(x, wb[0])
    u = _dot(x, wb[1]) * _dot(x, wb[2])
    y = _conv_tile(e_sc, carry, init_ref, tail_ref, u, wconv_ref, (m % tps) == 0,
                   hb=hb, stride=stride, width=width)
    mix_ref[...] = (g_b * y).astype(mix_ref.dtype)


def _short_conv(grp, x, w_in, layer, w_conv, init):
    n, d = x.shape
    tm = grp.tm
    width = w_conv.shape[0]
    hb = init.shape[1]
    tn = _pick(d, (256, 128))
    nb = d // tn
    tps = grp.tps
    n_init = init.shape[0]

    def wspec(part):
        return pl.BlockSpec((None, d, tn), lambda c, m: (layer, 0, part * nb + c))

    def seq_of(m):
        return m // tps

    return pl.pallas_call(
        functools.partial(_sc_kernel, tps=tps, hb=hb, stride=grp.stride, width=width),
        out_shape=[jax.ShapeDtypeStruct((n, d), BF),
                   jax.ShapeDtypeStruct((grp.n_seq, hb, d), F32)],
        grid=(nb, grp.nm),
        in_specs=[pl.BlockSpec((tm, d), lambda c, m: (m, 0)),
                  wspec(0), wspec(1), wspec(2),
                  pl.BlockSpec((width, tn), lambda c, m: (0, c)),
                  pl.BlockSpec((None, hb, tn), lambda c, m: (seq_of(m) % n_init, 0, c))],
        out_specs=[pl.BlockSpec((tm, tn), lambda c, m: (m, c)),
                   pl.BlockSpec((None, hb, tn), lambda c, m: (seq_of(m), 0, c))],
        scratch_shapes=[pltpu.VMEM((3, d, tn), BF),
                        pltpu.VMEM((hb + tm, tn), F32),
                        pltpu.VMEM((hb, tn), F32)],
        compiler_params=_cparams(("arbitrary", "arbitrary")),
        name="short_conv_mixer",
    )(x, w_in, w_in, w_in, w_conv, init)


def _softplus(x):
    return jnp.maximum(x, 0.0) + jnp.log1p(jnp.exp(-jnp.abs(x)))


def _neg_expm1(x):
    th = jnp.tanh(0.5 * x)
    return -2.0 * th / (1.0 - th)


def _scan_rows(a, u):
    tm = a.shape[0]
    row = lax.broadcasted_iota(jnp.int32, a.shape, 0)
    d = 1
    while d < tm:
        keep = row >= d
        a_s = jnp.where(keep, pltpu.roll(a, d, 0), 1.0)
        u_s = jnp.where(keep, pltpu.roll(u, d, 0), 0.0)
        u = u + a * u_s
        a = a * a_s
        d *= 2
    return a, u


def _lru_kernel(x_ref, wg_ref, wr_ref, cw_ref, cb_ref, bda_ref, bdx_ref, ba_ref, bx_ref, lam_ref,
                init_ref, h0_ref, mix_ref, tail_ref, hlast_ref,
                wb, bdb, e_sc, carry, hcarry, *, tps, hb, stride, width):
    m = pl.program_id(1)
    tm = x_ref.shape[0]

    @pl.when(m == 0)
    def _():
        wb[0] = wg_ref[...].astype(BF)
        wb[1] = wr_ref[...].astype(BF)
        bdb[0] = bda_ref[...].astype(BF)
        bdb[1] = bdx_ref[...].astype(BF)

    first = (m % tps) == 0
    x = x_ref[...]
    gate = jax.nn.gelu(_dot(x, wb[0]), approximate=True)
    xr = _dot(x, wb[1])
    xc = _conv_tile(e_sc, carry, init_ref, tail_ref, xr, cw_ref, first,
                    hb=hb, stride=stride, width=width) + cb_ref[...]
    xcb = xc.astype(BF)
    r = jax.nn.sigmoid(_dot(xcb, bdb[0]) + ba_ref[...])
    i = jax.nn.sigmoid(_dot(xcb, bdb[1]) + bx_ref[...])
    log_a = (-LRU_C) * r * _softplus(-lam_ref[...])
    a = jnp.exp(log_a)
    u = jnp.sqrt(_neg_expm1(2.0 * log_a)) * i * xc

    @pl.when(first)
    def _():
        hcarry[...] = h0_ref[...]

    if stride == 1:
        a_cum, h_loc = _scan_rows(a, u)
        hs = h_loc + a_cum * hcarry[0:1, :]
        mix_ref[...] = (gate * hs).astype(mix_ref.dtype)
        h_end = jnp.broadcast_to(hs[tm - 1:tm, :], hcarry.shape)
    else:
        h = hcarry[...]
        for t in range(tm // stride):
            sl = slice(t * stride, (t + 1) * stride)
            h = a[sl] * h + u[sl]
            mix_ref[sl, :] = (gate[sl] * h).astype(mix_ref.dtype)
        h_end = h
    hcarry[...] = h_end
    hlast_ref[...] = h_end


def _block_diag(w, per):
    nblk, bd, _ = w.shape
    eye = jnp.eye(per, dtype=w.dtype)
    return jnp.einsum('cide,ik->cidke', w.reshape(nblk // per, per, bd, bd), eye).reshape(
        nblk // per, per * bd, per * bd)


def _rglru(grp, x, w_in, layer, conv_w, conv_b, w_a, b_a, w_x, b_x, lam, init, h0):
    n, d = x.shape
    rw = conv_w.shape[1]
    tm = grp.tm
    width = conv_w.shape[0]
    hb = init.shape[1]
    hr = h0.shape[1]
    bd = w_a.shape[1]
    tn = _pick(rw, (256, 128))
    nb = rw // tn
    tps = grp.tps
    n_init = init.shape[0]
    bda = _block_diag(w_a, tn // bd)
    bdx = _block_diag(w_x, tn // bd)

    def row_spec():
        return pl.BlockSpec((1, tn), lambda c, m: (0, c))

    def seq_of(m):
        return m // tps

    return pl.pallas_call(
        functools.partial(_lru_kernel, tps=tps, hb=hb, stride=grp.stride, width=width),
        out_shape=[jax.ShapeDtypeStruct((n, rw), BF),
                   jax.ShapeDtypeStruct((grp.n_seq, hb, rw), F32),
                   jax.ShapeDtypeStruct((grp.n_seq, hr, rw), F32)],
        grid=(nb, grp.nm),
        in_specs=[pl.BlockSpec((tm, d), lambda c, m: (m, 0)),
                  pl.BlockSpec((None, d, tn), lambda c, m: (layer, 0, c)),
                  pl.BlockSpec((None, d, tn), lambda c, m: (layer, 0, nb + c)),
                  pl.BlockSpec((width, tn), lambda c, m: (0, c)),
                  row_spec(),
                  pl.BlockSpec((None, tn, tn), lambda c, m: (c, 0, 0)),
                  pl.BlockSpec((None, tn, tn), lambda c, m: (c, 0, 0)),
                  row_spec(), row_spec(), row_spec(),
                  pl.BlockSpec((None, hb, tn), lambda c, m: (seq_of(m) % n_init, 0, c)),
                  pl.BlockSpec((None, hr, tn), lambda c, m: (seq_of(m) % n_init, 0, c))],
        out_specs=[pl.BlockSpec((tm, tn), lambda c, m: (m, c)),
                   pl.BlockSpec((None, hb, tn), lambda c, m: (seq_of(m), 0, c)),
                   pl.BlockSpec((None, hr, tn), lambda c, m: (seq_of(m), 0, c))],
        scratch_shapes=[pltpu.VMEM((2, d, tn), BF),
                        pltpu.VMEM((2, tn, tn), BF),
                        pltpu.VMEM((hb + tm, tn), F32),
                        pltpu.VMEM((hb, tn), F32),
                        pltpu.VMEM((hr, tn), F32)],
        compiler_params=_cparams(("arbitrary", "arbitrary")),
        name="rglru_mixer",
    )(x, w_in, w_in, conv_w, conv_b.reshape(1, rw), bda, bdx, b_a.reshape(1, rw),
      b_x.reshape(1, rw), lam.reshape(1, rw), init, h0)


def _post_ln(x, f, gate, ln_g, ln_b, alpha):
    z = alpha * x + (1.0 + gate) * f
    mu = jnp.mean(z, axis=-1, keepdims=True)
    zc = z - mu
    var = jnp.mean(zc * zc, axis=-1, keepdims=True)
    return zc * lax.rsqrt(var + LN_EPS) * ln_g + ln_b


def _top2_gates_t(logits_t):
    ne = float(logits_t.shape[0])
    idx = lax.broadcasted_iota(jnp.int32, logits_t.shape, 0).astype(F32)
    m1 = jnp.max(logits_t, axis=0, keepdims=True)
    i1 = jnp.min(jnp.where(logits_t == m1, idx, ne), axis=0, keepdims=True)
    sel1 = idx == i1
    rest = jnp.where(sel1, -jnp.inf, logits_t)
    m2 = jnp.max(rest, axis=0, keepdims=True)
    i2 = jnp.min(jnp.where(rest == m2, idx, ne), axis=0, keepdims=True)
    sel2 = idx == i2
    e2 = jnp.exp(m2 - m1)
    w1 = 1.0 / (1.0 + e2)
    w2 = e2 / (1.0 + e2)
    return jnp.where(sel1, w1, 0.0) + jnp.where(sel2, w2, 0.0)


def _out_ln_kernel(a_ref, w_ref, x_ref, g_ref, sc_ref, sh_ref, lng_ref, lnb_ref, *refs,
                   alpha, has_router):
    if has_router:
        wr_ref, x1_ref, hf_ref, gates_ref, wb = refs
    else:
        x1_ref, hf_ref, wb = refs

    @pl.when(pl.program_id(0) == 0)
    def _():
        wb[...] = w_ref[...].astype(BF)

    y = _dot(a_ref[...], wb[...])
    x1 = _post_ln(x_ref[...], y, g_ref[...], lng_ref[...], lnb_ref[...], alpha)
    x1_ref[...] = x1
    hf = x1 * (1.0 + sc_ref[...]) + sh_ref[...]
    hf_ref[...] = hf.astype(hf_ref.dtype)
    if has_router:
        wh, wl = _split(wr_ref[...])
        hh, hl = _split(hf)
        logits_t = _dot_nt(wh, hh) + (_dot_nt(wh, hl) + _dot_nt(wl, hh))
        gates_ref[...] = _top2_gates_t(logits_t)


def _out_ln(grp, a, w3, layer, x, gate, sc, sh, ln_g, ln_b, alpha, w_router=None):
    n, d = x.shape
    k = a.shape[1]
    tm = grp.tm
    has_router = w_router is not None
    in_specs = [pl.BlockSpec((tm, k), lambda m: (m, 0)),
                pl.BlockSpec((None, k, d), lambda m: (layer, 0, 0)),
                pl.BlockSpec((tm, d), lambda m: (m, 0)),
                grp.mod_spec(d, 0, 1), grp.mod_spec(d, 0, 1), grp.mod_spec(d, 0, 1),
                pl.BlockSpec((1, d), lambda m: (0, 0)),
                pl.BlockSpec((1, d), lambda m: (0, 0))]
    args = [a, w3, x, gate, sc, sh, ln_g.reshape(1, d), ln_b.reshape(1, d)]
    out_shape = [jax.ShapeDtypeStruct((n, d), F32), jax.ShapeDtypeStruct((n, d), BF)]
    out_specs = [pl.BlockSpec((tm, d), lambda m: (m, 0)), pl.BlockSpec((tm, d), lambda m: (m, 0))]
    if has_router:
        ne = w_router.shape[1]
        in_specs.append(pl.BlockSpec((ne, d), lambda m: (0, 0)))
        args.append(w_router.T)
        out_shape.append(jax.ShapeDtypeStruct((ne, n), F32))
        out_specs.append(pl.BlockSpec((ne, tm), lambda m: (0, m)))
    return pl.pallas_call(
        functools.partial(_out_ln_kernel, alpha=alpha, has_router=has_router),
        out_shape=out_shape,
        grid=(grp.nm,),
        in_specs=in_specs,
        out_specs=out_specs,
        scratch_shapes=[pltpu.VMEM((k, d), BF)],
        compiler_params=_cparams(("arbitrary",)),
        name="out_proj_ln",
    )(*args)


def _swiglu_part(h, wg, wu, wd):
    gu = _dot(h, wg.astype(BF))
    up = _dot(h, wu.astype(BF))
    act = (gu * jax.nn.sigmoid(gu) * up).astype(BF)
    return _dot(act, wd.astype(BF))


def _ffn_kernel(h_ref, wg_ref, wu_ref, wd_ref, x_ref, g_ref, sc_ref, sh_ref, lng_ref, lnb_ref, *refs,
                alpha, has_next):
    if has_next:
        x2_ref, hn_ref, acc = refs
    else:
        x2_ref, acc = refs
    ch = pl.program_id(1)
    part = _swiglu_part(h_ref[...], wg_ref[...], wu_ref[...], wd_ref[...])

    @pl.when(ch == 0)
    def _():
        acc[...] = part

    @pl.when(ch > 0)
    def _():
        acc[...] += part

    @pl.when(ch == pl.num_programs(1) - 1)
    def _():
        x2 = _post_ln(x_ref[...], acc[...], g_ref[...], lng_ref[...], lnb_ref[...], alpha)
        x2_ref[...] = x2
        if has_next:
            hn_ref[...] = (x2 * (1.0 + sc_ref[...]) + sh_ref[...]).astype(hn_ref.dtype)


def _ffn(grp, h, w_gu, w_down, layer, x, gate, sc, sh, ln_g, ln_b, alpha, has_next=True):
    n, d = x.shape
    tm = grp.tm
    ff = w_down.shape[1]
    tf = _pick(ff, (256, 128))
    nch = ff // tf
    row = pl.BlockSpec((tm, d), lambda m, c: (m, 0))
    vec = pl.BlockSpec((1, d), lambda m, c: (0, 0))
    in_specs = [row,
                pl.BlockSpec((None, d, tf), lambda m, c: (layer, 0, c)),
                pl.BlockSpec((None, d, tf), lambda m, c: (layer, 0, nch + c)),
                pl.BlockSpec((None, tf, d), lambda m, c: (layer, c, 0)),
                row, grp.mod_spec(d, 0, 2), grp.mod_spec(d, 0, 2), grp.mod_spec(d, 0, 2), vec, vec]
    out_shape = [jax.ShapeDtypeStruct((n, d), F32)]
    out_specs = [row]
    if has_next:
        out_shape.append(jax.ShapeDtypeStruct((n, d), BF))
        out_specs.append(row)
    res = pl.pallas_call(
        functools.partial(_ffn_kernel, alpha=alpha, has_next=has_next),
        out_shape=out_shape,
        grid=(grp.nm, nch),
        in_specs=in_specs,
        out_specs=out_specs,
        scratch_shapes=[pltpu.VMEM((tm, d), F32)],
        compiler_params=_cparams(("arbitrary", "arbitrary")),
        name="ffn_dense",
    )(h, w_gu, w_gu, w_down, x, gate, sc, sh, ln_g.reshape(1, d), ln_b.reshape(1, d))
    return (res[0], res[1]) if has_next else (res[0], None)


MOE_TT = 512
MOE_RC = 128
MOE_TMG = 1024
MOE_ALIGN = 16


def _seg_sizes(tt):
    sizes = []
    sz = tt
    while sz >= MOE_ALIGN:
        sizes.append(sz)
        sz //= 2
    return tuple(sizes)


def _segments(c16, sizes):
    off = jnp.int32(0)
    for j, sz in enumerate(sizes):
        cond = (c16 & sz) != 0
        yield cond, off, sz, j
        off = off + jnp.where(cond, sz, 0)


def _route_count_kernel(g_ref, o_ref):
    routed = jnp.where(g_ref[...] > 0.0, 1.0, 0.0)
    o_ref[...] = jnp.broadcast_to(jnp.sum(routed, axis=1, keepdims=True), o_ref.shape)


def _route_counts(gates_t, tt):
    ne, n = gates_t.shape
    nt = n // tt
    return pl.pallas_call(
        _route_count_kernel,
        out_shape=jax.ShapeDtypeStruct((nt, ne, LANE_V7X), F32),
        grid=(nt,),
        in_specs=[pl.BlockSpec((ne, tt), lambda i: (0, i))],
        out_specs=pl.BlockSpec((None, ne, LANE_V7X), lambda i: (i, 0, 0)),
        compiler_params=_cparams(("arbitrary",)),
        name="moe_route_counts",
    )(gates_t)


def _dispatch_kernel(base_ref, c16_ref, g_ref, h_ref, sorted_in_ref, sorted_ref, stage, sems,
                     *, ne, tt, rc, sizes):
    del sorted_in_ref
    i = pl.program_id(0)
    gt = g_ref[...]
    routed = gt > 0.0
    ri = lax.broadcasted_iota(jnp.int32, (tt, tt), 0)
    ci = lax.broadcasted_iota(jnp.int32, (tt, tt), 1)
    before = jnp.where(ri < ci, 1.0, 0.0).astype(BF)
    rank = _dot(jnp.where(routed, 1.0, 0.0).astype(BF), before)
    rank = jnp.where(routed, rank, -1.0)
    h = h_ref[...]

    def seg_copy(e, b0, off, sz, j):
        return pltpu.make_async_copy(
            stage.at[e, pl.ds(pl.multiple_of(off, MOE_ALIGN), sz)],
            sorted_ref.at[pl.ds(pl.multiple_of(b0 + off, MOE_ALIGN), sz)],
            sems.at[e, j])

    for e in range(ne):
        c16 = c16_ref[i * ne + e]
        b0 = base_ref[i * ne + e]
        for c in range(tt // rc):
            @pl.when(c * rc < c16)
            def _():
                slot = (lax.broadcasted_iota(jnp.int32, (rc, tt), 0) + c * rc).astype(F32)
                sel = slot == rank[e:e + 1, :]
                stage[e, c * rc:(c + 1) * rc, :] = _dot(jnp.where(sel, 1.0, 0.0).astype(BF), h).astype(BF)
        for cond, off, sz, j in _segments(c16, sizes):
            @pl.when(cond)
            def _():
                seg_copy(e, b0, off, sz, j).start()

    for e in range(ne):
        c16 = c16_ref[i * ne + e]
        b0 = base_ref[i * ne + e]
        for cond, off, sz, j in _segments(c16, sizes):
            @pl.when(cond)
            def _():
                seg_copy(e, b0, off, sz, j).wait()


def _dispatch(base, c16, gates_t, h_all, s_max, tt):
    ne, n = gates_t.shape
    d = h_all.shape[1]
    sizes = _seg_sizes(tt)
    grid_spec = pltpu.PrefetchScalarGridSpec(
        num_scalar_prefetch=2,
        grid=(n // tt,),
        in_specs=[pl.BlockSpec((ne, tt), lambda i, b, c: (0, i)),
                  pl.BlockSpec((tt, d), lambda i, b, c: (i, 0)),
                  pl.BlockSpec(memory_space=pl.ANY)],
        out_specs=pl.BlockSpec(memory_space=pl.ANY),
        scratch_shapes=[pltpu.VMEM((ne, tt, d), BF),
                        pltpu.SemaphoreType.DMA((ne, len(sizes)))])
    return pl.pallas_call(
        functools.partial(_dispatch_kernel, ne=ne, tt=tt, rc=min(MOE_RC, tt), sizes=sizes),
        out_shape=jax.ShapeDtypeStruct((s_max, d), BF),
        grid_spec=grid_spec,
        input_output_aliases={4: 0},
        compiler_params=_cparams(("arbitrary",)),
        name="moe_dispatch",
    )(base, c16, gates_t, h_all, jnp.zeros((s_max, d), BF))


def _grouped_kernel(te_ref, nu_ref, x_ref, wg_ref, wu_ref, wd_ref, y_ref, acc):
    del te_ref
    g = pl.program_id(0)
    ch = pl.program_id(1)
    last = pl.num_programs(1) - 1

    @pl.when(g < nu_ref[0])
    def _():
        part = _swiglu_part(x_ref[...], wg_ref[...], wu_ref[...], wd_ref[...])

        @pl.when(ch == 0)
        def _():
            acc[...] = part

        @pl.when(ch > 0)
        def _():
            acc[...] += part

        @pl.when(ch == last)
        def _():
            y_ref[...] = acc[...].astype(y_ref.dtype)

    @pl.when(jnp.logical_and(g >= nu_ref[0], ch == last))
    def _():
        y_ref[...] = jnp.zeros(y_ref.shape, y_ref.dtype)


def _grouped_ffn(tile_expert, n_used, xs, w_gu, w_down, layer):
    s_max, d = xs.shape
    ff = w_down.shape[2]
    tmg = MOE_TMG
    tf = _pick(ff, (512, 256, 128))
    nch = ff // tf

    def row(g, c, te, nu):
        return (jnp.minimum(g, nu[0] - 1), 0)

    def chunk(g, c, nu):
        return jnp.where(g < nu[0], c, nch - 1)

    grid_spec = pltpu.PrefetchScalarGridSpec(
        num_scalar_prefetch=2,
        grid=(s_max // tmg, nch),
        in_specs=[pl.BlockSpec((tmg, d), row),
                  pl.BlockSpec((None, None, d, tf), lambda g, c, te, nu: (layer, te[g], 0, chunk(g, c, nu))),
                  pl.BlockSpec((None, None, d, tf), lambda g, c, te, nu: (layer, te[g], 0, nch + chunk(g, c, nu))),
                  pl.BlockSpec((None, None, tf, d), lambda g, c, te, nu: (layer, te[g], chunk(g, c, nu), 0))],
        out_specs=pl.BlockSpec((tmg, d), lambda g, c, te, nu: (g, 0)),
        scratch_shapes=[pltpu.VMEM((tmg, d), F32)])
    return pl.pallas_call(
        _grouped_kernel,
        out_shape=jax.ShapeDtypeStruct((s_max, d), BF),
        grid_spec=grid_spec,
        compiler_params=_cparams(("arbitrary", "arbitrary")),
        name="moe_grouped_ffn",
    )(tile_expert, n_used, xs, w_gu, w_gu, w_down)


def _combine_kernel(base_ref, c16_ref, g_ref, y_ref, f_ref, ystage, acc, sems, *, ne, tt, rc, sizes):
    i = pl.program_id(0)

    @pl.when(i == 0)
    def _():
        ystage[...] = jnp.zeros(ystage.shape, ystage.dtype)

    def seg_copy(e, b0, off, sz, j):
        return pltpu.make_async_copy(
            y_ref.at[pl.ds(pl.multiple_of(b0 + off, MOE_ALIGN), sz)],
            ystage.at[e, pl.ds(pl.multiple_of(off, MOE_ALIGN), sz)],
            sems.at[e, j])

    for e in range(ne):
        c16 = c16_ref[i * ne + e]
        b0 = base_ref[i * ne + e]
        for cond, off, sz, j in _segments(c16, sizes):
            @pl.when(cond)
            def _():
                seg_copy(e, b0, off, sz, j).start()

    ri = lax.broadcasted_iota(jnp.int32, (tt, tt), 0)
    ci = lax.broadcasted_iota(jnp.int32, (tt, tt), 1)
    eye = jnp.where(ri == ci, 1.0, 0.0).astype(BF)
    p1, p2, p3 = _split3(g_ref[...])
    gtok = _dot_nt(eye, p1) + _dot_nt(eye, p2) + _dot_nt(eye, p3)
    routed = gtok > 0.0
    before = jnp.where(ci < ri, 1.0, 0.0).astype(BF)
    rank = _dot(before, jnp.where(routed, 1.0, 0.0).astype(BF))
    rank = jnp.where(routed, rank, -1.0)
    acc[...] = jnp.zeros(acc.shape, F32)

    for e in range(ne):
        c16 = c16_ref[i * ne + e]
        b0 = base_ref[i * ne + e]
        for cond, off, sz, j in _segments(c16, sizes):
            @pl.when(cond)
            def _():
                seg_copy(e, b0, off, sz, j).wait()
        for c in range(tt // rc):
            @pl.when(c * rc < c16)
            def _():
                slot = (lax.broadcasted_iota(jnp.int32, (tt, rc), 1) + c * rc).astype(F32)
                sel = slot == rank[:, e:e + 1]
                rows = _dot(jnp.where(sel, 1.0, 0.0).astype(BF), ystage[e, c * rc:(c + 1) * rc, :])
                acc[...] += gtok[:, e:e + 1] * rows

    f_ref[...] = acc[...]


def _combine(base, c16, gates_t, y, tt):
    ne, n = gates_t.shape
    d = y.shape[1]
    sizes = _seg_sizes(tt)
    grid_spec = pltpu.PrefetchScalarGridSpec(
        num_scalar_prefetch=2,
        grid=(n // tt,),
        in_specs=[pl.BlockSpec((ne, tt), lambda i, b, c: (0, i)),
                  pl.BlockSpec(memory_space=pl.ANY)],
        out_specs=pl.BlockSpec((tt, d), lambda i, b, c: (i, 0)),
        scratch_shapes=[pltpu.VMEM((ne, tt, d), BF),
                        pltpu.VMEM((tt, d), F32),
                        pltpu.SemaphoreType.DMA((ne, len(sizes)))])
    return pl.pallas_call(
        functools.partial(_combine_kernel, ne=ne, tt=tt, rc=min(MOE_RC, tt), sizes=sizes),
        out_shape=jax.ShapeDtypeStruct((n, d), F32),
        grid_spec=grid_spec,
        compiler_params=_cparams(("arbitrary",)),
        name="moe_combine",
    )(base, c16, gates_t, y)


def _moe(gates_t, h_all, w_gu, w_down, layer):
    ne, n = gates_t.shape
    tt, tmg = MOE_TT, MOE_TMG
    nt = n // tt
    cnt = _route_counts(gates_t, tt)[:, :, 0].astype(jnp.int32)
    c16 = (cnt + (MOE_ALIGN - 1)) // MOE_ALIGN * MOE_ALIGN
    region = (jnp.sum(c16, axis=0) + (tmg - 1)) // tmg * tmg
    reg_end = jnp.cumsum(region)
    base = (reg_end - region)[None, :] + jnp.cumsum(c16, axis=0) - c16
    s_max = -(-(TOP_K * n + (MOE_ALIGN - 1) * nt * ne + tmg * ne) // tmg) * tmg
    n_used = (reg_end[-1:] // tmg).astype(jnp.int32)
    tile_start = jnp.arange(s_max // tmg, dtype=jnp.int32) * tmg
    tile_expert = jnp.minimum(jnp.sum(tile_start[:, None] >= reg_end[None, :], axis=1), ne - 1).astype(jnp.int32)
    base = base.reshape(-1).astype(jnp.int32)
    c16 = c16.reshape(-1).astype(jnp.int32)
    xs = _dispatch(base, c16, gates_t, h_all, s_max, tt)
    y = _grouped_ffn(tile_expert, n_used, xs, w_gu, w_down, layer)
    return _combine(base, c16, gates_t, y, tt)


def _ln_res_kernel(f_ref, x_ref, g_ref, sc_ref, sh_ref, lng_ref, lnb_ref, *refs, alpha, has_next):
    x2 = _post_ln(x_ref[...], f_ref[...], g_ref[...], lng_ref[...], lnb_ref[...], alpha)
    refs[0][...] = x2
    if has_next:
        refs[1][...] = (x2 * (1.0 + sc_ref[...]) + sh_ref[...]).astype(refs[1].dtype)


def _ln_residual(grp, f_all, row0, x, gate, sc, sh, ln_g, ln_b, alpha, has_next=True):
    n, d = x.shape
    tm = grp.tm
    assert row0 % tm == 0
    blk0 = row0 // tm
    row = pl.BlockSpec((tm, d), lambda m: (m, 0))
    vec = pl.BlockSpec((1, d), lambda m: (0, 0))
    out_shape = [jax.ShapeDtypeStruct((n, d), F32)]
    out_specs = [row]
    if has_next:
        out_shape.append(jax.ShapeDtypeStruct((n, d), BF))
        out_specs.append(row)
    res = pl.pallas_call(
        functools.partial(_ln_res_kernel, alpha=alpha, has_next=has_next),
        out_shape=out_shape,
        grid=(grp.nm,),
        in_specs=[pl.BlockSpec((tm, d), lambda m: (blk0 + m, 0)), row,
                  grp.mod_spec(d, 0, 1), grp.mod_spec(d, 0, 1), grp.mod_spec(d, 0, 1), vec, vec],
        out_specs=out_specs,
        compiler_params=_cparams(("arbitrary",)),
        name="moe_ln_residual",
    )(f_all, x, gate, sc, sh, ln_g.reshape(1, d), ln_b.reshape(1, d))
    return (res[0], res[1]) if has_next else (res[0], None)


def _to_time_major(a):
    a = jnp.swapaxes(a, 0, 1)
    return a.reshape((a.shape[0] * a.shape[1],) + a.shape[2:])


def _from_time_major(a, db):
    a = a.reshape((a.shape[0] // db, db) + a.shape[1:])
    return jnp.swapaxes(a, 0, 1)


def kernel(x_prompt, x_sample, cache_k, cache_v, cache_lf, state_conv_b, state_conv_c, state_h, page_table, c_prompt, c_sample, w_ada, b_ada, ln_g, ln_b, fox_w_in, fox_b_f, fox_w_out, sc_w_in, sc_w_conv, sc_w_out, lru_w_in, lru_conv_w, lru_conv_b, lru_w_a, lru_b_a, lru_w_x, lru_b_x, lru_lam, lru_w_out, ffn_w_gu, ffn_w_down, moe_w_router, moe_w_gu, moe_w_down):
    b, t, d = x_prompt.shape
    db, dt, _ = x_sample.shape
    depth = w_ada.shape[0]
    nh = fox_b_f.shape[1]
    hd = d // nh
    assert hd == FOX_HEAD_DIM and d % LANE_V7X == 0 and db % SUBLANE_V7X == 0
    alpha = (2.0 * depth) ** 0.25
    n_fox = cache_k.shape[0]
    n_pool, page = cache_k.shape[1], cache_k.shape[2]
    ns = db * dt

    gp = _Group(b * t, _pick(t, (1024, 512, 256, 128)), t, 1, b)
    gs = _Group(ns, ns, ns, db, 1)

    mods = _ada(jnp.concatenate([c_prompt, c_sample], axis=0), w_ada, b_ada)
    mods_p = mods[:, :b].reshape(depth, b, 6, d)
    mods_s = jnp.tile(mods[:, b:].reshape(depth, db, 6, d), (1, dt, 1, 1))

    def mod(layer, which, comp):
        if which == 0:
            return mods_p[layer, :, comp][:, None, :]
        return mods_s[layer, :, comp][None]

    cache_kt = jnp.transpose(cache_k, (0, 1, 3, 4, 2))
    cache_vt = jnp.transpose(cache_v, (0, 1, 3, 4, 2))
    cache_lft = jnp.transpose(cache_lf, (0, 1, 3, 2))

    groups = (gp, gs)
    xs = [x_prompt.reshape(b * t, d), _to_time_major(x_sample)]
    hm = [_modulate(groups[w], xs[w], mod(0, w, 1), mod(0, w, 0)) for w in range(2)]
    outs = [dict(k=[], v=[], lf=[], cb=[], cc=[], h=[]) for _ in range(2)]
    x1, hf, gates_t = [None, None], [None, None], [None, None]
    kt_all = jnp.zeros((n_fox, b, d, t), F32)
    vt_all = jnp.zeros((n_fox, b, d, t), F32)

    for i in range(depth):
        kind, j = i % N_MIXERS, i // N_MIXERS
        last = i == depth - 1
        is_moe = i % 2 == 1
        for w in range(2):
            grp = groups[w]
            o = outs[w]
            if kind == 0:
                (qb,) = _linear(grp, hm[w], fox_w_in, j, 0, d, [F32 if w else BF], scale=1.0 if w else Q_SCALE)
                lf = _forget(grp, hm[w], fox_w_in[j][:, 3 * d:], fox_b_f[j])
                if w == 0:
                    kt_all, ktb = _linear_t(grp, hm[w], fox_w_in[j][:, d:2 * d].T, kt_all, j, rows_out=False)
                    vt_all, vb = _linear_t(grp, hm[w], fox_w_in[j][:, 2 * d:3 * d].T, vt_all, j, rows_out=True)
                    fcum = _cumsum_time(jnp.swapaxes(lf.reshape(b, t, nh), 1, 2))
                    a_mix = _attention_prompt(qb, ktb, vb, fcum.reshape(b, nh // 2, 2, t), b, t)
                    o['lf'].append(lf.reshape(b, t, nh))
                else:
                    k32, kb = _linear(grp, hm[w], fox_w_in, j, d, d, [F32, BF])
                    v32, vb = _linear(grp, hm[w], fox_w_in, j, 2 * d, d, [F32, BF])
                    nk = 2 * SUBLANE_V7X
                    assert dt <= nk
                    k_b = _from_time_major(k32, db)
                    v_b = _from_time_major(v32, db)
                    lf_b = _from_time_major(lf, db)
                    pad = ((0, 0), (0, nk - dt), (0, 0))
                    lf_new_t = jnp.pad(jnp.swapaxes(lf_b, 1, 2), ((0, 0), (0, 0), (0, LANE_V7X - dt)))
                    att = _attention_sample(j, page_table, _from_time_major(qb, db), jnp.pad(k_b, pad),
                                            jnp.pad(v_b, pad), lf_new_t, cache_kt, cache_vt, cache_lft)
                    a_mix = _to_time_major(att).astype(BF)
                    o['k'].append(k_b.reshape(db, dt, nh, hd))
                    o['v'].append(v_b.reshape(db, dt, nh, hd))
                    o['lf'].append(lf_b)
                w_out = fox_w_out
            elif kind == 1:
                width = sc_w_conv.shape[1]
                if w == 0:
                    init = jnp.zeros((1, SUBLANE_V7X, d), F32)
                else:
                    init = _to_time_major(state_conv_b[j])[None]
                a_mix, tail = _short_conv(grp, hm[w], sc_w_in, j, sc_w_conv[j], init)
                if w == 0:
                    o['cb'].append(tail[:, SUBLANE_V7X - (width - 1):, :])
                else:
                    o['cb'].append(_from_time_major(tail[0], db))
                w_out = sc_w_out
            else:
                width = lru_conv_w.shape[1]
                if w == 0:
                    init = jnp.zeros((1, SUBLANE_V7X, d), F32)
                    h0 = jnp.zeros((1, SUBLANE_V7X, d), F32)
                else:
                    init = _to_time_major(state_conv_c[j])[None]
                    h0 = state_h[j][None]
                a_mix, tail, hl = _rglru(grp, hm[w], lru_w_in, j, lru_conv_w[j], lru_conv_b[j], lru_w_a[j],
                                         lru_b_a[j], lru_w_x[j], lru_b_x[j], lru_lam[j], init, h0)
                if w == 0:
                    o['cc'].append(tail[:, SUBLANE_V7X - (width - 1):, :])
                    o['h'].append(hl[:, 0, :])
                else:
                    o['cc'].append(_from_time_major(tail[0], db))
                    o['h'].append(hl[0])
                w_out = lru_w_out

            res = _out_ln(grp, a_mix, w_out, j, xs[w], mod(i, w, 2), mod(i, w, 4), mod(i, w, 3),
                          ln_g[i, 0], ln_b[i, 0], alpha, moe_w_router[i // 2] if is_moe else None)
            x1[w], hf[w] = res[0], res[1]
            if is_moe:
                gates_t[w] = res[2]

        nxt = [(mod(i + 1, w, 1), mod(i + 1, w, 0)) if not last else (mod(i, w, 1), mod(i, w, 0)) for w in range(2)]
        if is_moe:
            n_real = b * t + ns
            n_all = -(-n_real // MOE_TT) * MOE_TT
            h_all = jnp.concatenate([hf[0], hf[1], jnp.zeros((n_all - n_real, d), BF)], axis=0)
            g_all = jnp.concatenate([gates_t[0], gates_t[1],
                                     jnp.zeros((gates_t[0].shape[0], n_all - n_real), F32)], axis=1)
            f_all = _moe(g_all, h_all, moe_w_gu, moe_w_down, i // 2)
            for w in range(2):
                xs[w], hm[w] = _ln_residual(groups[w], f_all, w * b * t, x1[w], mod(i, w, 5), nxt[w][0], nxt[w][1],
                                            ln_g[i, 1], ln_b[i, 1], alpha, has_next=not last)
        else:
            for w in range(2):
                xs[w], hm[w] = _ffn(groups[w], hf[w], ffn_w_gu, ffn_w_down, i // 2, x1[w], mod(i, w, 5),
                                    nxt[w][0], nxt[w][1], ln_g[i, 1], ln_b[i, 1], alpha, has_next=not last)

    y_prompt = xs[0].reshape(b, t, d)
    y_sample = _from_time_major(xs[1], db)
    op, os_ = outs
    return (y_prompt, y_sample,
            jnp.transpose(kt_all.reshape(n_fox, b, nh, hd, t), (0, 1, 4, 2, 3)),
            jnp.transpose(vt_all.reshape(n_fox, b, nh, hd, t), (0, 1, 4, 2, 3)), jnp.stack(op['lf']),
            jnp.stack(op['cb']), jnp.stack(op['cc']), jnp.stack(op['h']),
            jnp.stack(os_['k']), jnp.stack(os_['v']), jnp.stack(os_['lf']),
            jnp.stack(os_['cb']), jnp.stack(os_['cc']), jnp.stack(os_['h']))
```

```python
import functools

import jax
import jax.numpy as jnp
from jax import lax
from jax.experimental import pallas as pl
from jax.experimental.pallas import tpu as pltpu

BF = jnp.bfloat16
F32 = jnp.float32

LANE_V7X = 128
SUBLANE_V7X = 8
VMEM_LIMIT_V7X = 52 * 1024 * 1024

FOX_HEAD_DIM = 64
Q_SCALE = FOX_HEAD_DIM ** -0.5
LOG2_E = 1.4426950408889634
LRU_C = 8.0
LN_EPS = 1e-5
N_MIXERS = 3
TOP_K = 2
PAGES_PER_STEP = 8
ATTN_ROW_GROUPS = 1


def _cparams(sem):
    return pltpu.CompilerParams(dimension_semantics=sem, vmem_limit_bytes=VMEM_LIMIT_V7X)


def _dot(a, b):
    return jnp.dot(a, b, preferred_element_type=F32)


def _dot_nt(a, b):
    return lax.dot_general(a, b, (((1,), (1,)), ((), ())), preferred_element_type=F32)


def _split(a):
    hi = a.astype(BF)
    lo = (a - hi.astype(F32)).astype(BF)
    return hi, lo


def _dot3(a, b):
    ah, al = _split(a)
    bh, bl = _split(b)
    return _dot(ah, bh) + (_dot(ah, bl) + _dot(al, bh))


def _pick(n, cands):
    for c in cands:
        if n % c == 0:
            return c
    return n


class _Group:
    def __init__(self, n_rows, tm, rows_per_seq, stride, n_seq):
        self.n = n_rows
        self.tm = tm
        self.nm = n_rows // tm
        self.rps = rows_per_seq
        self.g = 1 if stride == 1 else tm
        self.stride = stride
        self.tps = rows_per_seq // tm
        self.n_seq = n_seq

    def mod_spec(self, d, m_axis, n_axes):
        tm, rps = self.tm, self.rps

        def idx(*ids):
            return ((ids[m_axis] * tm) // rps, 0, 0)
        return pl.BlockSpec((None, self.g, d), idx)


def _ada_kernel(c_ref, w_ref, b_ref, o_ref):
    c = c_ref[...]
    a = c * jax.nn.sigmoid(c)
    o_ref[...] = _dot3(a, w_ref[...]) + b_ref[...]


def _ada(c_all, w_ada, b_ada):
    depth, d, d6 = w_ada.shape
    s = c_all.shape[0]
    tn = _pick(d6, (1024, 512, 256, 128))
    return pl.pallas_call(
        _ada_kernel,
        out_shape=jax.ShapeDtypeStruct((depth, s, d6), F32),
        grid=(depth, d6 // tn),
        in_specs=[pl.BlockSpec((s, d), lambda l, n: (0, 0)),
                  pl.BlockSpec((None, d, tn), lambda l, n: (l, 0, n)),
                  pl.BlockSpec((None, 1, tn), lambda l, n: (l, 0, n))],
        out_specs=pl.BlockSpec((None, s, tn), lambda l, n: (l, 0, n)),
        compiler_params=_cparams(("arbitrary", "arbitrary")),
        name="ada_mod",
    )(c_all, w_ada, b_ada.reshape(depth, 1, d6))


def _modulate_kernel(x_ref, sc_ref, sh_ref, o_ref):
    o_ref[...] = (x_ref[...] * (1.0 + sc_ref[...]) + sh_ref[...]).astype(BF)


def _modulate(grp, x, sc, sh):
    n, d = x.shape
    tm = grp.tm
    return pl.pallas_call(
        _modulate_kernel,
        out_shape=jax.ShapeDtypeStruct((n, d), BF),
        grid=(grp.nm,),
        in_specs=[pl.BlockSpec((tm, d), lambda m: (m, 0)),
                  grp.mod_spec(d, 0, 1), grp.mod_spec(d, 0, 1)],
        out_specs=pl.BlockSpec((tm, d), lambda m: (m, 0)),
        compiler_params=_cparams(("arbitrary",)),
        name="modulate",
    )(x, sc, sh)


def _linear_kernel(x_ref, w_ref, *refs, scale, n_out):
    outs, wb = refs[:n_out], refs[n_out]

    @pl.when(pl.program_id(1) == 0)
    def _():
        wb[...] = w_ref[...].astype(BF)

    y = _dot(x_ref[...], wb[...])
    if scale != 1.0:
        y = y * scale
    for o in outs:
        o[...] = y.astype(o.dtype)


def _linear(grp, x, w3, layer, col0, n_cols, out_dtypes, scale=1.0):
    n, k = x.shape
    tm = grp.tm
    tn = _pick(n_cols, (512, 256, 128))
    assert col0 % tn == 0
    cb0 = col0 // tn
    outs = [jax.ShapeDtypeStruct((n, n_cols), dt) for dt in out_dtypes]
    return pl.pallas_call(
        functools.partial(_linear_kernel, scale=scale, n_out=len(outs)),
        out_shape=outs,
        grid=(n_cols // tn, grp.nm),
        in_specs=[pl.BlockSpec((tm, k), lambda c, m: (m, 0)),
                  pl.BlockSpec((None, k, tn), lambda c, m: (layer, 0, cb0 + c))],
        out_specs=[pl.BlockSpec((tm, tn), lambda c, m: (m, c)) for _ in outs],
        scratch_shapes=[pltpu.VMEM((k, tn), BF)],
        compiler_params=_cparams(("arbitrary", "arbitrary")),
        name="linear",
    )(x, w3)


def _linear_t_kernel(x_ref, wt_ref, dst_in_ref, yt32_ref, yb_ref, wb, *, rows_out):
    del dst_in_ref

    @pl.when(pl.program_id(1) == 0)
    def _():
        wb[...] = wt_ref[...].astype(BF)

    x = x_ref[...]
    yt = _dot_nt(wb[...], x)
    yt32_ref[...] = yt
    if rows_out:
        yb_ref[...] = _dot_nt(x, wb[...]).astype(yb_ref.dtype)
    else:
        yb_ref[...] = yt.astype(yb_ref.dtype)


def _linear_t(grp, x, wt, dst, layer, rows_out):
    n, k = x.shape
    nl, nb, f, t = dst.shape
    tm = grp.tm
    tps = grp.tps
    tn = _pick(f, (512, 256, 128))
    if rows_out:
        yb_shape = jax.ShapeDtypeStruct((n, f), BF)
        yb_spec = pl.BlockSpec((tm, tn), lambda c, m: (m, c))
    else:
        yb_shape = jax.ShapeDtypeStruct((nb, f, t), BF)
        yb_spec = pl.BlockSpec((None, tn, tm), lambda c, m: (m // tps, c, m % tps))
    return pl.pallas_call(
        functools.partial(_linear_t_kernel, rows_out=rows_out),
        out_shape=[jax.ShapeDtypeStruct(dst.shape, F32), yb_shape],
        grid=(f // tn, grp.nm),
        in_specs=[pl.BlockSpec((tm, k), lambda c, m: (m, 0)),
                  pl.BlockSpec((tn, k), lambda c, m: (c, 0)),
                  pl.BlockSpec(memory_space=pl.ANY)],
        out_specs=[pl.BlockSpec((None, None, tn, tm), lambda c, m: (layer, m // tps, c, m % tps)), yb_spec],
        scratch_shapes=[pltpu.VMEM((tn, k), BF)],
        input_output_aliases={2: 0},
        compiler_params=_cparams(("arbitrary", "arbitrary")),
        name="linear_t",
    )(x, wt, dst)


def _log_sigmoid(z):
    return jnp.minimum(z, 0.0) - jnp.log1p(jnp.exp(-jnp.abs(z)))


def _forget_kernel(x_ref, w_ref, b_ref, o_ref):
    z = _dot(x_ref[...], w_ref[...].astype(BF)) + b_ref[...]
    o_ref[...] = _log_sigmoid(z)


def _forget(grp, x, w_f, b_f):
    n, k = x.shape
    h = w_f.shape[1]
    tm = grp.tm
    return pl.pallas_call(
        _forget_kernel,
        out_shape=jax.ShapeDtypeStruct((n, h), F32),
        grid=(grp.nm,),
        in_specs=[pl.BlockSpec((tm, k), lambda m: (m, 0)),
                  pl.BlockSpec((k, h), lambda m: (0, 0)),
                  pl.BlockSpec((1, h), lambda m: (0, 0))],
        out_specs=pl.BlockSpec((tm, h), lambda m: (m, 0)),
        compiler_params=_cparams(("arbitrary",)),
        name="forget_gate",
    )(x, w_f, b_f.reshape(1, h))


def _prefix_lanes(x):
    n = x.shape[-1]
    lane = lax.broadcasted_iota(jnp.int32, x.shape, x.ndim - 1)
    d = 1
    while d < n:
        x = x + jnp.where(lane >= d, pltpu.roll(x, d, x.ndim - 1), 0.0)
        d *= 2
    return x


def _cumsum_kernel(x_ref, o_ref, *, scale):
    o_ref[...] = _prefix_lanes(x_ref[...]) * scale


def _cumsum_time(lf_t, scale):
    b, h, t = lf_t.shape
    return pl.pallas_call(
        functools.partial(_cumsum_kernel, scale=scale),
        out_shape=jax.ShapeDtypeStruct((b, h, t), F32),
        grid=(b,),
        in_specs=[pl.BlockSpec((None, h, t), lambda i: (i, 0, 0))],
        out_specs=pl.BlockSpec((None, h, t), lambda i: (i, 0, 0)),
        compiler_params=_cparams(("arbitrary",)),
        name="forget_cumsum",
    )(lf_t)


def _attn_kernel(q_ref, k_ref, v_ref, f_ref, o_ref, m_sc, acc_sc, *, tq):
    qi = pl.program_id(2)
    half = LANE_V7X // 2
    lane = lax.broadcasted_iota(jnp.int32, (tq, LANE_V7X), 1)
    lo = lane < half
    q = q_ref[...]
    zero = jnp.zeros_like(q)
    qs = (jnp.where(lo, q, zero), jnp.where(lo, zero, q))
    for h in range(2):
        m_sc[h] = jnp.full((tq, LANE_V7X), -jnp.inf, F32)
        acc_sc[h] = jnp.zeros((tq, LANE_V7X), F32)

    def block(k0, masked):
        kb = k_ref[:, pl.ds(k0, tq)]
        vb = v_ref[pl.ds(k0, tq), :]
        one = jnp.ones_like(vb)
        vbs = (jnp.where(lo, vb, one), jnp.where(lo, one, vb))
        rg = tq // ATTN_ROW_GROUPS
        chains = [(h, g * rg) for h in range(2) for g in range(ATTN_ROW_GROUPS)]
        ss = []
        for h, r0 in chains:
            s = _dot(qs[h][r0:r0 + rg], kb) - f_ref[h:h + 1, pl.ds(k0, tq)]
            if masked:
                row = lax.broadcasted_iota(jnp.int32, (rg, tq), 0) + r0
                col = lax.broadcasted_iota(jnp.int32, (rg, tq), 1)
                s = jnp.where(col <= row, s, -jnp.inf)
            ss.append(s)
        for (h, r0), s in zip(chains, ss):
            m_prev = m_sc[h, r0:r0 + rg, :]
            m_new = jnp.maximum(m_prev, jnp.max(s, axis=1, keepdims=True))
            p = jnp.exp2(s - m_new[:, 0:1])
            acc_sc[h, r0:r0 + rg, :] = (jnp.exp2(m_prev - m_new) * acc_sc[h, r0:r0 + rg, :]
                                        + _dot(p.astype(BF), vbs[h]))
            m_sc[h, r0:r0 + rg, :] = m_new

    def body(i, carry):
        block(pl.multiple_of(i * tq, tq), False)
        return carry

    lax.fori_loop(0, qi, body, 0)
    block(pl.multiple_of(qi * tq, tq), True)
    a0, a1 = acc_sc[0], acc_sc[1]
    o = jnp.where(lo, a0 / pltpu.roll(a0, half, 1), a1 / pltpu.roll(a1, half, 1))
    o_ref[...] = o.astype(o_ref.dtype)


def _attention_prompt(q, kt, v, fcum, b, t):
    n, d = q.shape
    hp = d // LANE_V7X
    tq = _pick(t, (512, 256, 128))
    nq = t // tq
    return pl.pallas_call(
        functools.partial(_attn_kernel, tq=tq),
        out_shape=jax.ShapeDtypeStruct((n, d), BF),
        grid=(b, hp, nq),
        in_specs=[pl.BlockSpec((tq, LANE_V7X), lambda i, j, qi: (i * nq + qi, j)),
                  pl.BlockSpec((None, LANE_V7X, t), lambda i, j, qi: (i, j, 0)),
                  pl.BlockSpec((t, LANE_V7X), lambda i, j, qi: (i, j)),
                  pl.BlockSpec((None, None, 2, t), lambda i, j, qi: (i, j, 0, 0))],
        out_specs=pl.BlockSpec((tq, LANE_V7X), lambda i, j, qi: (i * nq + qi, j)),
        scratch_shapes=[pltpu.VMEM((2, tq, LANE_V7X), F32)] * 2,
        compiler_params=_cparams(("arbitrary", "arbitrary", "arbitrary")),
        name="fox_attention_prompt",
    )(q, kt, v, fcum)


def _split3(x):
    hi = x.astype(BF)
    r1 = x - hi.astype(F32)
    mid = r1.astype(BF)
    lo = (r1 - mid.astype(F32)).astype(BF)
    return hi, mid, lo


def _decode_kernel(pt_ref, q_ref, kn_ref, vn_ref, lfn_ref, *refs, pp, nh, nk):
    k_refs = refs[:pp]
    v_refs = refs[pp:2 * pp]
    lf_refs = refs[2 * pp:3 * pp]
    o_ref, qbd, m_sc, l_sc, acc_sc, c_sc = refs[3 * pp:]
    g = pl.program_id(1)
    dt, d = q_ref.shape
    hd = d // nh
    r = dt * nh
    own = (lax.rem(lax.broadcasted_iota(jnp.int32, (r, d), 0), nh)
           == lax.div(lax.broadcasted_iota(jnp.int32, (r, d), 1), hd))

    @pl.when(g == 0)
    def _():
        q = q_ref[...] * Q_SCALE
        qrep = jnp.concatenate([jnp.broadcast_to(q[t:t + 1], (nh, d)) for t in range(dt)], axis=0)
        qbd[...] = jnp.where(own, qrep, 0.0).astype(BF)
        m_sc[...] = jnp.full(m_sc.shape, -jnp.inf, F32)
        l_sc[...] = jnp.zeros(l_sc.shape, F32)
        acc_sc[...] = jnp.zeros(acc_sc.shape, F32)
        c_sc[...] = jnp.zeros(c_sc.shape, F32)

    def softmax_step(s):
        m_prev = m_sc[...]
        m_new = jnp.maximum(m_prev, jnp.max(s, axis=1, keepdims=True))
        alpha = jnp.exp(m_prev - m_new)
        p = jnp.exp(s - m_new[:, 0:1])
        l_sc[...] = alpha * l_sc[...] + jnp.sum(p, axis=1, keepdims=True)
        m_sc[...] = m_new
        return alpha[:, 0:1], p.astype(BF)

    def d_by_keys(ref):
        x = ref[...]
        return x.reshape(x.shape[0] * x.shape[1], x.shape[2]).astype(BF)

    qb = qbd[...]
    lf_all = jnp.concatenate([lf_refs[i][...] for i in range(pp)], axis=1)
    cum = _prefix_lanes(lf_all) + c_sc[:, 0:1]
    c_sc[...] = c_sc[...] + jnp.sum(lf_all, axis=1, keepdims=True)
    kt = jnp.concatenate([d_by_keys(k_refs[i]) for i in range(pp)], axis=1)
    vt = jnp.concatenate([d_by_keys(v_refs[i]) for i in range(pp)], axis=1)
    alpha, p = softmax_step(_dot(qb, kt) - jnp.concatenate([cum] * dt, axis=0))
    acc_sc[...] = alpha * acc_sc[...] + _dot_nt(p, vt)

    @pl.when(g == pl.num_programs(1) - 1)
    def _():
        cum_n = (_prefix_lanes(lfn_ref[...]) + c_sc[:, 0:1])[:, 0:nk]
        sn = _dot_nt(qb, kn_ref[...].astype(BF)) - jnp.concatenate([cum_n] * dt, axis=0)
        trow = lax.div(lax.broadcasted_iota(jnp.int32, (r, nk), 0), nh)
        col = lax.broadcasted_iota(jnp.int32, (r, nk), 1)
        alpha_n, pn = softmax_step(jnp.where(col <= trow, sn, -jnp.inf))
        acc = alpha_n * acc_sc[...] + _dot(pn, vn_ref[...].astype(BF))
        o = jnp.where(own, acc / l_sc[:, 0:1], 0.0)
        o_ref[...] = jnp.sum(o.reshape(dt, nh, d), axis=1)


def _attention_sample(layer, page_table, q_b, k_new, v_new, lf_new_t, cache_kt, cache_vt, cache_lft):
    db, dt, d = q_b.shape
    nk = k_new.shape[1]
    nh, hd, page = cache_kt.shape[2:]
    n_pages = page_table.shape[1]
    pp = _pick(n_pages, (PAGES_PER_STEP, 4, 2, 1))
    r = dt * nh

    def page_spec(shape, i):
        zeros = (0,) * len(shape)
        return pl.BlockSpec((None, None) + shape, lambda b, g, pt: (layer, pt[b, g * pp + i]) + zeros)

    in_specs = [pl.BlockSpec((None, dt, d), lambda b, g, pt: (b, 0, 0)),
                pl.BlockSpec((None, nk, d), lambda b, g, pt: (b, 0, 0)),
                pl.BlockSpec((None, nk, d), lambda b, g, pt: (b, 0, 0)),
                pl.BlockSpec((None, nh, LANE_V7X), lambda b, g, pt: (b, 0, 0))]
    in_specs += [page_spec((nh, hd, page), i) for i in range(pp)]
    in_specs += [page_spec((nh, hd, page), i) for i in range(pp)]
    in_specs += [page_spec((nh, page), i) for i in range(pp)]
    grid_spec = pltpu.PrefetchScalarGridSpec(
        num_scalar_prefetch=1,
        grid=(db, n_pages // pp),
        in_specs=in_specs,
        out_specs=pl.BlockSpec((None, dt, d), lambda b, g, pt: (b, 0, 0)),
        scratch_shapes=[pltpu.VMEM((r, d), BF),
                        pltpu.VMEM((r, LANE_V7X), F32),
                        pltpu.VMEM((r, LANE_V7X), F32),
                        pltpu.VMEM((r, d), F32),
                        pltpu.VMEM((nh, LANE_V7X), F32)])
    return pl.pallas_call(
        functools.partial(_decode_kernel, pp=pp, nh=nh, nk=nk),
        out_shape=jax.ShapeDtypeStruct((db, dt, d), F32),
        grid_spec=grid_spec,
        compiler_params=_cparams(("arbitrary", "arbitrary")),
        name="fox_attention_sample",
    )(page_table, q_b, k_new, v_new, lf_new_t,
      *([cache_kt] * pp), *([cache_vt] * pp), *([cache_lft] * pp))


def _conv_tile(e_ref, carry_ref, init_ref, tail_ref, u, w_ref, first, *, hb, stride, width):
    tm = u.shape[0]

    @pl.when(first)
    def _():
        e_ref[0:hb, :] = init_ref[...]

    @pl.when(jnp.logical_not(first))
    def _():
        e_ref[0:hb, :] = carry_ref[...]

    e_ref[hb:hb + tm, :] = u
    y = None
    for j in range(width):
        off = hb - (width - 1 - j) * stride
        term = w_ref[j:j + 1, :] * e_ref[off:off + tm, :]
        y = term if y is None else y + term
    tail = e_ref[tm:tm + hb, :]
    carry_ref[...] = tail
    tail_ref[...] = tail
    return y


def _sc_kernel(x_ref, wb_ref, wc_ref, wv_ref, wconv_ref, init_ref, mix_ref, tail_ref,
               wb, e_sc, carry, *, tps, hb, stride, width):
    m = pl.program_id(1)

    @pl.when(m == 0)
    def _():
        wb[0] = wb_ref[...].astype(BF)
        wb[1] = wc_ref[...].astype(BF)
        wb[2] = wv_ref[...].astype(BF)

    x = x_ref[...]
    g_b = _dot(x, wb[0])
    u = _dot(x, wb[1]) * _dot(x, wb[2])
    y = _conv_tile(e_sc, carry, init_ref, tail_ref, u, wconv_ref, (m % tps) == 0,
                   hb=hb, stride=stride, width=width)
    mix_ref[...] = (g_b * y).astype(mix_ref.dtype)


def _short_conv(grp, x, w_in, layer, w_conv, init):
    n, d = x.shape
    tm = grp.tm
    width = w_conv.shape[0]
    hb = init.shape[1]
    tn = _pick(d, (256, 128))
    nb = d // tn
    tps = grp.tps
    n_init = init.shape[0]

    def wspec(part):
        return pl.BlockSpec((None, d, tn), lambda c, m: (layer, 0, part * nb + c))

    def seq_of(m):
        return m // tps

    return pl.pallas_call(
        functools.partial(_sc_kernel, tps=tps, hb=hb, stride=grp.stride, width=width),
        out_shape=[jax.ShapeDtypeStruct((n, d), BF),
                   jax.ShapeDtypeStruct((grp.n_seq, hb, d), F32)],
        grid=(nb, grp.nm),
        in_specs=[pl.BlockSpec((tm, d), lambda c, m: (m, 0)),
                  wspec(0), wspec(1), wspec(2),
                  pl.BlockSpec((width, tn), lambda c, m: (0, c)),
                  pl.BlockSpec((None, hb, tn), lambda c, m: (seq_of(m) % n_init, 0, c))],
        out_specs=[pl.BlockSpec((tm, tn), lambda c, m: (m, c)),
                   pl.BlockSpec((None, hb, tn), lambda c, m: (seq_of(m), 0, c))],
        scratch_shapes=[pltpu.VMEM((3, d, tn), BF),
                        pltpu.VMEM((hb + tm, tn), F32),
                        pltpu.VMEM((hb, tn), F32)],
        compiler_params=_cparams(("arbitrary", "arbitrary")),
        name="short_conv_mixer",
    )(x, w_in, w_in, w_in, w_conv, init)


def _softplus(x):
    return jnp.maximum(x, 0.0) + jnp.log1p(jnp.exp(-jnp.abs(x)))


def _neg_expm1(x):
    th = jnp.tanh(0.5 * x)
    return -2.0 * th / (1.0 - th)


def _scan_rows(a, u):
    tm = a.shape[0]
    row = lax.broadcasted_iota(jnp.int32, a.shape, 0)
    d = 1
    while d < tm:
        keep = row >= d
        a_s = jnp.where(keep, pltpu.roll(a, d, 0), 1.0)
        u_s = jnp.where(keep, pltpu.roll(u, d, 0), 0.0)
        u = u + a * u_s
        a = a * a_s
        d *= 2
    return a, u


def _lru_kernel(x_ref, wg_ref, wr_ref, cw_ref, cb_ref, bda_ref, bdx_ref, ba_ref, bx_ref, lam_ref,
                init_ref, h0_ref, mix_ref, tail_ref, hlast_ref,
                wb, bdb, e_sc, carry, hcarry, *, tps, hb, stride, width):
    m = pl.program_id(1)
    tm = x_ref.shape[0]

    @pl.when(m == 0)
    def _():
        wb[0] = wg_ref[...].astype(BF)
        wb[1] = wr_ref[...].astype(BF)
        bdb[0] = bda_ref[...].astype(BF)
        bdb[1] = bdx_ref[...].astype(BF)

    first = (m % tps) == 0
    x = x_ref[...]
    gate = jax.nn.gelu(_dot(x, wb[0]), approximate=True)
    xr = _dot(x, wb[1])
    xc = _conv_tile(e_sc, carry, init_ref, tail_ref, xr, cw_ref, first,
                    hb=hb, stride=stride, width=width) + cb_ref[...]
    xcb = xc.astype(BF)
    r = jax.nn.sigmoid(_dot(xcb, bdb[0]) + ba_ref[...])
    i = jax.nn.sigmoid(_dot(xcb, bdb[1]) + bx_ref[...])
    log_a = (-LRU_C) * r * _softplus(-lam_ref[...])
    a = jnp.exp(log_a)
    u = jnp.sqrt(_neg_expm1(2.0 * log_a)) * i * xc

    @pl.when(first)
    def _():
        hcarry[...] = h0_ref[...]

    if stride == 1:
        a_cum, h_loc = _scan_rows(a, u)
        hs = h_loc + a_cum * hcarry[0:1, :]
        mix_ref[...] = (gate * hs).astype(mix_ref.dtype)
        h_end = jnp.broadcast_to(hs[tm - 1:tm, :], hcarry.shape)
    else:
        h = hcarry[...]
        for t in range(tm // stride):
            sl = slice(t * stride, (t + 1) * stride)
            h = a[sl] * h + u[sl]
            mix_ref[sl, :] = (gate[sl] * h).astype(mix_ref.dtype)
        h_end = h
    hcarry[...] = h_end
    hlast_ref[...] = h_end


def _block_diag(w, per):
    nblk, bd, _ = w.shape
    eye = jnp.eye(per, dtype=w.dtype)
    return jnp.einsum('cide,ik->cidke', w.reshape(nblk // per, per, bd, bd), eye).reshape(
        nblk // per, per * bd, per * bd)


def _rglru(grp, x, w_in, layer, conv_w, conv_b, w_a, b_a, w_x, b_x, lam, init, h0):
    n, d = x.shape
    rw = conv_w.shape[1]
    tm = grp.tm
    width = conv_w.shape[0]
    hb = init.shape[1]
    hr = h0.shape[1]
    bd = w_a.shape[1]
    tn = _pick(rw, (256, 128))
    nb = rw // tn
    tps = grp.tps
    n_init = init.shape[0]
    bda = _block_diag(w_a, tn // bd)
    bdx = _block_diag(w_x, tn // bd)

    def row_spec():
        return pl.BlockSpec((1, tn), lambda c, m: (0, c))

    def seq_of(m):
        return m // tps

    return pl.pallas_call(
        functools.partial(_lru_kernel, tps=tps, hb=hb, stride=grp.stride, width=width),
        out_shape=[jax.ShapeDtypeStruct((n, rw), BF),
                   jax.ShapeDtypeStruct((grp.n_seq, hb, rw), F32),
                   jax.ShapeDtypeStruct((grp.n_seq, hr, rw), F32)],
        grid=(nb, grp.nm),
        in_specs=[pl.BlockSpec((tm, d), lambda c, m: (m, 0)),
                  pl.BlockSpec((None, d, tn), lambda c, m: (layer, 0, c)),
                  pl.BlockSpec((None, d, tn), lambda c, m: (layer, 0, nb + c)),
                  pl.BlockSpec((width, tn), lambda c, m: (0, c)),
                  row_spec(),
                  pl.BlockSpec((None, tn, tn), lambda c, m: (c, 0, 0)),
                  pl.BlockSpec((None, tn, tn), lambda c, m: (c, 0, 0)),
                  row_spec(), row_spec(), row_spec(),
                  pl.BlockSpec((None, hb, tn), lambda c, m: (seq_of(m) % n_init, 0, c)),
                  pl.BlockSpec((None, hr, tn), lambda c, m: (seq_of(m) % n_init, 0, c))],
        out_specs=[pl.BlockSpec((tm, tn), lambda c, m: (m, c)),
                   pl.BlockSpec((None, hb, tn), lambda c, m: (seq_of(m), 0, c)),
                   pl.BlockSpec((None, hr, tn), lambda c, m: (seq_of(m), 0, c))],
        scratch_shapes=[pltpu.VMEM((2, d, tn), BF),
                        pltpu.VMEM((2, tn, tn), BF),
                        pltpu.VMEM((hb + tm, tn), F32),
                        pltpu.VMEM((hb, tn), F32),
                        pltpu.VMEM((hr, tn), F32)],
        compiler_params=_cparams(("arbitrary", "arbitrary")),
        name="rglru_mixer",
    )(x, w_in, w_in, conv_w, conv_b.reshape(1, rw), bda, bdx, b_a.reshape(1, rw),
      b_x.reshape(1, rw), lam.reshape(1, rw), init, h0)


def _post_ln(x, f, gate, ln_g, ln_b, alpha):
    z = alpha * x + (1.0 + gate) * f
    mu = jnp.mean(z, axis=-1, keepdims=True)
    zc = z - mu
    var = jnp.mean(zc * zc, axis=-1, keepdims=True)
    return zc * lax.rsqrt(var + LN_EPS) * ln_g + ln_b


def _top2_gates_t(logits_t):
    ne = float(logits_t.shape[0])
    idx = lax.broadcasted_iota(jnp.int32, logits_t.shape, 0).astype(F32)
    m1 = jnp.max(logits_t, axis=0, keepdims=True)
    i1 = jnp.min(jnp.where(logits_t == m1, idx, ne), axis=0, keepdims=True)
    sel1 = idx == i1
    rest = jnp.where(sel1, -jnp.inf, logits_t)
    m2 = jnp.max(rest, axis=0, keepdims=True)
    i2 = jnp.min(jnp.where(rest == m2, idx, ne), axis=0, keepdims=True)
    sel2 = idx == i2
    e2 = jnp.exp(m2 - m1)
    w1 = 1.0 / (1.0 + e2)
    w2 = e2 / (1.0 + e2)
    return jnp.where(sel1, w1, 0.0) + jnp.where(sel2, w2, 0.0)


def _out_ln_kernel(a_ref, w_ref, x_ref, g_ref, sc_ref, sh_ref, lng_ref, lnb_ref, *refs,
                   alpha, has_router):
    if has_router:
        wr_ref, x1_ref, hf_ref, gates_ref, wb = refs
    else:
        x1_ref, hf_ref, wb = refs

    @pl.when(pl.program_id(0) == 0)
    def _():
        wb[...] = w_ref[...].astype(BF)

    y = _dot(a_ref[...], wb[...])
    x1 = _post_ln(x_ref[...], y, g_ref[...], lng_ref[...], lnb_ref[...], alpha)
    x1_ref[...] = x1
    hf = x1 * (1.0 + sc_ref[...]) + sh_ref[...]
    hf_ref[...] = hf.astype(hf_ref.dtype)
    if has_router:
        wh, wl = _split(wr_ref[...])
        hh, hl = _split(hf)
        logits_t = _dot_nt(wh, hh) + (_dot_nt(wh, hl) + _dot_nt(wl, hh))
        gates_ref[...] = _top2_gates_t(logits_t)


def _out_ln(grp, a, w3, layer, x, gate, sc, sh, ln_g, ln_b, alpha, w_router=None):
    n, d = x.shape
    k = a.shape[1]
    tm = grp.tm
    has_router = w_router is not None
    in_specs = [pl.BlockSpec((tm, k), lambda m: (m, 0)),
                pl.BlockSpec((None, k, d), lambda m: (layer, 0, 0)),
                pl.BlockSpec((tm, d), lambda m: (m, 0)),
                grp.mod_spec(d, 0, 1), grp.mod_spec(d, 0, 1), grp.mod_spec(d, 0, 1),
                pl.BlockSpec((1, d), lambda m: (0, 0)),
                pl.BlockSpec((1, d), lambda m: (0, 0))]
    args = [a, w3, x, gate, sc, sh, ln_g.reshape(1, d), ln_b.reshape(1, d)]
    out_shape = [jax.ShapeDtypeStruct((n, d), F32), jax.ShapeDtypeStruct((n, d), BF)]
    out_specs = [pl.BlockSpec((tm, d), lambda m: (m, 0)), pl.BlockSpec((tm, d), lambda m: (m, 0))]
    if has_router:
        ne = w_router.shape[1]
        in_specs.append(pl.BlockSpec((ne, d), lambda m: (0, 0)))
        args.append(w_router.T)
        out_shape.append(jax.ShapeDtypeStruct((ne, n), F32))
        out_specs.append(pl.BlockSpec((ne, tm), lambda m: (0, m)))
    return pl.pallas_call(
        functools.partial(_out_ln_kernel, alpha=alpha, has_router=has_router),
        out_shape=out_shape,
        grid=(grp.nm,),
        in_specs=in_specs,
        out_specs=out_specs,
        scratch_shapes=[pltpu.VMEM((k, d), BF)],
        compiler_params=_cparams(("arbitrary",)),
        name="out_proj_ln",
    )(*args)


def _swiglu_part(h, wg, wu, wd):
    gu = _dot(h, wg.astype(BF))
    up = _dot(h, wu.astype(BF))
    act = (gu * jax.nn.sigmoid(gu) * up).astype(BF)
    return _dot(act, wd.astype(BF))


def _ffn_kernel(h_ref, wg_ref, wu_ref, wd_ref, x_ref, g_ref, sc_ref, sh_ref, lng_ref, lnb_ref, *refs,
                alpha, has_next):
    if has_next:
        x2_ref, hn_ref, acc = refs
    else:
        x2_ref, acc = refs
    ch = pl.program_id(1)
    part = _swiglu_part(h_ref[...], wg_ref[...], wu_ref[...], wd_ref[...])

    @pl.when(ch == 0)
    def _():
        acc[...] = part

    @pl.when(ch > 0)
    def _():
        acc[...] += part

    @pl.when(ch == pl.num_programs(1) - 1)
    def _():
        x2 = _post_ln(x_ref[...], acc[...], g_ref[...], lng_ref[...], lnb_ref[...], alpha)
        x2_ref[...] = x2
        if has_next:
            hn_ref[...] = (x2 * (1.0 + sc_ref[...]) + sh_ref[...]).astype(hn_ref.dtype)


def _ffn(grp, h, w_gu, w_down, layer, x, gate, sc, sh, ln_g, ln_b, alpha, has_next=True):
    n, d = x.shape
    tm = grp.tm
    ff = w_down.shape[1]
    tf = _pick(ff, (256, 128))
    nch = ff // tf
    row = pl.BlockSpec((tm, d), lambda m, c: (m, 0))
    vec = pl.BlockSpec((1, d), lambda m, c: (0, 0))
    in_specs = [row,
                pl.BlockSpec((None, d, tf), lambda m, c: (layer, 0, c)),
                pl.BlockSpec((None, d, tf), lambda m, c: (layer, 0, nch + c)),
                pl.BlockSpec((None, tf, d), lambda m, c: (layer, c, 0)),
                row, grp.mod_spec(d, 0, 2), grp.mod_spec(d, 0, 2), grp.mod_spec(d, 0, 2), vec, vec]
    out_shape = [jax.ShapeDtypeStruct((n, d), F32)]
    out_specs = [row]
    if has_next:
        out_shape.append(jax.ShapeDtypeStruct((n, d), BF))
        out_specs.append(row)
    res = pl.pallas_call(
        functools.partial(_ffn_kernel, alpha=alpha, has_next=has_next),
        out_shape=out_shape,
        grid=(grp.nm, nch),
        in_specs=in_specs,
        out_specs=out_specs,
        scratch_shapes=[pltpu.VMEM((tm, d), F32)],
        compiler_params=_cparams(("arbitrary", "arbitrary")),
        name="ffn_dense",
    )(h, w_gu, w_gu, w_down, x, gate, sc, sh, ln_g.reshape(1, d), ln_b.reshape(1, d))
    return (res[0], res[1]) if has_next else (res[0], None)


MOE_TT = 512
MOE_RC = 128
MOE_TMG = 1024
MOE_ALIGN = 16


def _seg_sizes(tt):
    sizes = []
    sz = tt
    while sz >= MOE_ALIGN:
        sizes.append(sz)
        sz //= 2
    return tuple(sizes)


def _segments(c16, sizes):
    off = jnp.int32(0)
    for j, sz in enumerate(sizes):
        cond = (c16 & sz) != 0
        yield cond, off, sz, j
        off = off + jnp.where(cond, sz, 0)


def _route_count_kernel(g_ref, o_ref):
    routed = jnp.where(g_ref[...] > 0.0, 1.0, 0.0)
    o_ref[...] = jnp.broadcast_to(jnp.sum(routed, axis=1, keepdims=True), o_ref.shape)


def _route_counts(gates_t, tt):
    ne, n = gates_t.shape
    nt = n // tt
    return pl.pallas_call(
        _route_count_kernel,
        out_shape=jax.ShapeDtypeStruct((nt, ne, LANE_V7X), F32),
        grid=(nt,),
        in_specs=[pl.BlockSpec((ne, tt), lambda i: (0, i))],
        out_specs=pl.BlockSpec((None, ne, LANE_V7X), lambda i: (i, 0, 0)),
        compiler_params=_cparams(("arbitrary",)),
        name="moe_route_counts",
    )(gates_t)


def _dispatch_kernel(base_ref, c16_ref, g_ref, h_ref, sorted_in_ref, sorted_ref, stage, sems,
                     *, ne, tt, rc, sizes):
    del sorted_in_ref
    i = pl.program_id(0)
    gt = g_ref[...]
    routed = gt > 0.0
    ri = lax.broadcasted_iota(jnp.int32, (tt, tt), 0)
    ci = lax.broadcasted_iota(jnp.int32, (tt, tt), 1)
    before = jnp.where(ri < ci, 1.0, 0.0).astype(BF)
    rank = _dot(jnp.where(routed, 1.0, 0.0).astype(BF), before)
    rank = jnp.where(routed, rank, -1.0)
    h = h_ref[...]

    def seg_copy(e, b0, off, sz, j):
        return pltpu.make_async_copy(
            stage.at[e, pl.ds(pl.multiple_of(off, MOE_ALIGN), sz)],
            sorted_ref.at[pl.ds(pl.multiple_of(b0 + off, MOE_ALIGN), sz)],
            sems.at[e, j])

    for e in range(ne):
        c16 = c16_ref[i * ne + e]
        b0 = base_ref[i * ne + e]
        for c in range(tt // rc):
            @pl.when(c * rc < c16)
            def _():
                slot = (lax.broadcasted_iota(jnp.int32, (rc, tt), 0) + c * rc).astype(F32)
                sel = slot == rank[e:e + 1, :]
                stage[e, c * rc:(c + 1) * rc, :] = _dot(jnp.where(sel, 1.0, 0.0).astype(BF), h).astype(BF)
        for cond, off, sz, j in _segments(c16, sizes):
            @pl.when(cond)
            def _():
                seg_copy(e, b0, off, sz, j).start()

    for e in range(ne):
        c16 = c16_ref[i * ne + e]
        b0 = base_ref[i * ne + e]
        for cond, off, sz, j in _segments(c16, sizes):
            @pl.when(cond)
            def _():
                seg_copy(e, b0, off, sz, j).wait()


def _dispatch(base, c16, gates_t, h_all, s_max, tt):
    ne, n = gates_t.shape
    d = h_all.shape[1]
    sizes = _seg_sizes(tt)
    grid_spec = pltpu.PrefetchScalarGridSpec(
        num_scalar_prefetch=2,
        grid=(n // tt,),
        in_specs=[pl.BlockSpec((ne, tt), lambda i, b, c: (0, i)),
                  pl.BlockSpec((tt, d), lambda i, b, c: (i, 0)),
                  pl.BlockSpec(memory_space=pl.ANY)],
        out_specs=pl.BlockSpec(memory_space=pl.ANY),
        scratch_shapes=[pltpu.VMEM((ne, tt, d), BF),
                        pltpu.SemaphoreType.DMA((ne, len(sizes)))])
    return pl.pallas_call(
        functools.partial(_dispatch_kernel, ne=ne, tt=tt, rc=min(MOE_RC, tt), sizes=sizes),
        out_shape=jax.ShapeDtypeStruct((s_max, d), BF),
        grid_spec=grid_spec,
        input_output_aliases={4: 0},
        compiler_params=_cparams(("arbitrary",)),
        name="moe_dispatch",
    )(base, c16, gates_t, h_all, jnp.zeros((s_max, d), BF))


def _grouped_kernel(te_ref, nu_ref, x_ref, wg_ref, wu_ref, wd_ref, y_ref, acc):
    del te_ref
    g = pl.program_id(0)
    ch = pl.program_id(1)
    last = pl.num_programs(1) - 1

    @pl.when(g < nu_ref[0])
    def _():
        part = _swiglu_part(x_ref[...], wg_ref[...], wu_ref[...], wd_ref[...])

        @pl.when(ch == 0)
        def _():
            acc[...] = part

        @pl.when(ch > 0)
        def _():
            acc[...] += part

        @pl.when(ch == last)
        def _():
            y_ref[...] = acc[...].astype(y_ref.dtype)

    @pl.when(jnp.logical_and(g >= nu_ref[0], ch == last))
    def _():
        y_ref[...] = jnp.zeros(y_ref.shape, y_ref.dtype)


def _grouped_ffn(tile_expert, n_used, xs, w_gu, w_down, layer):
    s_max, d = xs.shape
    ff = w_down.shape[2]
    tmg = MOE_TMG
    tf = _pick(ff, (512, 256, 128))
    nch = ff // tf

    def row(g, c, te, nu):
        return (jnp.minimum(g, nu[0] - 1), 0)

    def chunk(g, c, nu):
        return jnp.where(g < nu[0], c, nch - 1)

    grid_spec = pltpu.PrefetchScalarGridSpec(
        num_scalar_prefetch=2,
        grid=(s_max // tmg, nch),
        in_specs=[pl.BlockSpec((tmg, d), row),
                  pl.BlockSpec((None, None, d, tf), lambda g, c, te, nu: (layer, te[g], 0, chunk(g, c, nu))),
                  pl.BlockSpec((None, None, d, tf), lambda g, c, te, nu: (layer, te[g], 0, nch + chunk(g, c, nu))),
                  pl.BlockSpec((None, None, tf, d), lambda g, c, te, nu: (layer, te[g], chunk(g, c, nu), 0))],
        out_specs=pl.BlockSpec((tmg, d), lambda g, c, te, nu: (g, 0)),
        scratch_shapes=[pltpu.VMEM((tmg, d), F32)])
    return pl.pallas_call(
        _grouped_kernel,
        out_shape=jax.ShapeDtypeStruct((s_max, d), BF),
        grid_spec=grid_spec,
        compiler_params=_cparams(("arbitrary", "arbitrary")),
        name="moe_grouped_ffn",
    )(tile_expert, n_used, xs, w_gu, w_gu, w_down)


def _combine_kernel(base_ref, c16_ref, g_ref, y_ref, f_ref, ystage, acc, sems, *, ne, tt, rc, sizes):
    i = pl.program_id(0)

    @pl.when(i == 0)
    def _():
        ystage[...] = jnp.zeros(ystage.shape, ystage.dtype)

    def seg_copy(e, b0, off, sz, j):
        return pltpu.make_async_copy(
            y_ref.at[pl.ds(pl.multiple_of(b0 + off, MOE_ALIGN), sz)],
            ystage.at[e, pl.ds(pl.multiple_of(off, MOE_ALIGN), sz)],
            sems.at[e, j])

    for e in range(ne):
        c16 = c16_ref[i * ne + e]
        b0 = base_ref[i * ne + e]
        for cond, off, sz, j in _segments(c16, sizes):
            @pl.when(cond)
            def _():
                seg_copy(e, b0, off, sz, j).start()

    ri = lax.broadcasted_iota(jnp.int32, (tt, tt), 0)
    ci = lax.broadcasted_iota(jnp.int32, (tt, tt), 1)
    eye = jnp.where(ri == ci, 1.0, 0.0).astype(BF)
    p1, p2, p3 = _split3(g_ref[...])
    gtok = _dot_nt(eye, p1) + _dot_nt(eye, p2) + _dot_nt(eye, p3)
    routed = gtok > 0.0
    before = jnp.where(ci < ri, 1.0, 0.0).astype(BF)
    rank = _dot(before, jnp.where(routed, 1.0, 0.0).astype(BF))
    rank = jnp.where(routed, rank, -1.0)
    acc[...] = jnp.zeros(acc.shape, F32)

    for e in range(ne):
        c16 = c16_ref[i * ne + e]
        b0 = base_ref[i * ne + e]
        for cond, off, sz, j in _segments(c16, sizes):
            @pl.when(cond)
            def _():
                seg_copy(e, b0, off, sz, j).wait()
        for c in range(tt // rc):
            @pl.when(c * rc < c16)
            def _():
                slot = (lax.broadcasted_iota(jnp.int32, (tt, rc), 1) + c * rc).astype(F32)
                sel = slot == rank[:, e:e + 1]
                rows = _dot(jnp.where(sel, 1.0, 0.0).astype(BF), ystage[e, c * rc:(c + 1) * rc, :])
                acc[...] += gtok[:, e:e + 1] * rows

    f_ref[...] = acc[...]


def _combine(base, c16, gates_t, y, tt):
    ne, n = gates_t.shape
    d = y.shape[1]
    sizes = _seg_sizes(tt)
    grid_spec = pltpu.PrefetchScalarGridSpec(
        num_scalar_prefetch=2,
        grid=(n // tt,),
        in_specs=[pl.BlockSpec((ne, tt), lambda i, b, c: (0, i)),
                  pl.BlockSpec(memory_space=pl.ANY)],
        out_specs=pl.BlockSpec((tt, d), lambda i, b, c: (i, 0)),
        scratch_shapes=[pltpu.VMEM((ne, tt, d), BF),
                        pltpu.VMEM((tt, d), F32),
                        pltpu.SemaphoreType.DMA((ne, len(sizes)))])
    return pl.pallas_call(
        functools.partial(_combine_kernel, ne=ne, tt=tt, rc=min(MOE_RC, tt), sizes=sizes),
        out_shape=jax.ShapeDtypeStruct((n, d), F32),
        grid_spec=grid_spec,
        compiler_params=_cparams(("arbitrary",)),
        name="moe_combine",
    )(base, c16, gates_t, y)


def _moe(gates_t, h_all, w_gu, w_down, layer):
    ne, n = gates_t.shape
    tt, tmg = MOE_TT, MOE_TMG
    nt = n // tt
    cnt = _route_counts(gates_t, tt)[:, :, 0].astype(jnp.int32)
    c16 = (cnt + (MOE_ALIGN - 1)) // MOE_ALIGN * MOE_ALIGN
    region = (jnp.sum(c16, axis=0) + (tmg - 1)) // tmg * tmg
    reg_end = jnp.cumsum(region)
    base = (reg_end - region)[None, :] + jnp.cumsum(c16, axis=0) - c16
    s_max = -(-(TOP_K * n + (MOE_ALIGN - 1) * nt * ne + tmg * ne) // tmg) * tmg
    n_used = (reg_end[-1:] // tmg).astype(jnp.int32)
    tile_start = jnp.arange(s_max // tmg, dtype=jnp.int32) * tmg
    tile_expert = jnp.minimum(jnp.sum(tile_start[:, None] >= reg_end[None, :], axis=1), ne - 1).astype(jnp.int32)
    base = base.reshape(-1).astype(jnp.int32)
    c16 = c16.reshape(-1).astype(jnp.int32)
    xs = _dispatch(base, c16, gates_t, h_all, s_max, tt)
    y = _grouped_ffn(tile_expert, n_used, xs, w_gu, w_down, layer)
    return _combine(base, c16, gates_t, y, tt)


def _ln_res_kernel(f_ref, x_ref, g_ref, sc_ref, sh_ref, lng_ref, lnb_ref, *refs, alpha, has_next):
    x2 = _post_ln(x_ref[...], f_ref[...], g_ref[...], lng_ref[...], lnb_ref[...], alpha)
    refs[0][...] = x2
    if has_next:
        refs[1][...] = (x2 * (1.0 + sc_ref[...]) + sh_ref[...]).astype(refs[1].dtype)


def _ln_residual(grp, f_all, row0, x, gate, sc, sh, ln_g, ln_b, alpha, has_next=True):
    n, d = x.shape
    tm = grp.tm
    assert row0 % tm == 0
    blk0 = row0 // tm
    row = pl.BlockSpec((tm, d), lambda m: (m, 0))
    vec = pl.BlockSpec((1, d), lambda m: (0, 0))
    out_shape = [jax.ShapeDtypeStruct((n, d), F32)]
    out_specs = [row]
    if has_next:
        out_shape.append(jax.ShapeDtypeStruct((n, d), BF))
        out_specs.append(row)
    res = pl.pallas_call(
        functools.partial(_ln_res_kernel, alpha=alpha, has_next=has_next),
        out_shape=out_shape,
        grid=(grp.nm,),
        in_specs=[pl.BlockSpec((tm, d), lambda m: (blk0 + m, 0)), row,
                  grp.mod_spec(d, 0, 1), grp.mod_spec(d, 0, 1), grp.mod_spec(d, 0, 1), vec, vec],
        out_specs=out_specs,
        compiler_params=_cparams(("arbitrary",)),
        name="moe_ln_residual",
    )(f_all, x, gate, sc, sh, ln_g.reshape(1, d), ln_b.reshape(1, d))
    return (res[0], res[1]) if has_next else (res[0], None)


def _to_time_major(a):
    a = jnp.swapaxes(a, 0, 1)
    return a.reshape((a.shape[0] * a.shape[1],) + a.shape[2:])


def _from_time_major(a, db):
    a = a.reshape((a.shape[0] // db, db) + a.shape[1:])
    return jnp.swapaxes(a, 0, 1)


def kernel(x_prompt, x_sample, cache_k, cache_v, cache_lf, state_conv_b, state_conv_c, state_h, page_table, c_prompt, c_sample, w_ada, b_ada, ln_g, ln_b, fox_w_in, fox_b_f, fox_w_out, sc_w_in, sc_w_conv, sc_w_out, lru_w_in, lru_conv_w, lru_conv_b, lru_w_a, lru_b_a, lru_w_x, lru_b_x, lru_lam, lru_w_out, ffn_w_gu, ffn_w_down, moe_w_router, moe_w_gu, moe_w_down):
    b, t, d = x_prompt.shape
    db, dt, _ = x_sample.shape
    depth = w_ada.shape[0]
    nh = fox_b_f.shape[1]
    hd = d // nh
    assert hd == FOX_HEAD_DIM and d % LANE_V7X == 0 and db % SUBLANE_V7X == 0
    alpha = (2.0 * depth) ** 0.25
    n_fox = cache_k.shape[0]
    n_pool, page = cache_k.shape[1], cache_k.shape[2]
    ns = db * dt

    gp = _Group(b * t, _pick(t, (1024, 512, 256, 128)), t, 1, b)
    gs = _Group(ns, ns, ns, db, 1)

    mods = _ada(jnp.concatenate([c_prompt, c_sample], axis=0), w_ada, b_ada)
    mods_p = mods[:, :b].reshape(depth, b, 6, d)
    mods_s = jnp.tile(mods[:, b:].reshape(depth, db, 6, d), (1, dt, 1, 1))

    def mod(layer, which, comp):
        if which == 0:
            return mods_p[layer, :, comp][:, None, :]
        return mods_s[layer, :, comp][None]

    cache_kt = jnp.transpose(cache_k, (0, 1, 3, 4, 2))
    cache_vt = jnp.transpose(cache_v, (0, 1, 3, 4, 2))
    cache_lft = jnp.transpose(cache_lf, (0, 1, 3, 2))

    groups = (gp, gs)
    xs = [x_prompt.reshape(b * t, d), _to_time_major(x_sample)]
    hm = [_modulate(groups[w], xs[w], mod(0, w, 1), mod(0, w, 0)) for w in range(2)]
    outs = [dict(k=[], v=[], lf=[], cb=[], cc=[], h=[]) for _ in range(2)]
    x1, hf, gates_t = [None, None], [None, None], [None, None]
    kt_all = jnp.zeros((n_fox, b, d, t), F32)
    vt_all = jnp.zeros((n_fox, b, d, t), F32)

    for i in range(depth):
        kind, j = i % N_MIXERS, i // N_MIXERS
        last = i == depth - 1
        is_moe = i % 2 == 1
        for w in range(2):
            grp = groups[w]
            o = outs[w]
            if kind == 0:
                (qb,) = _linear(grp, hm[w], fox_w_in, j, 0, d, [F32 if w else BF],
                                scale=1.0 if w else Q_SCALE * LOG2_E)
                lf = _forget(grp, hm[w], fox_w_in[j][:, 3 * d:], fox_b_f[j])
                if w == 0:
                    kt_all, ktb = _linear_t(grp, hm[w], fox_w_in[j][:, d:2 * d].T, kt_all, j, rows_out=False)
                    vt_all, vb = _linear_t(grp, hm[w], fox_w_in[j][:, 2 * d:3 * d].T, vt_all, j, rows_out=True)
                    fcum = _cumsum_time(jnp.swapaxes(lf.reshape(b, t, nh), 1, 2), LOG2_E)
                    a_mix = _attention_prompt(qb, ktb, vb, fcum.reshape(b, nh // 2, 2, t), b, t)
                    o['lf'].append(lf.reshape(b, t, nh))
                else:
                    k32, kb = _linear(grp, hm[w], fox_w_in, j, d, d, [F32, BF])
                    v32, vb = _linear(grp, hm[w], fox_w_in, j, 2 * d, d, [F32, BF])
                    nk = 2 * SUBLANE_V7X
                    assert dt <= nk
                    k_b = _from_time_major(k32, db)
                    v_b = _from_time_major(v32, db)
                    lf_b = _from_time_major(lf, db)
                    pad = ((0, 0), (0, nk - dt), (0, 0))
                    lf_new_t = jnp.pad(jnp.swapaxes(lf_b, 1, 2), ((0, 0), (0, 0), (0, LANE_V7X - dt)))
                    att = _attention_sample(j, page_table, _from_time_major(qb, db), jnp.pad(k_b, pad),
                                            jnp.pad(v_b, pad), lf_new_t, cache_kt, cache_vt, cache_lft)
                    a_mix = _to_time_major(att).astype(BF)
                    o['k'].append(k_b.reshape(db, dt, nh, hd))
                    o['v'].append(v_b.reshape(db, dt, nh, hd))
                    o['lf'].append(lf_b)
                w_out = fox_w_out
            elif kind == 1:
                width = sc_w_conv.shape[1]
                if w == 0:
                    init = jnp.zeros((1, SUBLANE_V7X, d), F32)
                else:
                    init = _to_time_major(state_conv_b[j])[None]
                a_mix, tail = _short_conv(grp, hm[w], sc_w_in, j, sc_w_conv[j], init)
                if w == 0:
                    o['cb'].append(tail[:, SUBLANE_V7X - (width - 1):, :])
                else:
                    o['cb'].append(_from_time_major(tail[0], db))
                w_out = sc_w_out
            else:
                width = lru_conv_w.shape[1]
                if w == 0:
                    init = jnp.zeros((1, SUBLANE_V7X, d), F32)
                    h0 = jnp.zeros((1, SUBLANE_V7X, d), F32)
                else:
                    init = _to_time_major(state_conv_c[j])[None]
                    h0 = state_h[j][None]
                a_mix, tail, hl = _rglru(grp, hm[w], lru_w_in, j, lru_conv_w[j], lru_conv_b[j], lru_w_a[j],
                                         lru_b_a[j], lru_w_x[j], lru_b_x[j], lru_lam[j], init, h0)
                if w == 0:
                    o['cc'].append(tail[:, SUBLANE_V7X - (width - 1):, :])
                    o['h'].append(hl[:, 0, :])
                else:
                    o['cc'].append(_from_time_major(tail[0], db))
                    o['h'].append(hl[0])
                w_out = lru_w_out

            res = _out_ln(grp, a_mix, w_out, j, xs[w], mod(i, w, 2), mod(i, w, 4), mod(i, w, 3),
                          ln_g[i, 0], ln_b[i, 0], alpha, moe_w_router[i // 2] if is_moe else None)
            x1[w], hf[w] = res[0], res[1]
            if is_moe:
                gates_t[w] = res[2]

        nxt = [(mod(i + 1, w, 1), mod(i + 1, w, 0)) if not last else (mod(i, w, 1), mod(i, w, 0)) for w in range(2)]
        if is_moe:
            n_real = b * t + ns
            n_all = -(-n_real // MOE_TT) * MOE_TT
            h_all = jnp.concatenate([hf[0], hf[1], jnp.zeros((n_all - n_real, d), BF)], axis=0)
            g_all = jnp.concatenate([gates_t[0], gates_t[1],
                                     jnp.zeros((gates_t[0].shape[0], n_all - n_real), F32)], axis=1)
            f_all = _moe(g_all, h_all, moe_w_gu, moe_w_down, i // 2)
            for w in range(2):
                xs[w], hm[w] = _ln_residual(groups[w], f_all, w * b * t, x1[w], mod(i, w, 5), nxt[w][0], nxt[w][1],
                                            ln_g[i, 1], ln_b[i, 1], alpha, has_next=not last)
        else:
            for w in range(2):
                xs[w], hm[w] = _ffn(groups[w], hf[w], ffn_w_gu, ffn_w_down, i // 2, x1[w], mod(i, w, 5),
                                    nxt[w][0], nxt[w][1], ln_g[i, 1], ln_b[i, 1], alpha, has_next=not last)

    y_prompt = xs[0].reshape(b, t, d)
    y_sample = _from_time_major(xs[1], db)
    op, os_ = outs
    return (y_prompt, y_sample,
            jnp.transpose(kt_all.reshape(n_fox, b, nh, hd, t), (0, 1, 4, 2, 3)),
            jnp.transpose(vt_all.reshape(n_fox, b, nh, hd, t), (0, 1, 4, 2, 3)), jnp.stack(op['lf']),
            jnp.stack(op['cb']), jnp.stack(op['cc']), jnp.stack(op['h']),
            jnp.stack(os_['k']), jnp.stack(os_['v']), jnp.stack(os_['lf']),
            jnp.stack(os_['cb']), jnp.stack(os_['cc']), jnp.stack(os_['h']))
```

```python
import functools

import jax
import jax.numpy as jnp
from jax import lax
from jax.experimental import pallas as pl
from jax.experimental.pallas import tpu as pltpu

BF = jnp.bfloat16
F32 = jnp.float32

LANE_V7X = 128
SUBLANE_V7X = 8
VMEM_LIMIT_V7X = 52 * 1024 * 1024

FOX_HEAD_DIM = 64
Q_SCALE = FOX_HEAD_DIM ** -0.5
LOG2_E = 1.4426950408889634
LRU_C = 8.0
LN_EPS = 1e-5
N_MIXERS = 3
TOP_K = 2
PAGES_PER_STEP = 8
ATTN_ROW_GROUPS = 1


def _cparams(sem):
    return pltpu.CompilerParams(dimension_semantics=sem, vmem_limit_bytes=VMEM_LIMIT_V7X)


def _dot(a, b):
    return jnp.dot(a, b, preferred_element_type=F32)


def _dot_nt(a, b):
    return lax.dot_general(a, b, (((1,), (1,)), ((), ())), preferred_element_type=F32)


def _split(a):
    hi = a.astype(BF)
    lo = (a - hi.astype(F32)).astype(BF)
    return hi, lo


def _dot3(a, b):
    ah, al = _split(a)
    bh, bl = _split(b)
    return _dot(ah, bh) + (_dot(ah, bl) + _dot(al, bh))


def _pick(n, cands):
    for c in cands:
        if n % c == 0:
            return c
    return n


class _Group:
    def __init__(self, n_rows, tm, rows_per_seq, stride, n_seq):
        self.n = n_rows
        self.tm = tm
        self.nm = n_rows // tm
        self.rps = rows_per_seq
        self.g = 1 if stride == 1 else tm
        self.stride = stride
        self.tps = rows_per_seq // tm
        self.n_seq = n_seq

    def mod_spec(self, d, m_axis, n_axes):
        tm, rps = self.tm, self.rps

        def idx(*ids):
            return ((ids[m_axis] * tm) // rps, 0, 0)
        return pl.BlockSpec((None, self.g, d), idx)


def _ada_kernel(c_ref, w_ref, b_ref, o_ref):
    c = c_ref[...]
    a = c * jax.nn.sigmoid(c)
    o_ref[...] = _dot3(a, w_ref[...]) + b_ref[...]


def _ada(c_all, w_ada, b_ada):
    depth, d, d6 = w_ada.shape
    s = c_all.shape[0]
    tn = _pick(d6, (1024, 512, 256, 128))
    return pl.pallas_call(
        _ada_kernel,
        out_shape=jax.ShapeDtypeStruct((depth, s, d6), F32),
        grid=(depth, d6 // tn),
        in_specs=[pl.BlockSpec((s, d), lambda l, n: (0, 0)),
                  pl.BlockSpec((None, d, tn), lambda l, n: (l, 0, n)),
                  pl.BlockSpec((None, 1, tn), lambda l, n: (l, 0, n))],
        out_specs=pl.BlockSpec((None, s, tn), lambda l, n: (l, 0, n)),
        compiler_params=_cparams(("arbitrary", "arbitrary")),
        name="ada_mod",
    )(c_all, w_ada, b_ada.reshape(depth, 1, d6))


def _modulate_kernel(x_ref, sc_ref, sh_ref, o_ref):
    o_ref[...] = (x_ref[...] * (1.0 + sc_ref[...]) + sh_ref[...]).astype(BF)


def _modulate(grp, x, sc, sh):
    n, d = x.shape
    tm = grp.tm
    return pl.pallas_call(
        _modulate_kernel,
        out_shape=jax.ShapeDtypeStruct((n, d), BF),
        grid=(grp.nm,),
        in_specs=[pl.BlockSpec((tm, d), lambda m: (m, 0)),
                  grp.mod_spec(d, 0, 1), grp.mod_spec(d, 0, 1)],
        out_specs=pl.BlockSpec((tm, d), lambda m: (m, 0)),
        compiler_params=_cparams(("arbitrary",)),
        name="modulate",
    )(x, sc, sh)


def _linear_kernel(x_ref, w_ref, *refs, scale, n_out):
    outs, wb = refs[:n_out], refs[n_out]

    @pl.when(pl.program_id(1) == 0)
    def _():
        wb[...] = w_ref[...].astype(BF)

    y = _dot(x_ref[...], wb[...])
    if scale != 1.0:
        y = y * scale
    for o in outs:
        o[...] = y.astype(o.dtype)


def _linear(grp, x, w3, layer, col0, n_cols, out_dtypes, scale=1.0):
    n, k = x.shape
    tm = grp.tm
    tn = _pick(n_cols, (512, 256, 128))
    assert col0 % tn == 0
    cb0 = col0 // tn
    outs = [jax.ShapeDtypeStruct((n, n_cols), dt) for dt in out_dtypes]
    return pl.pallas_call(
        functools.partial(_linear_kernel, scale=scale, n_out=len(outs)),
        out_shape=outs,
        grid=(n_cols // tn, grp.nm),
        in_specs=[pl.BlockSpec((tm, k), lambda c, m: (m, 0)),
                  pl.BlockSpec((None, k, tn), lambda c, m: (layer, 0, cb0 + c))],
        out_specs=[pl.BlockSpec((tm, tn), lambda c, m: (m, c)) for _ in outs],
        scratch_shapes=[pltpu.VMEM((k, tn), BF)],
        compiler_params=_cparams(("arbitrary", "arbitrary")),
        name="linear",
    )(x, w3)


def _linear_t_kernel(x_ref, wt_ref, dst_in_ref, yt32_ref, yb_ref, wb, *, rows_out):
    del dst_in_ref

    @pl.when(pl.program_id(1) == 0)
    def _():
        wb[...] = wt_ref[...].astype(BF)

    x = x_ref[...]
    yt = _dot_nt(wb[...], x)
    yt32_ref[...] = yt
    if rows_out:
        yb_ref[...] = _dot_nt(x, wb[...]).astype(yb_ref.dtype)
    else:
        yb_ref[...] = yt.astype(yb_ref.dtype)


def _linear_t(grp, x, wt, dst, layer, rows_out):
    n, k = x.shape
    nl, nb, f, t = dst.shape
    tm = grp.tm
    tps = grp.tps
    tn = _pick(f, (512, 256, 128))
    if rows_out:
        yb_shape = jax.ShapeDtypeStruct((n, f), BF)
        yb_spec = pl.BlockSpec((tm, tn), lambda c, m: (m, c))
    else:
        yb_shape = jax.ShapeDtypeStruct((nb, f, t), BF)
        yb_spec = pl.BlockSpec((None, tn, tm), lambda c, m: (m // tps, c, m % tps))
    return pl.pallas_call(
        functools.partial(_linear_t_kernel, rows_out=rows_out),
        out_shape=[jax.ShapeDtypeStruct(dst.shape, F32), yb_shape],
        grid=(f // tn, grp.nm),
        in_specs=[pl.BlockSpec((tm, k), lambda c, m: (m, 0)),
                  pl.BlockSpec((tn, k), lambda c, m: (c, 0)),
                  pl.BlockSpec(memory_space=pl.ANY)],
        out_specs=[pl.BlockSpec((None, None, tn, tm), lambda c, m: (layer, m // tps, c, m % tps)), yb_spec],
        scratch_shapes=[pltpu.VMEM((tn, k), BF)],
        input_output_aliases={2: 0},
        compiler_params=_cparams(("arbitrary", "arbitrary")),
        name="linear_t",
    )(x, wt, dst)


def _log_sigmoid(z):
    return jnp.minimum(z, 0.0) - jnp.log1p(jnp.exp(-jnp.abs(z)))


def _forget_kernel(x_ref, w_ref, b_ref, o_ref):
    z = _dot(x_ref[...], w_ref[...].astype(BF)) + b_ref[...]
    o_ref[...] = _log_sigmoid(z)


def _forget(grp, x, w_f, b_f):
    n, k = x.shape
    h = w_f.shape[1]
    tm = grp.tm
    return pl.pallas_call(
        _forget_kernel,
        out_shape=jax.ShapeDtypeStruct((n, h), F32),
        grid=(grp.nm,),
        in_specs=[pl.BlockSpec((tm, k), lambda m: (m, 0)),
                  pl.BlockSpec((k, h), lambda m: (0, 0)),
                  pl.BlockSpec((1, h), lambda m: (0, 0))],
        out_specs=pl.BlockSpec((tm, h), lambda m: (m, 0)),
        compiler_params=_cparams(("arbitrary",)),
        name="forget_gate",
    )(x, w_f, b_f.reshape(1, h))


def _prefix_lanes(x):
    n = x.shape[-1]
    lane = lax.broadcasted_iota(jnp.int32, x.shape, x.ndim - 1)
    d = 1
    while d < n:
        x = x + jnp.where(lane >= d, pltpu.roll(x, d, x.ndim - 1), 0.0)
        d *= 2
    return x


def _cumsum_kernel(x_ref, o_ref, *, scale):
    o_ref[...] = _prefix_lanes(x_ref[...]) * scale


def _cumsum_time(lf_t, scale):
    b, h, t = lf_t.shape
    return pl.pallas_call(
        functools.partial(_cumsum_kernel, scale=scale),
        out_shape=jax.ShapeDtypeStruct((b, h, t), F32),
        grid=(b,),
        in_specs=[pl.BlockSpec((None, h, t), lambda i: (i, 0, 0))],
        out_specs=pl.BlockSpec((None, h, t), lambda i: (i, 0, 0)),
        compiler_params=_cparams(("arbitrary",)),
        name="forget_cumsum",
    )(lf_t)


def _attn_kernel(q_ref, k_ref, v_ref, f_ref, o_ref, m_sc, acc_sc, *, tq):
    qi = pl.program_id(2)
    half = LANE_V7X // 2
    lane = lax.broadcasted_iota(jnp.int32, (tq, LANE_V7X), 1)
    lo = lane < half
    q = q_ref[...]
    zero = jnp.zeros_like(q)
    qs = (jnp.where(lo, q, zero), jnp.where(lo, zero, q))
    for h in range(2):
        m_sc[h] = jnp.full((tq, LANE_V7X), -jnp.inf, F32)
        acc_sc[h] = jnp.zeros((tq, LANE_V7X), F32)

    def block(k0, masked):
        kb = k_ref[:, pl.ds(k0, tq)]
        vb = v_ref[pl.ds(k0, tq), :]
        one = jnp.ones_like(vb)
        vbs = (jnp.where(lo, vb, one), jnp.where(lo, one, vb))
        rg = tq // ATTN_ROW_GROUPS
        chains = [(h, g * rg) for h in range(2) for g in range(ATTN_ROW_GROUPS)]
        ss = []
        for h, r0 in chains:
            s = _dot(qs[h][r0:r0 + rg], kb) - f_ref[h:h + 1, pl.ds(k0, tq)]
            if masked:
                row = lax.broadcasted_iota(jnp.int32, (rg, tq), 0) + r0
                col = lax.broadcasted_iota(jnp.int32, (rg, tq), 1)
                s = jnp.where(col <= row, s, -jnp.inf)
            ss.append(s)
        for (h, r0), s in zip(chains, ss):
            m_prev = m_sc[h, r0:r0 + rg, :]
            m_new = jnp.maximum(m_prev, jnp.max(s, axis=1, keepdims=True))
            p = jnp.exp2(s - m_new[:, 0:1])
            acc_sc[h, r0:r0 + rg, :] = (jnp.exp2(m_prev - m_new) * acc_sc[h, r0:r0 + rg, :]
                                        + _dot(p.astype(BF), vbs[h]))
            m_sc[h, r0:r0 + rg, :] = m_new

    def body(i, carry):
        block(pl.multiple_of(i * tq, tq), False)
        return carry

    lax.fori_loop(0, qi, body, 0)
    block(pl.multiple_of(qi * tq, tq), True)
    a0, a1 = acc_sc[0], acc_sc[1]
    o = jnp.where(lo, a0 / pltpu.roll(a0, half, 1), a1 / pltpu.roll(a1, half, 1))
    o_ref[...] = o.astype(o_ref.dtype)


def _attention_prompt(q, kt, v, fcum, b, t):
    n, d = q.shape
    hp = d // LANE_V7X
    tq = _pick(t, (1024, 512, 256, 128))
    nq = t // tq
    return pl.pallas_call(
        functools.partial(_attn_kernel, tq=tq),
        out_shape=jax.ShapeDtypeStruct((n, d), BF),
        grid=(b, hp, nq),
        in_specs=[pl.BlockSpec((tq, LANE_V7X), lambda i, j, qi: (i * nq + qi, j)),
                  pl.BlockSpec((None, LANE_V7X, t), lambda i, j, qi: (i, j, 0)),
                  pl.BlockSpec((t, LANE_V7X), lambda i, j, qi: (i, j)),
                  pl.BlockSpec((None, None, 2, t), lambda i, j, qi: (i, j, 0, 0))],
        out_specs=pl.BlockSpec((tq, LANE_V7X), lambda i, j, qi: (i * nq + qi, j)),
        scratch_shapes=[pltpu.VMEM((2, tq, LANE_V7X), F32)] * 2,
        compiler_params=_cparams(("arbitrary", "arbitrary", "arbitrary")),
        name="fox_attention_prompt",
    )(q, kt, v, fcum)


def _split3(x):
    hi = x.astype(BF)
    r1 = x - hi.astype(F32)
    mid = r1.astype(BF)
    lo = (r1 - mid.astype(F32)).astype(BF)
    return hi, mid, lo


def _decode_kernel(pt_ref, q_ref, kn_ref, vn_ref, lfn_ref, *refs, pp, nh, nk):
    k_refs = refs[:pp]
    v_refs = refs[pp:2 * pp]
    lf_refs = refs[2 * pp:3 * pp]
    o_ref, qbd, m_sc, l_sc, acc_sc, c_sc = refs[3 * pp:]
    g = pl.program_id(1)
    dt, d = q_ref.shape
    hd = d // nh
    r = dt * nh
    own = (lax.rem(lax.broadcasted_iota(jnp.int32, (r, d), 0), nh)
           == lax.div(lax.broadcasted_iota(jnp.int32, (r, d), 1), hd))

    @pl.when(g == 0)
    def _():
        q = q_ref[...] * Q_SCALE
        qrep = jnp.concatenate([jnp.broadcast_to(q[t:t + 1], (nh, d)) for t in range(dt)], axis=0)
        qbd[...] = jnp.where(own, qrep, 0.0).astype(BF)
        m_sc[...] = jnp.full(m_sc.shape, -jnp.inf, F32)
        l_sc[...] = jnp.zeros(l_sc.shape, F32)
        acc_sc[...] = jnp.zeros(acc_sc.shape, F32)
        c_sc[...] = jnp.zeros(c_sc.shape, F32)

    def softmax_step(s):
        m_prev = m_sc[...]
        m_new = jnp.maximum(m_prev, jnp.max(s, axis=1, keepdims=True))
        alpha = jnp.exp(m_prev - m_new)
        p = jnp.exp(s - m_new[:, 0:1])
        l_sc[...] = alpha * l_sc[...] + jnp.sum(p, axis=1, keepdims=True)
        m_sc[...] = m_new
        return alpha[:, 0:1], p.astype(BF)

    def d_by_keys(ref):
        x = ref[...]
        return x.reshape(x.shape[0] * x.shape[1], x.shape[2]).astype(BF)

    qb = qbd[...]
    lf_all = jnp.concatenate([lf_refs[i][...] for i in range(pp)], axis=1)
    cum = _prefix_lanes(lf_all) + c_sc[:, 0:1]
    c_sc[...] = c_sc[...] + jnp.sum(lf_all, axis=1, keepdims=True)
    kt = jnp.concatenate([d_by_keys(k_refs[i]) for i in range(pp)], axis=1)
    vt = jnp.concatenate([d_by_keys(v_refs[i]) for i in range(pp)], axis=1)
    alpha, p = softmax_step(_dot(qb, kt) - jnp.concatenate([cum] * dt, axis=0))
    acc_sc[...] = alpha * acc_sc[...] + _dot_nt(p, vt)

    @pl.when(g == pl.num_programs(1) - 1)
    def _():
        cum_n = (_prefix_lanes(lfn_ref[...]) + c_sc[:, 0:1])[:, 0:nk]
        sn = _dot_nt(qb, kn_ref[...].astype(BF)) - jnp.concatenate([cum_n] * dt, axis=0)
        trow = lax.div(lax.broadcasted_iota(jnp.int32, (r, nk), 0), nh)
        col = lax.broadcasted_iota(jnp.int32, (r, nk), 1)
        alpha_n, pn = softmax_step(jnp.where(col <= trow, sn, -jnp.inf))
        acc = alpha_n * acc_sc[...] + _dot(pn, vn_ref[...].astype(BF))
        o = jnp.where(own, acc / l_sc[:, 0:1], 0.0)
        o_ref[...] = jnp.sum(o.reshape(dt, nh, d), axis=1)


def _attention_sample(layer, page_table, q_b, k_new, v_new, lf_new_t, cache_kt, cache_vt, cache_lft):
    db, dt, d = q_b.shape
    nk = k_new.shape[1]
    nh, hd, page = cache_kt.shape[2:]
    n_pages = page_table.shape[1]
    pp = _pick(n_pages, (PAGES_PER_STEP, 4, 2, 1))
    r = dt * nh

    def page_spec(shape, i):
        zeros = (0,) * len(shape)
        return pl.BlockSpec((None, None) + shape, lambda b, g, pt: (layer, pt[b, g * pp + i]) + zeros)

    in_specs = [pl.BlockSpec((None, dt, d), lambda b, g, pt: (b, 0, 0)),
                pl.BlockSpec((None, nk, d), lambda b, g, pt: (b, 0, 0)),
                pl.BlockSpec((None, nk, d), lambda b, g, pt: (b, 0, 0)),
                pl.BlockSpec((None, nh, LANE_V7X), lambda b, g, pt: (b, 0, 0))]
    in_specs += [page_spec((nh, hd, page), i) for i in range(pp)]
    in_specs += [page_spec((nh, hd, page), i) for i in range(pp)]
    in_specs += [page_spec((nh, page), i) for i in range(pp)]
    grid_spec = pltpu.PrefetchScalarGridSpec(
        num_scalar_prefetch=1,
        grid=(db, n_pages // pp),
        in_specs=in_specs,
        out_specs=pl.BlockSpec((None, dt, d), lambda b, g, pt: (b, 0, 0)),
        scratch_shapes=[pltpu.VMEM((r, d), BF),
                        pltpu.VMEM((r, LANE_V7X), F32),
                        pltpu.VMEM((r, LANE_V7X), F32),
                        pltpu.VMEM((r, d), F32),
                        pltpu.VMEM((nh, LANE_V7X), F32)])
    return pl.pallas_call(
        functools.partial(_decode_kernel, pp=pp, nh=nh, nk=nk),
        out_shape=jax.ShapeDtypeStruct((db, dt, d), F32),
        grid_spec=grid_spec,
        compiler_params=_cparams(("arbitrary", "arbitrary")),
        name="fox_attention_sample",
    )(page_table, q_b, k_new, v_new, lf_new_t,
      *([cache_kt] * pp), *([cache_vt] * pp), *([cache_lft] * pp))


def _conv_tile(e_ref, carry_ref, init_ref, tail_ref, u, w_ref, first, *, hb, stride, width):
    tm = u.shape[0]

    @pl.when(first)
    def _():
        e_ref[0:hb, :] = init_ref[...]

    @pl.when(jnp.logical_not(first))
    def _():
        e_ref[0:hb, :] = carry_ref[...]

    e_ref[hb:hb + tm, :] = u
    y = None
    for j in range(width):
        off = hb - (width - 1 - j) * stride
        term = w_ref[j:j + 1, :] * e_ref[off:off + tm, :]
        y = term if y is None else y + term
    tail = e_ref[tm:tm + hb, :]
    carry_ref[...] = tail
    tail_ref[...] = tail
    return y


def _sc_kernel(x_ref, wb_ref, wc_ref, wv_ref, wconv_ref, init_ref, mix_ref, tail_ref,
               wb, e_sc, carry, *, tps, hb, stride, width):
    m = pl.program_id(1)

    @pl.when(m == 0)
    def _():
        wb[0] = wb_ref[...].astype(BF)
        wb[1] = wc_ref[...].astype(BF)
        wb[2] = wv_ref[...].astype(BF)

    x = x_ref[...]
    g_b = _dot(x, wb[0])
    u = _dot(x, wb[1]) * _dot(x, wb[2])
    y = _conv_tile(e_sc, carry, init_ref, tail_ref, u, wconv_ref, (m % tps) == 0,
                   hb=hb, stride=stride, width=width)
    mix_ref[...] = (g_b * y).astype(mix_ref.dtype)


def _short_conv(grp, x, w_in, layer, w_conv, init):
    n, d = x.shape
    tm = grp.tm
    width = w_conv.shape[0]
    hb = init.shape[1]
    tn = _pick(d, (256, 128))
    nb = d // tn
    tps = grp.tps
    n_init = init.shape[0]

    def wspec(part):
        return pl.BlockSpec((None, d, tn), lambda c, m: (layer, 0, part * nb + c))

    def seq_of(m):
        return m // tps

    return pl.pallas_call(
        functools.partial(_sc_kernel, tps=tps, hb=hb, stride=grp.stride, width=width),
        out_shape=[jax.ShapeDtypeStruct((n, d), BF),
                   jax.ShapeDtypeStruct((grp.n_seq, hb, d), F32)],
        grid=(nb, grp.nm),
        in_specs=[pl.BlockSpec((tm, d), lambda c, m: (m, 0)),
                  wspec(0), wspec(1), wspec(2),
                  pl.BlockSpec((width, tn), lambda c, m: (0, c)),
                  pl.BlockSpec((None, hb, tn), lambda c, m: (seq_of(m) % n_init, 0, c))],
        out_specs=[pl.BlockSpec((tm, tn), lambda c, m: (m, c)),
                   pl.BlockSpec((None, hb, tn), lambda c, m: (seq_of(m), 0, c))],
        scratch_shapes=[pltpu.VMEM((3, d, tn), BF),
                        pltpu.VMEM((hb + tm, tn), F32),
                        pltpu.VMEM((hb, tn), F32)],
        compiler_params=_cparams(("arbitrary", "arbitrary")),
        name="short_conv_mixer",
    )(x, w_in, w_in, w_in, w_conv, init)


def _softplus(x):
    return jnp.maximum(x, 0.0) + jnp.log1p(jnp.exp(-jnp.abs(x)))


def _neg_expm1(x):
    th = jnp.tanh(0.5 * x)
    return -2.0 * th / (1.0 - th)


def _scan_rows(a, u):
    tm = a.shape[0]
    row = lax.broadcasted_iota(jnp.int32, a.shape, 0)
    d = 1
    while d < tm:
        keep = row >= d
        a_s = jnp.where(keep, pltpu.roll(a, d, 0), 1.0)
        u_s = jnp.where(keep, pltpu.roll(u, d, 0), 0.0)
        u = u + a * u_s
        a = a * a_s
        d *= 2
    return a, u


def _lru_kernel(x_ref, wg_ref, wr_ref, cw_ref, cb_ref, bda_ref, bdx_ref, ba_ref, bx_ref, lam_ref,
                init_ref, h0_ref, mix_ref, tail_ref, hlast_ref,
                wb, bdb, e_sc, carry, hcarry, *, tps, hb, stride, width):
    m = pl.program_id(1)
    tm = x_ref.shape[0]

    @pl.when(m == 0)
    def _():
        wb[0] = wg_ref[...].astype(BF)
        wb[1] = wr_ref[...].astype(BF)
        bdb[0] = bda_ref[...].astype(BF)
        bdb[1] = bdx_ref[...].astype(BF)

    first = (m % tps) == 0
    x = x_ref[...]
    gate = jax.nn.gelu(_dot(x, wb[0]), approximate=True)
    xr = _dot(x, wb[1])
    xc = _conv_tile(e_sc, carry, init_ref, tail_ref, xr, cw_ref, first,
                    hb=hb, stride=stride, width=width) + cb_ref[...]
    xcb = xc.astype(BF)
    r = jax.nn.sigmoid(_dot(xcb, bdb[0]) + ba_ref[...])
    i = jax.nn.sigmoid(_dot(xcb, bdb[1]) + bx_ref[...])
    log_a = (-LRU_C) * r * _softplus(-lam_ref[...])
    a = jnp.exp(log_a)
    u = jnp.sqrt(_neg_expm1(2.0 * log_a)) * i * xc

    @pl.when(first)
    def _():
        hcarry[...] = h0_ref[...]

    if stride == 1:
        a_cum, h_loc = _scan_rows(a, u)
        hs = h_loc + a_cum * hcarry[0:1, :]
        mix_ref[...] = (gate * hs).astype(mix_ref.dtype)
        h_end = jnp.broadcast_to(hs[tm - 1:tm, :], hcarry.shape)
    else:
        h = hcarry[...]
        for t in range(tm // stride):
            sl = slice(t * stride, (t + 1) * stride)
            h = a[sl] * h + u[sl]
            mix_ref[sl, :] = (gate[sl] * h).astype(mix_ref.dtype)
        h_end = h
    hcarry[...] = h_end
    hlast_ref[...] = h_end


def _block_diag(w, per):
    nblk, bd, _ = w.shape
    eye = jnp.eye(per, dtype=w.dtype)
    return jnp.einsum('cide,ik->cidke', w.reshape(nblk // per, per, bd, bd), eye).reshape(
        nblk // per, per * bd, per * bd)


def _rglru(grp, x, w_in, layer, conv_w, conv_b, w_a, b_a, w_x, b_x, lam, init, h0):
    n, d = x.shape
    rw = conv_w.shape[1]
    tm = grp.tm
    width = conv_w.shape[0]
    hb = init.shape[1]
    hr = h0.shape[1]
    bd = w_a.shape[1]
    tn = _pick(rw, (256, 128))
    nb = rw // tn
    tps = grp.tps
    n_init = init.shape[0]
    bda = _block_diag(w_a, tn // bd)
    bdx = _block_diag(w_x, tn // bd)

    def row_spec():
        return pl.BlockSpec((1, tn), lambda c, m: (0, c))

    def seq_of(m):
        return m // tps

    return pl.pallas_call(
        functools.partial(_lru_kernel, tps=tps, hb=hb, stride=grp.stride, width=width),
        out_shape=[jax.ShapeDtypeStruct((n, rw), BF),
                   jax.ShapeDtypeStruct((grp.n_seq, hb, rw), F32),
                   jax.ShapeDtypeStruct((grp.n_seq, hr, rw), F32)],
        grid=(nb, grp.nm),
        in_specs=[pl.BlockSpec((tm, d), lambda c, m: (m, 0)),
                  pl.BlockSpec((None, d, tn), lambda c, m: (layer, 0, c)),
                  pl.BlockSpec((None, d, tn), lambda c, m: (layer, 0, nb + c)),
                  pl.BlockSpec((width, tn), lambda c, m: (0, c)),
                  row_spec(),
                  pl.BlockSpec((None, tn, tn), lambda c, m: (c, 0, 0)),
                  pl.BlockSpec((None, tn, tn), lambda c, m: (c, 0, 0)),
                  row_spec(), row_spec(), row_spec(),
                  pl.BlockSpec((None, hb, tn), lambda c, m: (seq_of(m) % n_init, 0, c)),
                  pl.BlockSpec((None, hr, tn), lambda c, m: (seq_of(m) % n_init, 0, c))],
        out_specs=[pl.BlockSpec((tm, tn), lambda c, m: (m, c)),
                   pl.BlockSpec((None, hb, tn), lambda c, m: (seq_of(m), 0, c)),
                   pl.BlockSpec((None, hr, tn), lambda c, m: (seq_of(m), 0, c))],
        scratch_shapes=[pltpu.VMEM((2, d, tn), BF),
                        pltpu.VMEM((2, tn, tn), BF),
                        pltpu.VMEM((hb + tm, tn), F32),
                        pltpu.VMEM((hb, tn), F32),
                        pltpu.VMEM((hr, tn), F32)],
        compiler_params=_cparams(("arbitrary", "arbitrary")),
        name="rglru_mixer",
    )(x, w_in, w_in, conv_w, conv_b.reshape(1, rw), bda, bdx, b_a.reshape(1, rw),
      b_x.reshape(1, rw), lam.reshape(1, rw), init, h0)


def _post_ln(x, f, gate, ln_g, ln_b, alpha):
    z = alpha * x + (1.0 + gate) * f
    mu = jnp.mean(z, axis=-1, keepdims=True)
    zc = z - mu
    var = jnp.mean(zc * zc, axis=-1, keepdims=True)
    return zc * lax.rsqrt(var + LN_EPS) * ln_g + ln_b


def _top2_gates_t(logits_t):
    ne = float(logits_t.shape[0])
    idx = lax.broadcasted_iota(jnp.int32, logits_t.shape, 0).astype(F32)
    m1 = jnp.max(logits_t, axis=0, keepdims=True)
    i1 = jnp.min(jnp.where(logits_t == m1, idx, ne), axis=0, keepdims=True)
    sel1 = idx == i1
    rest = jnp.where(sel1, -jnp.inf, logits_t)
    m2 = jnp.max(rest, axis=0, keepdims=True)
    i2 = jnp.min(jnp.where(rest == m2, idx, ne), axis=0, keepdims=True)
    sel2 = idx == i2
    e2 = jnp.exp(m2 - m1)
    w1 = 1.0 / (1.0 + e2)
    w2 = e2 / (1.0 + e2)
    return jnp.where(sel1, w1, 0.0) + jnp.where(sel2, w2, 0.0)


def _out_ln_kernel(a_ref, w_ref, x_ref, g_ref, sc_ref, sh_ref, lng_ref, lnb_ref, *refs,
                   alpha, has_router):
    if has_router:
        wr_ref, x1_ref, hf_ref, gates_ref, wb = refs
    else:
        x1_ref, hf_ref, wb = refs

    @pl.when(pl.program_id(0) == 0)
    def _():
        wb[...] = w_ref[...].astype(BF)

    y = _dot(a_ref[...], wb[...])
    x1 = _post_ln(x_ref[...], y, g_ref[...], lng_ref[...], lnb_ref[...], alpha)
    x1_ref[...] = x1
    hf = x1 * (1.0 + sc_ref[...]) + sh_ref[...]
    hf_ref[...] = hf.astype(hf_ref.dtype)
    if has_router:
        wh, wl = _split(wr_ref[...])
        hh, hl = _split(hf)
        logits_t = _dot_nt(wh, hh) + (_dot_nt(wh, hl) + _dot_nt(wl, hh))
        gates_ref[...] = _top2_gates_t(logits_t)


def _out_ln(grp, a, w3, layer, x, gate, sc, sh, ln_g, ln_b, alpha, w_router=None):
    n, d = x.shape
    k = a.shape[1]
    tm = grp.tm
    has_router = w_router is not None
    in_specs = [pl.BlockSpec((tm, k), lambda m: (m, 0)),
                pl.BlockSpec((None, k, d), lambda m: (layer, 0, 0)),
                pl.BlockSpec((tm, d), lambda m: (m, 0)),
                grp.mod_spec(d, 0, 1), grp.mod_spec(d, 0, 1), grp.mod_spec(d, 0, 1),
                pl.BlockSpec((1, d), lambda m: (0, 0)),
                pl.BlockSpec((1, d), lambda m: (0, 0))]
    args = [a, w3, x, gate, sc, sh, ln_g.reshape(1, d), ln_b.reshape(1, d)]
    out_shape = [jax.ShapeDtypeStruct((n, d), F32), jax.ShapeDtypeStruct((n, d), BF)]
    out_specs = [pl.BlockSpec((tm, d), lambda m: (m, 0)), pl.BlockSpec((tm, d), lambda m: (m, 0))]
    if has_router:
        ne = w_router.shape[1]
        in_specs.append(pl.BlockSpec((ne, d), lambda m: (0, 0)))
        args.append(w_router.T)
        out_shape.append(jax.ShapeDtypeStruct((ne, n), F32))
        out_specs.append(pl.BlockSpec((ne, tm), lambda m: (0, m)))
    return pl.pallas_call(
        functools.partial(_out_ln_kernel, alpha=alpha, has_router=has_router),
        out_shape=out_shape,
        grid=(grp.nm,),
        in_specs=in_specs,
        out_specs=out_specs,
        scratch_shapes=[pltpu.VMEM((k, d), BF)],
        compiler_params=_cparams(("arbitrary",)),
        name="out_proj_ln",
    )(*args)


def _swiglu_part(h, wg, wu, wd):
    gu = _dot(h, wg.astype(BF))
    up = _dot(h, wu.astype(BF))
    act = (gu * jax.nn.sigmoid(gu) * up).astype(BF)
    return _dot(act, wd.astype(BF))


def _ffn_kernel(h_ref, wg_ref, wu_ref, wd_ref, x_ref, g_ref, sc_ref, sh_ref, lng_ref, lnb_ref, *refs,
                alpha, has_next):
    if has_next:
        x2_ref, hn_ref, acc = refs
    else:
        x2_ref, acc = refs
    ch = pl.program_id(1)
    part = _swiglu_part(h_ref[...], wg_ref[...], wu_ref[...], wd_ref[...])

    @pl.when(ch == 0)
    def _():
        acc[...] = part

    @pl.when(ch > 0)
    def _():
        acc[...] += part

    @pl.when(ch == pl.num_programs(1) - 1)
    def _():
        x2 = _post_ln(x_ref[...], acc[...], g_ref[...], lng_ref[...], lnb_ref[...], alpha)
        x2_ref[...] = x2
        if has_next:
            hn_ref[...] = (x2 * (1.0 + sc_ref[...]) + sh_ref[...]).astype(hn_ref.dtype)


def _ffn(grp, h, w_gu, w_down, layer, x, gate, sc, sh, ln_g, ln_b, alpha, has_next=True):
    n, d = x.shape
    tm = grp.tm
    ff = w_down.shape[1]
    tf = _pick(ff, (256, 128))
    nch = ff // tf
    row = pl.BlockSpec((tm, d), lambda m, c: (m, 0))
    vec = pl.BlockSpec((1, d), lambda m, c: (0, 0))
    in_specs = [row,
                pl.BlockSpec((None, d, tf), lambda m, c: (layer, 0, c)),
                pl.BlockSpec((None, d, tf), lambda m, c: (layer, 0, nch + c)),
                pl.BlockSpec((None, tf, d), lambda m, c: (layer, c, 0)),
                row, grp.mod_spec(d, 0, 2), grp.mod_spec(d, 0, 2), grp.mod_spec(d, 0, 2), vec, vec]
    out_shape = [jax.ShapeDtypeStruct((n, d), F32)]
    out_specs = [row]
    if has_next:
        out_shape.append(jax.ShapeDtypeStruct((n, d), BF))
        out_specs.append(row)
    res = pl.pallas_call(
        functools.partial(_ffn_kernel, alpha=alpha, has_next=has_next),
        out_shape=out_shape,
        grid=(grp.nm, nch),
        in_specs=in_specs,
        out_specs=out_specs,
        scratch_shapes=[pltpu.VMEM((tm, d), F32)],
        compiler_params=_cparams(("arbitrary", "arbitrary")),
        name="ffn_dense",
    )(h, w_gu, w_gu, w_down, x, gate, sc, sh, ln_g.reshape(1, d), ln_b.reshape(1, d))
    return (res[0], res[1]) if has_next else (res[0], None)


MOE_TT = 512
MOE_RC = 128
MOE_TMG = 1024
MOE_ALIGN = 16


def _seg_sizes(tt):
    sizes = []
    sz = tt
    while sz >= MOE_ALIGN:
        sizes.append(sz)
        sz //= 2
    return tuple(sizes)


def _segments(c16, sizes):
    off = jnp.int32(0)
    for j, sz in enumerate(sizes):
        cond = (c16 & sz) != 0
        yield cond, off, sz, j
        off = off + jnp.where(cond, sz, 0)


def _route_count_kernel(g_ref, o_ref):
    routed = jnp.where(g_ref[...] > 0.0, 1.0, 0.0)
    o_ref[...] = jnp.broadcast_to(jnp.sum(routed, axis=1, keepdims=True), o_ref.shape)


def _route_counts(gates_t, tt):
    ne, n = gates_t.shape
    nt = n // tt
    return pl.pallas_call(
        _route_count_kernel,
        out_shape=jax.ShapeDtypeStruct((nt, ne, LANE_V7X), F32),
        grid=(nt,),
        in_specs=[pl.BlockSpec((ne, tt), lambda i: (0, i))],
        out_specs=pl.BlockSpec((None, ne, LANE_V7X), lambda i: (i, 0, 0)),
        compiler_params=_cparams(("arbitrary",)),
        name="moe_route_counts",
    )(gates_t)


def _dispatch_kernel(base_ref, c16_ref, g_ref, h_ref, sorted_in_ref, sorted_ref, stage, sems,
                     *, ne, tt, rc, sizes):
    del sorted_in_ref
    i = pl.program_id(0)
    gt = g_ref[...]
    routed = gt > 0.0
    ri = lax.broadcasted_iota(jnp.int32, (tt, tt), 0)
    ci = lax.broadcasted_iota(jnp.int32, (tt, tt), 1)
    before = jnp.where(ri < ci, 1.0, 0.0).astype(BF)
    rank = _dot(jnp.where(routed, 1.0, 0.0).astype(BF), before)
    rank = jnp.where(routed, rank, -1.0)
    h = h_ref[...]

    def seg_copy(e, b0, off, sz, j):
        return pltpu.make_async_copy(
            stage.at[e, pl.ds(pl.multiple_of(off, MOE_ALIGN), sz)],
            sorted_ref.at[pl.ds(pl.multiple_of(b0 + off, MOE_ALIGN), sz)],
            sems.at[e, j])

    for e in range(ne):
        c16 = c16_ref[i * ne + e]
        b0 = base_ref[i * ne + e]
        for c in range(tt // rc):
            @pl.when(c * rc < c16)
            def _():
                slot = (lax.broadcasted_iota(jnp.int32, (rc, tt), 0) + c * rc).astype(F32)
                sel = slot == rank[e:e + 1, :]
                stage[e, c * rc:(c + 1) * rc, :] = _dot(jnp.where(sel, 1.0, 0.0).astype(BF), h).astype(BF)
        for cond, off, sz, j in _segments(c16, sizes):
            @pl.when(cond)
            def _():
                seg_copy(e, b0, off, sz, j).start()

    for e in range(ne):
        c16 = c16_ref[i * ne + e]
        b0 = base_ref[i * ne + e]
        for cond, off, sz, j in _segments(c16, sizes):
            @pl.when(cond)
            def _():
                seg_copy(e, b0, off, sz, j).wait()


def _dispatch(base, c16, gates_t, h_all, s_max, tt):
    ne, n = gates_t.shape
    d = h_all.shape[1]
    sizes = _seg_sizes(tt)
    grid_spec = pltpu.PrefetchScalarGridSpec(
        num_scalar_prefetch=2,
        grid=(n // tt,),
        in_specs=[pl.BlockSpec((ne, tt), lambda i, b, c: (0, i)),
                  pl.BlockSpec((tt, d), lambda i, b, c: (i, 0)),
                  pl.BlockSpec(memory_space=pl.ANY)],
        out_specs=pl.BlockSpec(memory_space=pl.ANY),
        scratch_shapes=[pltpu.VMEM((ne, tt, d), BF),
                        pltpu.SemaphoreType.DMA((ne, len(sizes)))])
    return pl.pallas_call(
        functools.partial(_dispatch_kernel, ne=ne, tt=tt, rc=min(MOE_RC, tt), sizes=sizes),
        out_shape=jax.ShapeDtypeStruct((s_max, d), BF),
        grid_spec=grid_spec,
        input_output_aliases={4: 0},
        compiler_params=_cparams(("arbitrary",)),
        name="moe_dispatch",
    )(base, c16, gates_t, h_all, jnp.zeros((s_max, d), BF))


def _grouped_kernel(te_ref, nu_ref, x_ref, wg_ref, wu_ref, wd_ref, y_ref, acc):
    del te_ref
    g = pl.program_id(0)
    ch = pl.program_id(1)
    last = pl.num_programs(1) - 1

    @pl.when(g < nu_ref[0])
    def _():
        part = _swiglu_part(x_ref[...], wg_ref[...], wu_ref[...], wd_ref[...])

        @pl.when(ch == 0)
        def _():
            acc[...] = part

        @pl.when(ch > 0)
        def _():
            acc[...] += part

        @pl.when(ch == last)
        def _():
            y_ref[...] = acc[...].astype(y_ref.dtype)

    @pl.when(jnp.logical_and(g >= nu_ref[0], ch == last))
    def _():
        y_ref[...] = jnp.zeros(y_ref.shape, y_ref.dtype)


def _grouped_ffn(tile_expert, n_used, xs, w_gu, w_down, layer):
    s_max, d = xs.shape
    ff = w_down.shape[2]
    tmg = MOE_TMG
    tf = _pick(ff, (512, 256, 128))
    nch = ff // tf

    def row(g, c, te, nu):
        return (jnp.minimum(g, nu[0] - 1), 0)

    def chunk(g, c, nu):
        return jnp.where(g < nu[0], c, nch - 1)

    grid_spec = pltpu.PrefetchScalarGridSpec(
        num_scalar_prefetch=2,
        grid=(s_max // tmg, nch),
        in_specs=[pl.BlockSpec((tmg, d), row),
                  pl.BlockSpec((None, None, d, tf), lambda g, c, te, nu: (layer, te[g], 0, chunk(g, c, nu))),
                  pl.BlockSpec((None, None, d, tf), lambda g, c, te, nu: (layer, te[g], 0, nch + chunk(g, c, nu))),
                  pl.BlockSpec((None, None, tf, d), lambda g, c, te, nu: (layer, te[g], chunk(g, c, nu), 0))],
        out_specs=pl.BlockSpec((tmg, d), lambda g, c, te, nu: (g, 0)),
        scratch_shapes=[pltpu.VMEM((tmg, d), F32)])
    return pl.pallas_call(
        _grouped_kernel,
        out_shape=jax.ShapeDtypeStruct((s_max, d), BF),
        grid_spec=grid_spec,
        compiler_params=_cparams(("arbitrary", "arbitrary")),
        name="moe_grouped_ffn",
    )(tile_expert, n_used, xs, w_gu, w_gu, w_down)


def _combine_kernel(base_ref, c16_ref, g_ref, y_ref, f_ref, ystage, acc, sems, *, ne, tt, rc, sizes):
    i = pl.program_id(0)

    @pl.when(i == 0)
    def _():
        ystage[...] = jnp.zeros(ystage.shape, ystage.dtype)

    def seg_copy(e, b0, off, sz, j):
        return pltpu.make_async_copy(
            y_ref.at[pl.ds(pl.multiple_of(b0 + off, MOE_ALIGN), sz)],
            ystage.at[e, pl.ds(pl.multiple_of(off, MOE_ALIGN), sz)],
            sems.at[e, j])

    for e in range(ne):
        c16 = c16_ref[i * ne + e]
        b0 = base_ref[i * ne + e]
        for cond, off, sz, j in _segments(c16, sizes):
            @pl.when(cond)
            def _():
                seg_copy(e, b0, off, sz, j).start()

    ri = lax.broadcasted_iota(jnp.int32, (tt, tt), 0)
    ci = lax.broadcasted_iota(jnp.int32, (tt, tt), 1)
    eye = jnp.where(ri == ci, 1.0, 0.0).astype(BF)
    p1, p2, p3 = _split3(g_ref[...])
    gtok = _dot_nt(eye, p1) + _dot_nt(eye, p2) + _dot_nt(eye, p3)
    routed = gtok > 0.0
    before = jnp.where(ci < ri, 1.0, 0.0).astype(BF)
    rank = _dot(before, jnp.where(routed, 1.0, 0.0).astype(BF))
    rank = jnp.where(routed, rank, -1.0)
    acc[...] = jnp.zeros(acc.shape, F32)

    for e in range(ne):
        c16 = c16_ref[i * ne + e]
        b0 = base_ref[i * ne + e]
        for cond, off, sz, j in _segments(c16, sizes):
            @pl.when(cond)
            def _():
                seg_copy(e, b0, off, sz, j).wait()
        for c in range(tt // rc):
            @pl.when(c * rc < c16)
            def _():
                slot = (lax.broadcasted_iota(jnp.int32, (tt, rc), 1) + c * rc).astype(F32)
                sel = slot == rank[:, e:e + 1]
                rows = _dot(jnp.where(sel, 1.0, 0.0).astype(BF), ystage[e, c * rc:(c + 1) * rc, :])
                acc[...] += gtok[:, e:e + 1] * rows

    f_ref[...] = acc[...]


def _combine(base, c16, gates_t, y, tt):
    ne, n = gates_t.shape
    d = y.shape[1]
    sizes = _seg_sizes(tt)
    grid_spec = pltpu.PrefetchScalarGridSpec(
        num_scalar_prefetch=2,
        grid=(n // tt,),
        in_specs=[pl.BlockSpec((ne, tt), lambda i, b, c: (0, i)),
                  pl.BlockSpec(memory_space=pl.ANY)],
        out_specs=pl.BlockSpec((tt, d), lambda i, b, c: (i, 0)),
        scratch_shapes=[pltpu.VMEM((ne, tt, d), BF),
                        pltpu.VMEM((tt, d), F32),
                        pltpu.SemaphoreType.DMA((ne, len(sizes)))])
    return pl.pallas_call(
        functools.partial(_combine_kernel, ne=ne, tt=tt, rc=min(MOE_RC, tt), sizes=sizes),
        out_shape=jax.ShapeDtypeStruct((n, d), F32),
        grid_spec=grid_spec,
        compiler_params=_cparams(("arbitrary",)),
        name="moe_combine",
    )(base, c16, gates_t, y)


def _moe(gates_t, h_all, w_gu, w_down, layer):
    ne, n = gates_t.shape
    tt, tmg = MOE_TT, MOE_TMG
    nt = n // tt
    cnt = _route_counts(gates_t, tt)[:, :, 0].astype(jnp.int32)
    c16 = (cnt + (MOE_ALIGN - 1)) // MOE_ALIGN * MOE_ALIGN
    region = (jnp.sum(c16, axis=0) + (tmg - 1)) // tmg * tmg
    reg_end = jnp.cumsum(region)
    base = (reg_end - region)[None, :] + jnp.cumsum(c16, axis=0) - c16
    s_max = -(-(TOP_K * n + (MOE_ALIGN - 1) * nt * ne + tmg * ne) // tmg) * tmg
    n_used = (reg_end[-1:] // tmg).astype(jnp.int32)
    tile_start = jnp.arange(s_max // tmg, dtype=jnp.int32) * tmg
    tile_expert = jnp.minimum(jnp.sum(tile_start[:, None] >= reg_end[None, :], axis=1), ne - 1).astype(jnp.int32)
    base = base.reshape(-1).astype(jnp.int32)
    c16 = c16.reshape(-1).astype(jnp.int32)
    xs = _dispatch(base, c16, gates_t, h_all, s_max, tt)
    y = _grouped_ffn(tile_expert, n_used, xs, w_gu, w_down, layer)
    return _combine(base, c16, gates_t, y, tt)


def _ln_res_kernel(f_ref, x_ref, g_ref, sc_ref, sh_ref, lng_ref, lnb_ref, *refs, alpha, has_next):
    x2 = _post_ln(x_ref[...], f_ref[...], g_ref[...], lng_ref[...], lnb_ref[...], alpha)
    refs[0][...] = x2
    if has_next:
        refs[1][...] = (x2 * (1.0 + sc_ref[...]) + sh_ref[...]).astype(refs[1].dtype)


def _ln_residual(grp, f_all, row0, x, gate, sc, sh, ln_g, ln_b, alpha, has_next=True):
    n, d = x.shape
    tm = grp.tm
    assert row0 % tm == 0
    blk0 = row0 // tm
    row = pl.BlockSpec((tm, d), lambda m: (m, 0))
    vec = pl.BlockSpec((1, d), lambda m: (0, 0))
    out_shape = [jax.ShapeDtypeStruct((n, d), F32)]
    out_specs = [row]
    if has_next:
        out_shape.append(jax.ShapeDtypeStruct((n, d), BF))
        out_specs.append(row)
    res = pl.pallas_call(
        functools.partial(_ln_res_kernel, alpha=alpha, has_next=has_next),
        out_shape=out_shape,
        grid=(grp.nm,),
        in_specs=[pl.BlockSpec((tm, d), lambda m: (blk0 + m, 0)), row,
                  grp.mod_spec(d, 0, 1), grp.mod_spec(d, 0, 1), grp.mod_spec(d, 0, 1), vec, vec],
        out_specs=out_specs,
        compiler_params=_cparams(("arbitrary",)),
        name="moe_ln_residual",
    )(f_all, x, gate, sc, sh, ln_g.reshape(1, d), ln_b.reshape(1, d))
    return (res[0], res[1]) if has_next else (res[0], None)


def _to_time_major(a):
    a = jnp.swapaxes(a, 0, 1)
    return a.reshape((a.shape[0] * a.shape[1],) + a.shape[2:])


def _from_time_major(a, db):
    a = a.reshape((a.shape[0] // db, db) + a.shape[1:])
    return jnp.swapaxes(a, 0, 1)


def kernel(x_prompt, x_sample, cache_k, cache_v, cache_lf, state_conv_b, state_conv_c, state_h, page_table, c_prompt, c_sample, w_ada, b_ada, ln_g, ln_b, fox_w_in, fox_b_f, fox_w_out, sc_w_in, sc_w_conv, sc_w_out, lru_w_in, lru_conv_w, lru_conv_b, lru_w_a, lru_b_a, lru_w_x, lru_b_x, lru_lam, lru_w_out, ffn_w_gu, ffn_w_down, moe_w_router, moe_w_gu, moe_w_down):
    b, t, d = x_prompt.shape
    db, dt, _ = x_sample.shape
    depth = w_ada.shape[0]
    nh = fox_b_f.shape[1]
    hd = d // nh
    assert hd == FOX_HEAD_DIM and d % LANE_V7X == 0 and db % SUBLANE_V7X == 0
    alpha = (2.0 * depth) ** 0.25
    n_fox = cache_k.shape[0]
    n_pool, page = cache_k.shape[1], cache_k.shape[2]
    ns = db * dt

    gp = _Group(b * t, _pick(t, (1024, 512, 256, 128)), t, 1, b)
    gs = _Group(ns, ns, ns, db, 1)

    mods = _ada(jnp.concatenate([c_prompt, c_sample], axis=0), w_ada, b_ada)
    mods_p = mods[:, :b].reshape(depth, b, 6, d)
    mods_s = jnp.tile(mods[:, b:].reshape(depth, db, 6, d), (1, dt, 1, 1))

    def mod(layer, which, comp):
        if which == 0:
            return mods_p[layer, :, comp][:, None, :]
        return mods_s[layer, :, comp][None]

    cache_kt = jnp.transpose(cache_k, (0, 1, 3, 4, 2))
    cache_vt = jnp.transpose(cache_v, (0, 1, 3, 4, 2))
    cache_lft = jnp.transpose(cache_lf, (0, 1, 3, 2))

    groups = (gp, gs)
    xs = [x_prompt.reshape(b * t, d), _to_time_major(x_sample)]
    hm = [_modulate(groups[w], xs[w], mod(0, w, 1), mod(0, w, 0)) for w in range(2)]
    outs = [dict(k=[], v=[], lf=[], cb=[], cc=[], h=[]) for _ in range(2)]
    x1, hf, gates_t = [None, None], [None, None], [None, None]
    kt_all = jnp.zeros((n_fox, b, d, t), F32)
    vt_all = jnp.zeros((n_fox, b, d, t), F32)

    for i in range(depth):
        kind, j = i % N_MIXERS, i // N_MIXERS
        last = i == depth - 1
        is_moe = i % 2 == 1
        for w in range(2):
            grp = groups[w]
            o = outs[w]
            if kind == 0:
                (qb,) = _linear(grp, hm[w], fox_w_in, j, 0, d, [F32 if w else BF],
                                scale=1.0 if w else Q_SCALE * LOG2_E)
                lf = _forget(grp, hm[w], fox_w_in[j][:, 3 * d:], fox_b_f[j])
                if w == 0:
                    kt_all, ktb = _linear_t(grp, hm[w], fox_w_in[j][:, d:2 * d].T, kt_all, j, rows_out=False)
                    vt_all, vb = _linear_t(grp, hm[w], fox_w_in[j][:, 2 * d:3 * d].T, vt_all, j, rows_out=True)
                    fcum = _cumsum_time(jnp.swapaxes(lf.reshape(b, t, nh), 1, 2), LOG2_E)
                    a_mix = _attention_prompt(qb, ktb, vb, fcum.reshape(b, nh // 2, 2, t), b, t)
                    o['lf'].append(lf.reshape(b, t, nh))
                else:
                    k32, kb = _linear(grp, hm[w], fox_w_in, j, d, d, [F32, BF])
                    v32, vb = _linear(grp, hm[w], fox_w_in, j, 2 * d, d, [F32, BF])
                    nk = 2 * SUBLANE_V7X
                    assert dt <= nk
                    k_b = _from_time_major(k32, db)
                    v_b = _from_time_major(v32, db)
                    lf_b = _from_time_major(lf, db)
                    pad = ((0, 0), (0, nk - dt), (0, 0))
                    lf_new_t = jnp.pad(jnp.swapaxes(lf_b, 1, 2), ((0, 0), (0, 0), (0, LANE_V7X - dt)))
                    att = _attention_sample(j, page_table, _from_time_major(qb, db), jnp.pad(k_b, pad),
                                            jnp.pad(v_b, pad), lf_new_t, cache_kt, cache_vt, cache_lft)
                    a_mix = _to_time_major(att).astype(BF)
                    o['k'].append(k_b.reshape(db, dt, nh, hd))
                    o['v'].append(v_b.reshape(db, dt, nh, hd))
                    o['lf'].append(lf_b)
                w_out = fox_w_out
            elif kind == 1:
                width = sc_w_conv.shape[1]
                if w == 0:
                    init = jnp.zeros((1, SUBLANE_V7X, d), F32)
                else:
                    init = _to_time_major(state_conv_b[j])[None]
                a_mix, tail = _short_conv(grp, hm[w], sc_w_in, j, sc_w_conv[j], init)
                if w == 0:
                    o['cb'].append(tail[:, SUBLANE_V7X - (width - 1):, :])
                else:
                    o['cb'].append(_from_time_major(tail[0], db))
                w_out = sc_w_out
            else:
                width = lru_conv_w.shape[1]
                if w == 0:
                    init = jnp.zeros((1, SUBLANE_V7X, d), F32)
                    h0 = jnp.zeros((1, SUBLANE_V7X, d), F32)
                else:
                    init = _to_time_major(state_conv_c[j])[None]
                    h0 = state_h[j][None]
                a_mix, tail, hl = _rglru(grp, hm[w], lru_w_in, j, lru_conv_w[j], lru_conv_b[j], lru_w_a[j],
                                         lru_b_a[j], lru_w_x[j], lru_b_x[j], lru_lam[j], init, h0)
                if w == 0:
                    o['cc'].append(tail[:, SUBLANE_V7X - (width - 1):, :])
                    o['h'].append(hl[:, 0, :])
                else:
                    o['cc'].append(_from_time_major(tail[0], db))
                    o['h'].append(hl[0])
                w_out = lru_w_out

            res = _out_ln(grp, a_mix, w_out, j, xs[w], mod(i, w, 2), mod(i, w, 4), mod(i, w, 3),
                          ln_g[i, 0], ln_b[i, 0], alpha, moe_w_router[i // 2] if is_moe else None)
            x1[w], hf[w] = res[0], res[1]
            if is_moe:
                gates_t[w] = res[2]

        nxt = [(mod(i + 1, w, 1), mod(i + 1, w, 0)) if not last else (mod(i, w, 1), mod(i, w, 0)) for w in range(2)]
        if is_moe:
            n_real = b * t + ns
            n_all = -(-n_real // MOE_TT) * MOE_TT
            h_all = jnp.concatenate([hf[0], hf[1], jnp.zeros((n_all - n_real, d), BF)], axis=0)
            g_all = jnp.concatenate([gates_t[0], gates_t[1],
                                     jnp.zeros((gates_t[0].shape[0], n_all - n_real), F32)], axis=1)
            f_all = _moe(g_all, h_all, moe_w_gu, moe_w_down, i // 2)
            for w in range(2):
                xs[w], hm[w] = _ln_residual(groups[w], f_all, w * b * t, x1[w], mod(i, w, 5), nxt[w][0], nxt[w][1],
                                            ln_g[i, 1], ln_b[i, 1], alpha, has_next=not last)
        else:
            for w in range(2):
                xs[w], hm[w] = _ffn(groups[w], hf[w], ffn_w_gu, ffn_w_down, i // 2, x1[w], mod(i, w, 5),
                                    nxt[w][0], nxt[w][1], ln_g[i, 1], ln_b[i, 1], alpha, has_next=not last)

    y_prompt = xs[0].reshape(b, t, d)
    y_sample = _from_time_major(xs[1], db)
    op, os_ = outs
    return (y_prompt, y_sample,
            jnp.transpose(kt_all.reshape(n_fox, b, nh, hd, t), (0, 1, 4, 2, 3)),
            jnp.transpose(vt_all.reshape(n_fox, b, nh, hd, t), (0, 1, 4, 2, 3)), jnp.stack(op['lf']),
            jnp.stack(op['cb']), jnp.stack(op['cc']), jnp.stack(op['h']),
            jnp.stack(os_['k']), jnp.stack(os_['v']), jnp.stack(os_['lf']),
            jnp.stack(os_['cb']), jnp.stack(os_['cc']), jnp.stack(os_['h']))
```

```python
import functools

import jax
import jax.numpy as jnp
from jax import lax
from jax.experimental import pallas as pl
from jax.experimental.pallas import tpu as pltpu

BF = jnp.bfloat16
F32 = jnp.float32

LANE_V7X = 128
SUBLANE_V7X = 8
VMEM_LIMIT_V7X = 52 * 1024 * 1024

FOX_HEAD_DIM = 64
Q_SCALE = FOX_HEAD_DIM ** -0.5
LOG2_E = 1.4426950408889634
LRU_C = 8.0
LN_EPS = 1e-5
N_MIXERS = 3
TOP_K = 2
PAGES_PER_STEP = 16
ATTN_ROW_GROUPS = 1


def _cparams(sem):
    return pltpu.CompilerParams(dimension_semantics=sem, vmem_limit_bytes=VMEM_LIMIT_V7X)


def _dot(a, b):
    return jnp.dot(a, b, preferred_element_type=F32)


def _dot_nt(a, b):
    return lax.dot_general(a, b, (((1,), (1,)), ((), ())), preferred_element_type=F32)


def _split(a):
    hi = a.astype(BF)
    lo = (a - hi.astype(F32)).astype(BF)
    return hi, lo


def _dot3(a, b):
    ah, al = _split(a)
    bh, bl = _split(b)
    return _dot(ah, bh) + (_dot(ah, bl) + _dot(al, bh))


def _pick(n, cands):
    for c in cands:
        if n % c == 0:
            return c
    return n


class _Group:
    def __init__(self, n_rows, tm, rows_per_seq, stride, n_seq):
        self.n = n_rows
        self.tm = tm
        self.nm = n_rows // tm
        self.rps = rows_per_seq
        self.g = 1 if stride == 1 else tm
        self.stride = stride
        self.tps = rows_per_seq // tm
        self.n_seq = n_seq

    def mod_spec(self, d, m_axis, n_axes):
        tm, rps = self.tm, self.rps

        def idx(*ids):
            return ((ids[m_axis] * tm) // rps, 0, 0)
        return pl.BlockSpec((None, self.g, d), idx)


def _ada_kernel(c_ref, w_ref, b_ref, o_ref):
    c = c_ref[...]
    a = c * jax.nn.sigmoid(c)
    o_ref[...] = _dot3(a, w_ref[...]) + b_ref[...]


def _ada(c_all, w_ada, b_ada):
    depth, d, d6 = w_ada.shape
    s = c_all.shape[0]
    tn = _pick(d6, (1024, 512, 256, 128))
    return pl.pallas_call(
        _ada_kernel,
        out_shape=jax.ShapeDtypeStruct((depth, s, d6), F32),
        grid=(depth, d6 // tn),
        in_specs=[pl.BlockSpec((s, d), lambda l, n: (0, 0)),
                  pl.BlockSpec((None, d, tn), lambda l, n: (l, 0, n)),
                  pl.BlockSpec((None, 1, tn), lambda l, n: (l, 0, n))],
        out_specs=pl.BlockSpec((None, s, tn), lambda l, n: (l, 0, n)),
        compiler_params=_cparams(("arbitrary", "arbitrary")),
        name="ada_mod",
    )(c_all, w_ada, b_ada.reshape(depth, 1, d6))


def _modulate_kernel(x_ref, sc_ref, sh_ref, o_ref):
    o_ref[...] = (x_ref[...] * (1.0 + sc_ref[...]) + sh_ref[...]).astype(BF)


def _modulate(grp, x, sc, sh):
    n, d = x.shape
    tm = grp.tm
    return pl.pallas_call(
        _modulate_kernel,
        out_shape=jax.ShapeDtypeStruct((n, d), BF),
        grid=(grp.nm,),
        in_specs=[pl.BlockSpec((tm, d), lambda m: (m, 0)),
                  grp.mod_spec(d, 0, 1), grp.mod_spec(d, 0, 1)],
        out_specs=pl.BlockSpec((tm, d), lambda m: (m, 0)),
        compiler_params=_cparams(("arbitrary",)),
        name="modulate",
    )(x, sc, sh)


def _linear_kernel(x_ref, w_ref, *refs, scale, n_out):
    outs, wb = refs[:n_out], refs[n_out]

    @pl.when(pl.program_id(1) == 0)
    def _():
        wb[...] = w_ref[...].astype(BF)

    y = _dot(x_ref[...], wb[...])
    if scale != 1.0:
        y = y * scale
    for o in outs:
        o[...] = y.astype(o.dtype)


def _linear(grp, x, w3, layer, col0, n_cols, out_dtypes, scale=1.0):
    n, k = x.shape
    tm = grp.tm
    tn = _pick(n_cols, (512, 256, 128))
    assert col0 % tn == 0
    cb0 = col0 // tn
    outs = [jax.ShapeDtypeStruct((n, n_cols), dt) for dt in out_dtypes]
    return pl.pallas_call(
        functools.partial(_linear_kernel, scale=scale, n_out=len(outs)),
        out_shape=outs,
        grid=(n_cols // tn, grp.nm),
        in_specs=[pl.BlockSpec((tm, k), lambda c, m: (m, 0)),
                  pl.BlockSpec((None, k, tn), lambda c, m: (layer, 0, cb0 + c))],
        out_specs=[pl.BlockSpec((tm, tn), lambda c, m: (m, c)) for _ in outs],
        scratch_shapes=[pltpu.VMEM((k, tn), BF)],
        compiler_params=_cparams(("arbitrary", "arbitrary")),
        name="linear",
    )(x, w3)


def _linear_t_kernel(x_ref, wt_ref, dst_in_ref, yt32_ref, yb_ref, wb, *, rows_out):
    del dst_in_ref

    @pl.when(pl.program_id(1) == 0)
    def _():
        wb[...] = wt_ref[...].astype(BF)

    x = x_ref[...]
    yt = _dot_nt(wb[...], x)
    yt32_ref[...] = yt
    if rows_out:
        yb_ref[...] = _dot_nt(x, wb[...]).astype(yb_ref.dtype)
    else:
        yb_ref[...] = yt.astype(yb_ref.dtype)


def _linear_t(grp, x, wt, dst, layer, rows_out):
    n, k = x.shape
    nl, nb, f, t = dst.shape
    tm = grp.tm
    tps = grp.tps
    tn = _pick(f, (512, 256, 128))
    if rows_out:
        yb_shape = jax.ShapeDtypeStruct((n, f), BF)
        yb_spec = pl.BlockSpec((tm, tn), lambda c, m: (m, c))
    else:
        yb_shape = jax.ShapeDtypeStruct((nb, f, t), BF)
        yb_spec = pl.BlockSpec((None, tn, tm), lambda c, m: (m // tps, c, m % tps))
    return pl.pallas_call(
        functools.partial(_linear_t_kernel, rows_out=rows_out),
        out_shape=[jax.ShapeDtypeStruct(dst.shape, F32), yb_shape],
        grid=(f // tn, grp.nm),
        in_specs=[pl.BlockSpec((tm, k), lambda c, m: (m, 0)),
                  pl.BlockSpec((tn, k), lambda c, m: (c, 0)),
                  pl.BlockSpec(memory_space=pl.ANY)],
        out_specs=[pl.BlockSpec((None, None, tn, tm), lambda c, m: (layer, m // tps, c, m % tps)), yb_spec],
        scratch_shapes=[pltpu.VMEM((tn, k), BF)],
        input_output_aliases={2: 0},
        compiler_params=_cparams(("arbitrary", "arbitrary")),
        name="linear_t",
    )(x, wt, dst)


def _log_sigmoid(z):
    return jnp.minimum(z, 0.0) - jnp.log1p(jnp.exp(-jnp.abs(z)))


def _forget_kernel(x_ref, w_ref, b_ref, o_ref):
    z = _dot(x_ref[...], w_ref[...].astype(BF)) + b_ref[...]
    o_ref[...] = _log_sigmoid(z)


def _forget(grp, x, w_f, b_f):
    n, k = x.shape
    h = w_f.shape[1]
    tm = grp.tm
    return pl.pallas_call(
        _forget_kernel,
        out_shape=jax.ShapeDtypeStruct((n, h), F32),
        grid=(grp.nm,),
        in_specs=[pl.BlockSpec((tm, k), lambda m: (m, 0)),
                  pl.BlockSpec((k, h), lambda m: (0, 0)),
                  pl.BlockSpec((1, h), lambda m: (0, 0))],
        out_specs=pl.BlockSpec((tm, h), lambda m: (m, 0)),
        compiler_params=_cparams(("arbitrary",)),
        name="forget_gate",
    )(x, w_f, b_f.reshape(1, h))


def _prefix_lanes(x):
    n = x.shape[-1]
    lane = lax.broadcasted_iota(jnp.int32, x.shape, x.ndim - 1)
    d = 1
    while d < n:
        x = x + jnp.where(lane >= d, pltpu.roll(x, d, x.ndim - 1), 0.0)
        d *= 2
    return x


def _cumsum_kernel(x_ref, o_ref, *, scale):
    o_ref[...] = _prefix_lanes(x_ref[...]) * scale


def _cumsum_time(lf_t, scale):
    b, h, t = lf_t.shape
    return pl.pallas_call(
        functools.partial(_cumsum_kernel, scale=scale),
        out_shape=jax.ShapeDtypeStruct((b, h, t), F32),
        grid=(b,),
        in_specs=[pl.BlockSpec((None, h, t), lambda i: (i, 0, 0))],
        out_specs=pl.BlockSpec((None, h, t), lambda i: (i, 0, 0)),
        compiler_params=_cparams(("arbitrary",)),
        name="forget_cumsum",
    )(lf_t)


def _attn_kernel(q_ref, k_ref, v_ref, f_ref, o_ref, m_sc, acc_sc, *, tq):
    qi = pl.program_id(2)
    half = LANE_V7X // 2
    lane = lax.broadcasted_iota(jnp.int32, (tq, LANE_V7X), 1)
    lo = lane < half
    q = q_ref[...]
    zero = jnp.zeros_like(q)
    qs = (jnp.where(lo, q, zero), jnp.where(lo, zero, q))
    for h in range(2):
        m_sc[h] = jnp.full((tq, LANE_V7X), -jnp.inf, F32)
        acc_sc[h] = jnp.zeros((tq, LANE_V7X), F32)

    def block(k0, masked):
        kb = k_ref[:, pl.ds(k0, tq)]
        vb = v_ref[pl.ds(k0, tq), :]
        one = jnp.ones_like(vb)
        vbs = (jnp.where(lo, vb, one), jnp.where(lo, one, vb))
        rg = tq // ATTN_ROW_GROUPS
        chains = [(h, g * rg) for h in range(2) for g in range(ATTN_ROW_GROUPS)]
        ss = []
        for h, r0 in chains:
            s = _dot(qs[h][r0:r0 + rg], kb) - f_ref[h:h + 1, pl.ds(k0, tq)]
            if masked:
                row = lax.broadcasted_iota(jnp.int32, (rg, tq), 0) + r0
                col = lax.broadcasted_iota(jnp.int32, (rg, tq), 1)
                s = jnp.where(col <= row, s, -jnp.inf)
            ss.append(s)
        for (h, r0), s in zip(chains, ss):
            m_prev = m_sc[h, r0:r0 + rg, :]
            m_new = jnp.maximum(m_prev, jnp.max(s, axis=1, keepdims=True))
            p = jnp.exp2(s - m_new[:, 0:1])
            acc_sc[h, r0:r0 + rg, :] = (jnp.exp2(m_prev - m_new) * acc_sc[h, r0:r0 + rg, :]
                                        + _dot(p.astype(BF), vbs[h]))
            m_sc[h, r0:r0 + rg, :] = m_new

    def body(i, carry):
        block(pl.multiple_of(i * tq, tq), False)
        return carry

    lax.fori_loop(0, qi, body, 0)
    block(pl.multiple_of(qi * tq, tq), True)
    a0, a1 = acc_sc[0], acc_sc[1]
    o = jnp.where(lo, a0 / pltpu.roll(a0, half, 1), a1 / pltpu.roll(a1, half, 1))
    o_ref[...] = o.astype(o_ref.dtype)


def _attention_prompt(q, kt, v, fcum, b, t):
    n, d = q.shape
    hp = d // LANE_V7X
    tq = _pick(t, (1024, 512, 256, 128))
    nq = t // tq
    return pl.pallas_call(
        functools.partial(_attn_kernel, tq=tq),
        out_shape=jax.ShapeDtypeStruct((n, d), BF),
        grid=(b, hp, nq),
        in_specs=[pl.BlockSpec((tq, LANE_V7X), lambda i, j, qi: (i * nq + qi, j)),
                  pl.BlockSpec((None, LANE_V7X, t), lambda i, j, qi: (i, j, 0)),
                  pl.BlockSpec((t, LANE_V7X), lambda i, j, qi: (i, j)),
                  pl.BlockSpec((None, None, 2, t), lambda i, j, qi: (i, j, 0, 0))],
        out_specs=pl.BlockSpec((tq, LANE_V7X), lambda i, j, qi: (i * nq + qi, j)),
        scratch_shapes=[pltpu.VMEM((2, tq, LANE_V7X), F32)] * 2,
        compiler_params=_cparams(("arbitrary", "arbitrary", "arbitrary")),
        name="fox_attention_prompt",
    )(q, kt, v, fcum)


def _split3(x):
    hi = x.astype(BF)
    r1 = x - hi.astype(F32)
    mid = r1.astype(BF)
    lo = (r1 - mid.astype(F32)).astype(BF)
    return hi, mid, lo


def _decode_kernel(pt_ref, q_ref, kn_ref, vn_ref, lfn_ref, *refs, pp, nh, nk):
    k_refs = refs[:pp]
    v_refs = refs[pp:2 * pp]
    lf_refs = refs[2 * pp:3 * pp]
    o_ref, qbd, m_sc, l_sc, acc_sc, c_sc = refs[3 * pp:]
    g = pl.program_id(1)
    dt, d = q_ref.shape
    hd = d // nh
    r = dt * nh
    own = (lax.rem(lax.broadcasted_iota(jnp.int32, (r, d), 0), nh)
           == lax.div(lax.broadcasted_iota(jnp.int32, (r, d), 1), hd))

    @pl.when(g == 0)
    def _():
        q = q_ref[...] * Q_SCALE
        qrep = jnp.concatenate([jnp.broadcast_to(q[t:t + 1], (nh, d)) for t in range(dt)], axis=0)
        qbd[...] = jnp.where(own, qrep, 0.0).astype(BF)
        m_sc[...] = jnp.full(m_sc.shape, -jnp.inf, F32)
        l_sc[...] = jnp.zeros(l_sc.shape, F32)
        acc_sc[...] = jnp.zeros(acc_sc.shape, F32)
        c_sc[...] = jnp.zeros(c_sc.shape, F32)

    def softmax_step(s):
        m_prev = m_sc[...]
        m_new = jnp.maximum(m_prev, jnp.max(s, axis=1, keepdims=True))
        alpha = jnp.exp(m_prev - m_new)
        p = jnp.exp(s - m_new[:, 0:1])
        l_sc[...] = alpha * l_sc[...] + jnp.sum(p, axis=1, keepdims=True)
        m_sc[...] = m_new
        return alpha[:, 0:1], p.astype(BF)

    def d_by_keys(ref):
        x = ref[...]
        return x.reshape(x.shape[0] * x.shape[1], x.shape[2]).astype(BF)

    qb = qbd[...]
    lf_all = jnp.concatenate([lf_refs[i][...] for i in range(pp)], axis=1)
    cum = _prefix_lanes(lf_all) + c_sc[:, 0:1]
    c_sc[...] = c_sc[...] + jnp.sum(lf_all, axis=1, keepdims=True)
    kt = jnp.concatenate([d_by_keys(k_refs[i]) for i in range(pp)], axis=1)
    vt = jnp.concatenate([d_by_keys(v_refs[i]) for i in range(pp)], axis=1)
    alpha, p = softmax_step(_dot(qb, kt) - jnp.concatenate([cum] * dt, axis=0))
    acc_sc[...] = alpha * acc_sc[...] + _dot_nt(p, vt)

    @pl.when(g == pl.num_programs(1) - 1)
    def _():
        cum_n = (_prefix_lanes(lfn_ref[...]) + c_sc[:, 0:1])[:, 0:nk]
        sn = _dot_nt(qb, kn_ref[...].astype(BF)) - jnp.concatenate([cum_n] * dt, axis=0)
        trow = lax.div(lax.broadcasted_iota(jnp.int32, (r, nk), 0), nh)
        col = lax.broadcasted_iota(jnp.int32, (r, nk), 1)
        alpha_n, pn = softmax_step(jnp.where(col <= trow, sn, -jnp.inf))
        acc = alpha_n * acc_sc[...] + _dot(pn, vn_ref[...].astype(BF))
        o = jnp.where(own, acc / l_sc[:, 0:1], 0.0)
        o_ref[...] = jnp.sum(o.reshape(dt, nh, d), axis=1)


def _attention_sample(layer, page_table, q_b, k_new, v_new, lf_new_t, cache_kt, cache_vt, cache_lft):
    db, dt, d = q_b.shape
    nk = k_new.shape[1]
    nh, hd, page = cache_kt.shape[2:]
    n_pages = page_table.shape[1]
    pp = _pick(n_pages, (PAGES_PER_STEP, 4, 2, 1))
    r = dt * nh

    def page_spec(shape, i):
        zeros = (0,) * len(shape)
        return pl.BlockSpec((None, None) + shape, lambda b, g, pt: (layer, pt[b, g * pp + i]) + zeros)

    in_specs = [pl.BlockSpec((None, dt, d), lambda b, g, pt: (b, 0, 0)),
                pl.BlockSpec((None, nk, d), lambda b, g, pt: (b, 0, 0)),
                pl.BlockSpec((None, nk, d), lambda b, g, pt: (b, 0, 0)),
                pl.BlockSpec((None, nh, LANE_V7X), lambda b, g, pt: (b, 0, 0))]
    in_specs += [page_spec((nh, hd, page), i) for i in range(pp)]
    in_specs += [page_spec((nh, hd, page), i) for i in range(pp)]
    in_specs += [page_spec((nh, page), i) for i in range(pp)]
    grid_spec = pltpu.PrefetchScalarGridSpec(
        num_scalar_prefetch=1,
        grid=(db, n_pages // pp),
        in_specs=in_specs,
        out_specs=pl.BlockSpec((None, dt, d), lambda b, g, pt: (b, 0, 0)),
        scratch_shapes=[pltpu.VMEM((r, d), BF),
                        pltpu.VMEM((r, LANE_V7X), F32),
                        pltpu.VMEM((r, LANE_V7X), F32),
                        pltpu.VMEM((r, d), F32),
                        pltpu.VMEM((nh, LANE_V7X), F32)])
    return pl.pallas_call(
        functools.partial(_decode_kernel, pp=pp, nh=nh, nk=nk),
        out_shape=jax.ShapeDtypeStruct((db, dt, d), F32),
        grid_spec=grid_spec,
        compiler_params=_cparams(("arbitrary", "arbitrary")),
        name="fox_attention_sample",
    )(page_table, q_b, k_new, v_new, lf_new_t,
      *([cache_kt] * pp), *([cache_vt] * pp), *([cache_lft] * pp))


def _conv_tile(e_ref, carry_ref, init_ref, tail_ref, u, w_ref, first, *, hb, stride, width):
    tm = u.shape[0]

    @pl.when(first)
    def _():
        e_ref[0:hb, :] = init_ref[...]

    @pl.when(jnp.logical_not(first))
    def _():
        e_ref[0:hb, :] = carry_ref[...]

    e_ref[hb:hb + tm, :] = u
    y = None
    for j in range(width):
        off = hb - (width - 1 - j) * stride
        term = w_ref[j:j + 1, :] * e_ref[off:off + tm, :]
        y = term if y is None else y + term
    tail = e_ref[tm:tm + hb, :]
    carry_ref[...] = tail
    tail_ref[...] = tail
    return y


def _sc_kernel(x_ref, wb_ref, wc_ref, wv_ref, wconv_ref, init_ref, mix_ref, tail_ref,
               wb, e_sc, carry, *, tps, hb, stride, width):
    m = pl.program_id(1)

    @pl.when(m == 0)
    def _():
        wb[0] = wb_ref[...].astype(BF)
        wb[1] = wc_ref[...].astype(BF)
        wb[2] = wv_ref[...].astype(BF)

    x = x_ref[...]
    g_b = _dot(x, wb[0])
    u = _dot(x, wb[1]) * _dot(x, wb[2])
    y = _conv_tile(e_sc, carry, init_ref, tail_ref, u, wconv_ref, (m % tps) == 0,
                   hb=hb, stride=stride, width=width)
    mix_ref[...] = (g_b * y).astype(mix_ref.dtype)


def _short_conv(grp, x, w_in, layer, w_conv, init):
    n, d = x.shape
    tm = grp.tm
    width = w_conv.shape[0]
    hb = init.shape[1]
    tn = _pick(d, (256, 128))
    nb = d // tn
    tps = grp.tps
    n_init = init.shape[0]

    def wspec(part):
        return pl.BlockSpec((None, d, tn), lambda c, m: (layer, 0, part * nb + c))

    def seq_of(m):
        return m // tps

    return pl.pallas_call(
        functools.partial(_sc_kernel, tps=tps, hb=hb, stride=grp.stride, width=width),
        out_shape=[jax.ShapeDtypeStruct((n, d), BF),
                   jax.ShapeDtypeStruct((grp.n_seq, hb, d), F32)],
        grid=(nb, grp.nm),
        in_specs=[pl.BlockSpec((tm, d), lambda c, m: (m, 0)),
                  wspec(0), wspec(1), wspec(2),
                  pl.BlockSpec((width, tn), lambda c, m: (0, c)),
                  pl.BlockSpec((None, hb, tn), lambda c, m: (seq_of(m) % n_init, 0, c))],
        out_specs=[pl.BlockSpec((tm, tn), lambda c, m: (m, c)),
                   pl.BlockSpec((None, hb, tn), lambda c, m: (seq_of(m), 0, c))],
        scratch_shapes=[pltpu.VMEM((3, d, tn), BF),
                        pltpu.VMEM((hb + tm, tn), F32),
                        pltpu.VMEM((hb, tn), F32)],
        compiler_params=_cparams(("arbitrary", "arbitrary")),
        name="short_conv_mixer",
    )(x, w_in, w_in, w_in, w_conv, init)


def _softplus(x):
    return jnp.maximum(x, 0.0) + jnp.log1p(jnp.exp(-jnp.abs(x)))


def _neg_expm1(x):
    th = jnp.tanh(0.5 * x)
    return -2.0 * th / (1.0 - th)


def _scan_rows(a, u):
    tm = a.shape[0]
    row = lax.broadcasted_iota(jnp.int32, a.shape, 0)
    d = 1
    while d < tm:
        keep = row >= d
        a_s = jnp.where(keep, pltpu.roll(a, d, 0), 1.0)
        u_s = jnp.where(keep, pltpu.roll(u, d, 0), 0.0)
        u = u + a * u_s
        a = a * a_s
        d *= 2
    return a, u


def _lru_kernel(x_ref, wg_ref, wr_ref, cw_ref, cb_ref, bda_ref, bdx_ref, ba_ref, bx_ref, lam_ref,
                init_ref, h0_ref, mix_ref, tail_ref, hlast_ref,
                wb, bdb, e_sc, carry, hcarry, *, tps, hb, stride, width):
    m = pl.program_id(1)
    tm = x_ref.shape[0]

    @pl.when(m == 0)
    def _():
        wb[0] = wg_ref[...].astype(BF)
        wb[1] = wr_ref[...].astype(BF)
        bdb[0] = bda_ref[...].astype(BF)
        bdb[1] = bdx_ref[...].astype(BF)

    first = (m % tps) == 0
    x = x_ref[...]
    gate = jax.nn.gelu(_dot(x, wb[0]), approximate=True)
    xr = _dot(x, wb[1])
    xc = _conv_tile(e_sc, carry, init_ref, tail_ref, xr, cw_ref, first,
                    hb=hb, stride=stride, width=width) + cb_ref[...]
    xcb = xc.astype(BF)
    r = jax.nn.sigmoid(_dot(xcb, bdb[0]) + ba_ref[...])
    i = jax.nn.sigmoid(_dot(xcb, bdb[1]) + bx_ref[...])
    log_a = (-LRU_C) * r * _softplus(-lam_ref[...])
    a = jnp.exp(log_a)
    u = jnp.sqrt(_neg_expm1(2.0 * log_a)) * i * xc

    @pl.when(first)
    def _():
        hcarry[...] = h0_ref[...]

    if stride == 1:
        a_cum, h_loc = _scan_rows(a, u)
        hs = h_loc + a_cum * hcarry[0:1, :]
        mix_ref[...] = (gate * hs).astype(mix_ref.dtype)
        h_end = jnp.broadcast_to(hs[tm - 1:tm, :], hcarry.shape)
    else:
        h = hcarry[...]
        for t in range(tm // stride):
            sl = slice(t * stride, (t + 1) * stride)
            h = a[sl] * h + u[sl]
            mix_ref[sl, :] = (gate[sl] * h).astype(mix_ref.dtype)
        h_end = h
    hcarry[...] = h_end
    hlast_ref[...] = h_end


def _block_diag(w, per):
    nblk, bd, _ = w.shape
    eye = jnp.eye(per, dtype=w.dtype)
    return jnp.einsum('cide,ik->cidke', w.reshape(nblk // per, per, bd, bd), eye).reshape(
        nblk // per, per * bd, per * bd)


def _rglru(grp, x, w_in, layer, conv_w, conv_b, w_a, b_a, w_x, b_x, lam, init, h0):
    n, d = x.shape
    rw = conv_w.shape[1]
    tm = grp.tm
    width = conv_w.shape[0]
    hb = init.shape[1]
    hr = h0.shape[1]
    bd = w_a.shape[1]
    tn = _pick(rw, (256, 128))
    nb = rw // tn
    tps = grp.tps
    n_init = init.shape[0]
    bda = _block_diag(w_a, tn // bd)
    bdx = _block_diag(w_x, tn // bd)

    def row_spec():
        return pl.BlockSpec((1, tn), lambda c, m: (0, c))

    def seq_of(m):
        return m // tps

    return pl.pallas_call(
        functools.partial(_lru_kernel, tps=tps, hb=hb, stride=grp.stride, width=width),
        out_shape=[jax.ShapeDtypeStruct((n, rw), BF),
                   jax.ShapeDtypeStruct((grp.n_seq, hb, rw), F32),
                   jax.ShapeDtypeStruct((grp.n_seq, hr, rw), F32)],
        grid=(nb, grp.nm),
        in_specs=[pl.BlockSpec((tm, d), lambda c, m: (m, 0)),
                  pl.BlockSpec((None, d, tn), lambda c, m: (layer, 0, c)),
                  pl.BlockSpec((None, d, tn), lambda c, m: (layer, 0, nb + c)),
                  pl.BlockSpec((width, tn), lambda c, m: (0, c)),
                  row_spec(),
                  pl.BlockSpec((None, tn, tn), lambda c, m: (c, 0, 0)),
                  pl.BlockSpec((None, tn, tn), lambda c, m: (c, 0, 0)),
                  row_spec(), row_spec(), row_spec(),
                  pl.BlockSpec((None, hb, tn), lambda c, m: (seq_of(m) % n_init, 0, c)),
                  pl.BlockSpec((None, hr, tn), lambda c, m: (seq_of(m) % n_init, 0, c))],
        out_specs=[pl.BlockSpec((tm, tn), lambda c, m: (m, c)),
                   pl.BlockSpec((None, hb, tn), lambda c, m: (seq_of(m), 0, c)),
                   pl.BlockSpec((None, hr, tn), lambda c, m: (seq_of(m), 0, c))],
        scratch_shapes=[pltpu.VMEM((2, d, tn), BF),
                        pltpu.VMEM((2, tn, tn), BF),
                        pltpu.VMEM((hb + tm, tn), F32),
                        pltpu.VMEM((hb, tn), F32),
                        pltpu.VMEM((hr, tn), F32)],
        compiler_params=_cparams(("arbitrary", "arbitrary")),
        name="rglru_mixer",
    )(x, w_in, w_in, conv_w, conv_b.reshape(1, rw), bda, bdx, b_a.reshape(1, rw),
      b_x.reshape(1, rw), lam.reshape(1, rw), init, h0)


def _post_ln(x, f, gate, ln_g, ln_b, alpha):
    z = alpha * x + (1.0 + gate) * f
    mu = jnp.mean(z, axis=-1, keepdims=True)
    zc = z - mu
    var = jnp.mean(zc * zc, axis=-1, keepdims=True)
    return zc * lax.rsqrt(var + LN_EPS) * ln_g + ln_b


def _top2_gates_t(logits_t):
    ne = float(logits_t.shape[0])
    idx = lax.broadcasted_iota(jnp.int32, logits_t.shape, 0).astype(F32)
    m1 = jnp.max(logits_t, axis=0, keepdims=True)
    i1 = jnp.min(jnp.where(logits_t == m1, idx, ne), axis=0, keepdims=True)
    sel1 = idx == i1
    rest = jnp.where(sel1, -jnp.inf, logits_t)
    m2 = jnp.max(rest, axis=0, keepdims=True)
    i2 = jnp.min(jnp.where(rest == m2, idx, ne), axis=0, keepdims=True)
    sel2 = idx == i2
    e2 = jnp.exp(m2 - m1)
    w1 = 1.0 / (1.0 + e2)
    w2 = e2 / (1.0 + e2)
    return jnp.where(sel1, w1, 0.0) + jnp.where(sel2, w2, 0.0)


def _out_ln_kernel(a_ref, w_ref, x_ref, g_ref, sc_ref, sh_ref, lng_ref, lnb_ref, *refs,
                   alpha, has_router):
    if has_router:
        wr_ref, x1_ref, hf_ref, gates_ref, wb = refs
    else:
        x1_ref, hf_ref, wb = refs

    @pl.when(pl.program_id(0) == 0)
    def _():
        wb[...] = w_ref[...].astype(BF)

    y = _dot(a_ref[...], wb[...])
    x1 = _post_ln(x_ref[...], y, g_ref[...], lng_ref[...], lnb_ref[...], alpha)
    x1_ref[...] = x1
    hf = x1 * (1.0 + sc_ref[...]) + sh_ref[...]
    hf_ref[...] = hf.astype(hf_ref.dtype)
    if has_router:
        wh, wl = _split(wr_ref[...])
        hh, hl = _split(hf)
        logits_t = _dot_nt(wh, hh) + (_dot_nt(wh, hl) + _dot_nt(wl, hh))
        gates_ref[...] = _top2_gates_t(logits_t)


def _out_ln(grp, a, w3, layer, x, gate, sc, sh, ln_g, ln_b, alpha, w_router=None):
    n, d = x.shape
    k = a.shape[1]
    tm = grp.tm
    has_router = w_router is not None
    in_specs = [pl.BlockSpec((tm, k), lambda m: (m, 0)),
                pl.BlockSpec((None, k, d), lambda m: (layer, 0, 0)),
                pl.BlockSpec((tm, d), lambda m: (m, 0)),
                grp.mod_spec(d, 0, 1), grp.mod_spec(d, 0, 1), grp.mod_spec(d, 0, 1),
                pl.BlockSpec((1, d), lambda m: (0, 0)),
                pl.BlockSpec((1, d), lambda m: (0, 0))]
    args = [a, w3, x, gate, sc, sh, ln_g.reshape(1, d), ln_b.reshape(1, d)]
    out_shape = [jax.ShapeDtypeStruct((n, d), F32), jax.ShapeDtypeStruct((n, d), BF)]
    out_specs = [pl.BlockSpec((tm, d), lambda m: (m, 0)), pl.BlockSpec((tm, d), lambda m: (m, 0))]
    if has_router:
        ne = w_router.shape[1]
        in_specs.append(pl.BlockSpec((ne, d), lambda m: (0, 0)))
        args.append(w_router.T)
        out_shape.append(jax.ShapeDtypeStruct((ne, n), F32))
        out_specs.append(pl.BlockSpec((ne, tm), lambda m: (0, m)))
    return pl.pallas_call(
        functools.partial(_out_ln_kernel, alpha=alpha, has_router=has_router),
        out_shape=out_shape,
        grid=(grp.nm,),
        in_specs=in_specs,
        out_specs=out_specs,
        scratch_shapes=[pltpu.VMEM((k, d), BF)],
        compiler_params=_cparams(("arbitrary",)),
        name="out_proj_ln",
    )(*args)


def _swiglu_part(h, wg, wu, wd):
    gu = _dot(h, wg.astype(BF))
    up = _dot(h, wu.astype(BF))
    act = (gu * jax.nn.sigmoid(gu) * up).astype(BF)
    return _dot(act, wd.astype(BF))


def _ffn_kernel(h_ref, wg_ref, wu_ref, wd_ref, x_ref, g_ref, sc_ref, sh_ref, lng_ref, lnb_ref, *refs,
                alpha, has_next):
    if has_next:
        x2_ref, hn_ref, acc = refs
    else:
        x2_ref, acc = refs
    ch = pl.program_id(1)
    part = _swiglu_part(h_ref[...], wg_ref[...], wu_ref[...], wd_ref[...])

    @pl.when(ch == 0)
    def _():
        acc[...] = part

    @pl.when(ch > 0)
    def _():
        acc[...] += part

    @pl.when(ch == pl.num_programs(1) - 1)
    def _():
        x2 = _post_ln(x_ref[...], acc[...], g_ref[...], lng_ref[...], lnb_ref[...], alpha)
        x2_ref[...] = x2
        if has_next:
            hn_ref[...] = (x2 * (1.0 + sc_ref[...]) + sh_ref[...]).astype(hn_ref.dtype)


def _ffn(grp, h, w_gu, w_down, layer, x, gate, sc, sh, ln_g, ln_b, alpha, has_next=True):
    n, d = x.shape
    tm = grp.tm
    ff = w_down.shape[1]
    tf = _pick(ff, (256, 128))
    nch = ff // tf
    row = pl.BlockSpec((tm, d), lambda m, c: (m, 0))
    vec = pl.BlockSpec((1, d), lambda m, c: (0, 0))
    in_specs = [row,
                pl.BlockSpec((None, d, tf), lambda m, c: (layer, 0, c)),
                pl.BlockSpec((None, d, tf), lambda m, c: (layer, 0, nch + c)),
                pl.BlockSpec((None, tf, d), lambda m, c: (layer, c, 0)),
                row, grp.mod_spec(d, 0, 2), grp.mod_spec(d, 0, 2), grp.mod_spec(d, 0, 2), vec, vec]
    out_shape = [jax.ShapeDtypeStruct((n, d), F32)]
    out_specs = [row]
    if has_next:
        out_shape.append(jax.ShapeDtypeStruct((n, d), BF))
        out_specs.append(row)
    res = pl.pallas_call(
        functools.partial(_ffn_kernel, alpha=alpha, has_next=has_next),
        out_shape=out_shape,
        grid=(grp.nm, nch),
        in_specs=in_specs,
        out_specs=out_specs,
        scratch_shapes=[pltpu.VMEM((tm, d), F32)],
        compiler_params=_cparams(("arbitrary", "arbitrary")),
        name="ffn_dense",
    )(h, w_gu, w_gu, w_down, x, gate, sc, sh, ln_g.reshape(1, d), ln_b.reshape(1, d))
    return (res[0], res[1]) if has_next else (res[0], None)


MOE_TT = 512
MOE_RC = 128
MOE_TMG = 1024
MOE_ALIGN = 16


def _seg_sizes(tt):
    sizes = []
    sz = tt
    while sz >= MOE_ALIGN:
        sizes.append(sz)
        sz //= 2
    return tuple(sizes)


def _segments(c16, sizes):
    off = jnp.int32(0)
    for j, sz in enumerate(sizes):
        cond = (c16 & sz) != 0
        yield cond, off, sz, j
        off = off + jnp.where(cond, sz, 0)


def _route_count_kernel(g_ref, o_ref):
    routed = jnp.where(g_ref[...] > 0.0, 1.0, 0.0)
    o_ref[...] = jnp.broadcast_to(jnp.sum(routed, axis=1, keepdims=True), o_ref.shape)


def _route_counts(gates_t, tt):
    ne, n = gates_t.shape
    nt = n // tt
    return pl.pallas_call(
        _route_count_kernel,
        out_shape=jax.ShapeDtypeStruct((nt, ne, LANE_V7X), F32),
        grid=(nt,),
        in_specs=[pl.BlockSpec((ne, tt), lambda i: (0, i))],
        out_specs=pl.BlockSpec((None, ne, LANE_V7X), lambda i: (i, 0, 0)),
        compiler_params=_cparams(("arbitrary",)),
        name="moe_route_counts",
    )(gates_t)


def _dispatch_kernel(base_ref, c16_ref, g_ref, h_ref, sorted_in_ref, sorted_ref, stage, sems,
                     *, ne, tt, rc, sizes):
    del sorted_in_ref
    i = pl.program_id(0)
    gt = g_ref[...]
    routed = gt > 0.0
    ri = lax.broadcasted_iota(jnp.int32, (tt, tt), 0)
    ci = lax.broadcasted_iota(jnp.int32, (tt, tt), 1)
    before = jnp.where(ri < ci, 1.0, 0.0).astype(BF)
    rank = _dot(jnp.where(routed, 1.0, 0.0).astype(BF), before)
    rank = jnp.where(routed, rank, -1.0)
    h = h_ref[...]

    def seg_copy(e, b0, off, sz, j):
        return pltpu.make_async_copy(
            stage.at[e, pl.ds(pl.multiple_of(off, MOE_ALIGN), sz)],
            sorted_ref.at[pl.ds(pl.multiple_of(b0 + off, MOE_ALIGN), sz)],
            sems.at[e, j])

    for e in range(ne):
        c16 = c16_ref[i * ne + e]
        b0 = base_ref[i * ne + e]
        for c in range(tt // rc):
            @pl.when(c * rc < c16)
            def _():
                slot = (lax.broadcasted_iota(jnp.int32, (rc, tt), 0) + c * rc).astype(F32)
                sel = slot == rank[e:e + 1, :]
                stage[e, c * rc:(c + 1) * rc, :] = _dot(jnp.where(sel, 1.0, 0.0).astype(BF), h).astype(BF)
        for cond, off, sz, j in _segments(c16, sizes):
            @pl.when(cond)
            def _():
                seg_copy(e, b0, off, sz, j).start()

    for e in range(ne):
        c16 = c16_ref[i * ne + e]
        b0 = base_ref[i * ne + e]
        for cond, off, sz, j in _segments(c16, sizes):
            @pl.when(cond)
            def _():
                seg_copy(e, b0, off, sz, j).wait()


def _dispatch(base, c16, gates_t, h_all, s_max, tt):
    ne, n = gates_t.shape
    d = h_all.shape[1]
    sizes = _seg_sizes(tt)
    grid_spec = pltpu.PrefetchScalarGridSpec(
        num_scalar_prefetch=2,
        grid=(n // tt,),
        in_specs=[pl.BlockSpec((ne, tt), lambda i, b, c: (0, i)),
                  pl.BlockSpec((tt, d), lambda i, b, c: (i, 0)),
                  pl.BlockSpec(memory_space=pl.ANY)],
        out_specs=pl.BlockSpec(memory_space=pl.ANY),
        scratch_shapes=[pltpu.VMEM((ne, tt, d), BF),
                        pltpu.SemaphoreType.DMA((ne, len(sizes)))])
    return pl.pallas_call(
        functools.partial(_dispatch_kernel, ne=ne, tt=tt, rc=min(MOE_RC, tt), sizes=sizes),
        out_shape=jax.ShapeDtypeStruct((s_max, d), BF),
        grid_spec=grid_spec,
        input_output_aliases={4: 0},
        compiler_params=_cparams(("arbitrary",)),
        name="moe_dispatch",
    )(base, c16, gates_t, h_all, jnp.zeros((s_max, d), BF))


def _grouped_kernel(te_ref, nu_ref, x_ref, wg_ref, wu_ref, wd_ref, y_ref, acc):
    del te_ref
    g = pl.program_id(0)
    ch = pl.program_id(1)
    last = pl.num_programs(1) - 1

    @pl.when(g < nu_ref[0])
    def _():
        part = _swiglu_part(x_ref[...], wg_ref[...], wu_ref[...], wd_ref[...])

        @pl.when(ch == 0)
        def _():
            acc[...] = part

        @pl.when(ch > 0)
        def _():
            acc[...] += part

        @pl.when(ch == last)
        def _():
            y_ref[...] = acc[...].astype(y_ref.dtype)

    @pl.when(jnp.logical_and(g >= nu_ref[0], ch == last))
    def _():
        y_ref[...] = jnp.zeros(y_ref.shape, y_ref.dtype)


def _grouped_ffn(tile_expert, n_used, xs, w_gu, w_down, layer):
    s_max, d = xs.shape
    ff = w_down.shape[2]
    tmg = MOE_TMG
    tf = _pick(ff, (512, 256, 128))
    nch = ff // tf

    def row(g, c, te, nu):
        return (jnp.minimum(g, nu[0] - 1), 0)

    def chunk(g, c, nu):
        return jnp.where(g < nu[0], c, nch - 1)

    grid_spec = pltpu.PrefetchScalarGridSpec(
        num_scalar_prefetch=2,
        grid=(s_max // tmg, nch),
        in_specs=[pl.BlockSpec((tmg, d), row),
                  pl.BlockSpec((None, None, d, tf), lambda g, c, te, nu: (layer, te[g], 0, chunk(g, c, nu))),
                  pl.BlockSpec((None, None, d, tf), lambda g, c, te, nu: (layer, te[g], 0, nch + chunk(g, c, nu))),
                  pl.BlockSpec((None, None, tf, d), lambda g, c, te, nu: (layer, te[g], chunk(g, c, nu), 0))],
        out_specs=pl.BlockSpec((tmg, d), lambda g, c, te, nu: (g, 0)),
        scratch_shapes=[pltpu.VMEM((tmg, d), F32)])
    return pl.pallas_call(
        _grouped_kernel,
        out_shape=jax.ShapeDtypeStruct((s_max, d), BF),
        grid_spec=grid_spec,
        compiler_params=_cparams(("arbitrary", "arbitrary")),
        name="moe_grouped_ffn",
    )(tile_expert, n_used, xs, w_gu, w_gu, w_down)


def _combine_kernel(base_ref, c16_ref, g_ref, y_ref, f_ref, ystage, acc, sems, *, ne, tt, rc, sizes):
    i = pl.program_id(0)

    @pl.when(i == 0)
    def _():
        ystage[...] = jnp.zeros(ystage.shape, ystage.dtype)

    def seg_copy(e, b0, off, sz, j):
        return pltpu.make_async_copy(
            y_ref.at[pl.ds(pl.multiple_of(b0 + off, MOE_ALIGN), sz)],
            ystage.at[e, pl.ds(pl.multiple_of(off, MOE_ALIGN), sz)],
            sems.at[e, j])

    for e in range(ne):
        c16 = c16_ref[i * ne + e]
        b0 = base_ref[i * ne + e]
        for cond, off, sz, j in _segments(c16, sizes):
            @pl.when(cond)
            def _():
                seg_copy(e, b0, off, sz, j).start()

    ri = lax.broadcasted_iota(jnp.int32, (tt, tt), 0)
    ci = lax.broadcasted_iota(jnp.int32, (tt, tt), 1)
    eye = jnp.where(ri == ci, 1.0, 0.0).astype(BF)
    p1, p2, p3 = _split3(g_ref[...])
    gtok = _dot_nt(eye, p1) + _dot_nt(eye, p2) + _dot_nt(eye, p3)
    routed = gtok > 0.0
    before = jnp.where(ci < ri, 1.0, 0.0).astype(BF)
    rank = _dot(before, jnp.where(routed, 1.0, 0.0).astype(BF))
    rank = jnp.where(routed, rank, -1.0)
    acc[...] = jnp.zeros(acc.shape, F32)

    for e in range(ne):
        c16 = c16_ref[i * ne + e]
        b0 = base_ref[i * ne + e]
        for cond, off, sz, j in _segments(c16, sizes):
            @pl.when(cond)
            def _():
                seg_copy(e, b0, off, sz, j).wait()
        for c in range(tt // rc):
            @pl.when(c * rc < c16)
            def _():
                slot = (lax.broadcasted_iota(jnp.int32, (tt, rc), 1) + c * rc).astype(F32)
                sel = slot == rank[:, e:e + 1]
                rows = _dot(jnp.where(sel, 1.0, 0.0).astype(BF), ystage[e, c * rc:(c + 1) * rc, :])
                acc[...] += gtok[:, e:e + 1] * rows

    f_ref[...] = acc[...]


def _combine(base, c16, gates_t, y, tt):
    ne, n = gates_t.shape
    d = y.shape[1]
    sizes = _seg_sizes(tt)
    grid_spec = pltpu.PrefetchScalarGridSpec(
        num_scalar_prefetch=2,
        grid=(n // tt,),
        in_specs=[pl.BlockSpec((ne, tt), lambda i, b, c: (0, i)),
                  pl.BlockSpec(memory_space=pl.ANY)],
        out_specs=pl.BlockSpec((tt, d), lambda i, b, c: (i, 0)),
        scratch_shapes=[pltpu.VMEM((ne, tt, d), BF),
                        pltpu.VMEM((tt, d), F32),
                        pltpu.SemaphoreType.DMA((ne, len(sizes)))])
    return pl.pallas_call(
        functools.partial(_combine_kernel, ne=ne, tt=tt, rc=min(MOE_RC, tt), sizes=sizes),
        out_shape=jax.ShapeDtypeStruct((n, d), F32),
        grid_spec=grid_spec,
        compiler_params=_cparams(("arbitrary",)),
        name="moe_combine",
    )(base, c16, gates_t, y)


def _moe(gates_t, h_all, w_gu, w_down, layer):
    ne, n = gates_t.shape
    tt, tmg = MOE_TT, MOE_TMG
    nt = n // tt
    cnt = _route_counts(gates_t, tt)[:, :, 0].astype(jnp.int32)
    c16 = (cnt + (MOE_ALIGN - 1)) // MOE_ALIGN * MOE_ALIGN
    region = (jnp.sum(c16, axis=0) + (tmg - 1)) // tmg * tmg
    reg_end = jnp.cumsum(region)
    base = (reg_end - region)[None, :] + jnp.cumsum(c16, axis=0) - c16
    s_max = -(-(TOP_K * n + (MOE_ALIGN - 1) * nt * ne + tmg * ne) // tmg) * tmg
    n_used = (reg_end[-1:] // tmg).astype(jnp.int32)
    tile_start = jnp.arange(s_max // tmg, dtype=jnp.int32) * tmg
    tile_expert = jnp.minimum(jnp.sum(tile_start[:, None] >= reg_end[None, :], axis=1), ne - 1).astype(jnp.int32)
    base = base.reshape(-1).astype(jnp.int32)
    c16 = c16.reshape(-1).astype(jnp.int32)
    xs = _dispatch(base, c16, gates_t, h_all, s_max, tt)
    y = _grouped_ffn(tile_expert, n_used, xs, w_gu, w_down, layer)
    return _combine(base, c16, gates_t, y, tt)


def _ln_res_kernel(f_ref, x_ref, g_ref, sc_ref, sh_ref, lng_ref, lnb_ref, *refs, alpha, has_next):
    x2 = _post_ln(x_ref[...], f_ref[...], g_ref[...], lng_ref[...], lnb_ref[...], alpha)
    refs[0][...] = x2
    if has_next:
        refs[1][...] = (x2 * (1.0 + sc_ref[...]) + sh_ref[...]).astype(refs[1].dtype)


def _ln_residual(grp, f_all, row0, x, gate, sc, sh, ln_g, ln_b, alpha, has_next=True):
    n, d = x.shape
    tm = grp.tm
    assert row0 % tm == 0
    blk0 = row0 // tm
    row = pl.BlockSpec((tm, d), lambda m: (m, 0))
    vec = pl.BlockSpec((1, d), lambda m: (0, 0))
    out_shape = [jax.ShapeDtypeStruct((n, d), F32)]
    out_specs = [row]
    if has_next:
        out_shape.append(jax.ShapeDtypeStruct((n, d), BF))
        out_specs.append(row)
    res = pl.pallas_call(
        functools.partial(_ln_res_kernel, alpha=alpha, has_next=has_next),
        out_shape=out_shape,
        grid=(grp.nm,),
        in_specs=[pl.BlockSpec((tm, d), lambda m: (blk0 + m, 0)), row,
                  grp.mod_spec(d, 0, 1), grp.mod_spec(d, 0, 1), grp.mod_spec(d, 0, 1), vec, vec],
        out_specs=out_specs,
        compiler_params=_cparams(("arbitrary",)),
        name="moe_ln_residual",
    )(f_all, x, gate, sc, sh, ln_g.reshape(1, d), ln_b.reshape(1, d))
    return (res[0], res[1]) if has_next else (res[0], None)


def _to_time_major(a):
    a = jnp.swapaxes(a, 0, 1)
    return a.reshape((a.shape[0] * a.shape[1],) + a.shape[2:])


def _from_time_major(a, db):
    a = a.reshape((a.shape[0] // db, db) + a.shape[1:])
    return jnp.swapaxes(a, 0, 1)


def kernel(x_prompt, x_sample, cache_k, cache_v, cache_lf, state_conv_b, state_conv_c, state_h, page_table, c_prompt, c_sample, w_ada, b_ada, ln_g, ln_b, fox_w_in, fox_b_f, fox_w_out, sc_w_in, sc_w_conv, sc_w_out, lru_w_in, lru_conv_w, lru_conv_b, lru_w_a, lru_b_a, lru_w_x, lru_b_x, lru_lam, lru_w_out, ffn_w_gu, ffn_w_down, moe_w_router, moe_w_gu, moe_w_down):
    b, t, d = x_prompt.shape
    db, dt, _ = x_sample.shape
    depth = w_ada.shape[0]
    nh = fox_b_f.shape[1]
    hd = d // nh
    assert hd == FOX_HEAD_DIM and d % LANE_V7X == 0 and db % SUBLANE_V7X == 0
    alpha = (2.0 * depth) ** 0.25
    n_fox = cache_k.shape[0]
    n_pool, page = cache_k.shape[1], cache_k.shape[2]
    ns = db * dt

    gp = _Group(b * t, _pick(t, (1024, 512, 256, 128)), t, 1, b)
    gs = _Group(ns, ns, ns, db, 1)

    mods = _ada(jnp.concatenate([c_prompt, c_sample], axis=0), w_ada, b_ada)
    mods_p = mods[:, :b].reshape(depth, b, 6, d)
    mods_s = jnp.tile(mods[:, b:].reshape(depth, db, 6, d), (1, dt, 1, 1))

    def mod(layer, which, comp):
        if which == 0:
            return mods_p[layer, :, comp][:, None, :]
        return mods_s[layer, :, comp][None]

    cache_kt = jnp.transpose(cache_k, (0, 1, 3, 4, 2))
    cache_vt = jnp.transpose(cache_v, (0, 1, 3, 4, 2))
    cache_lft = jnp.transpose(cache_lf, (0, 1, 3, 2))

    groups = (gp, gs)
    xs = [x_prompt.reshape(b * t, d), _to_time_major(x_sample)]
    hm = [_modulate(groups[w], xs[w], mod(0, w, 1), mod(0, w, 0)) for w in range(2)]
    outs = [dict(k=[], v=[], lf=[], cb=[], cc=[], h=[]) for _ in range(2)]
    x1, hf, gates_t = [None, None], [None, None], [None, None]
    kt_all = jnp.zeros((n_fox, b, d, t), F32)
    vt_all = jnp.zeros((n_fox, b, d, t), F32)

    for i in range(depth):
        kind, j = i % N_MIXERS, i // N_MIXERS
        last = i == depth - 1
        is_moe = i % 2 == 1
        for w in range(2):
            grp = groups[w]
            o = outs[w]
            if kind == 0:
                (qb,) = _linear(grp, hm[w], fox_w_in, j, 0, d, [F32 if w else BF],
                                scale=1.0 if w else Q_SCALE * LOG2_E)
                lf = _forget(grp, hm[w], fox_w_in[j][:, 3 * d:], fox_b_f[j])
                if w == 0:
                    kt_all, ktb = _linear_t(grp, hm[w], fox_w_in[j][:, d:2 * d].T, kt_all, j, rows_out=False)
                    vt_all, vb = _linear_t(grp, hm[w], fox_w_in[j][:, 2 * d:3 * d].T, vt_all, j, rows_out=True)
                    fcum = _cumsum_time(jnp.swapaxes(lf.reshape(b, t, nh), 1, 2), LOG2_E)
                    a_mix = _attention_prompt(qb, ktb, vb, fcum.reshape(b, nh // 2, 2, t), b, t)
                    o['lf'].append(lf.reshape(b, t, nh))
                else:
                    k32, kb = _linear(grp, hm[w], fox_w_in, j, d, d, [F32, BF])
                    v32, vb = _linear(grp, hm[w], fox_w_in, j, 2 * d, d, [F32, BF])
                    nk = 2 * SUBLANE_V7X
                    assert dt <= nk
                    k_b = _from_time_major(k32, db)
                    v_b = _from_time_major(v32, db)
                    lf_b = _from_time_major(lf, db)
                    pad = ((0, 0), (0, nk - dt), (0, 0))
                    lf_new_t = jnp.pad(jnp.swapaxes(lf_b, 1, 2), ((0, 0), (0, 0), (0, LANE_V7X - dt)))
                    att = _attention_sample(j, page_table, _from_time_major(qb, db), jnp.pad(k_b, pad),
                                            jnp.pad(v_b, pad), lf_new_t, cache_kt, cache_vt, cache_lft)
                    a_mix = _to_time_major(att).astype(BF)
                    o['k'].append(k_b.reshape(db, dt, nh, hd))
                    o['v'].append(v_b.reshape(db, dt, nh, hd))
                    o['lf'].append(lf_b)
                w_out = fox_w_out
            elif kind == 1:
                width = sc_w_conv.shape[1]
                if w == 0:
                    init = jnp.zeros((1, SUBLANE_V7X, d), F32)
                else:
                    init = _to_time_major(state_conv_b[j])[None]
                a_mix, tail = _short_conv(grp, hm[w], sc_w_in, j, sc_w_conv[j], init)
                if w == 0:
                    o['cb'].append(tail[:, SUBLANE_V7X - (width - 1):, :])
                else:
                    o['cb'].append(_from_time_major(tail[0], db))
                w_out = sc_w_out
            else:
                width = lru_conv_w.shape[1]
                if w == 0:
                    init = jnp.zeros((1, SUBLANE_V7X, d), F32)
                    h0 = jnp.zeros((1, SUBLANE_V7X, d), F32)
                else:
                    init = _to_time_major(state_conv_c[j])[None]
                    h0 = state_h[j][None]
                a_mix, tail, hl = _rglru(grp, hm[w], lru_w_in, j, lru_conv_w[j], lru_conv_b[j], lru_w_a[j],
                                         lru_b_a[j], lru_w_x[j], lru_b_x[j], lru_lam[j], init, h0)
                if w == 0:
                    o['cc'].append(tail[:, SUBLANE_V7X - (width - 1):, :])
                    o['h'].append(hl[:, 0, :])
                else:
                    o['cc'].append(_from_time_major(tail[0], db))
                    o['h'].append(hl[0])
                w_out = lru_w_out

            res = _out_ln(grp, a_mix, w_out, j, xs[w], mod(i, w, 2), mod(i, w, 4), mod(i, w, 3),
                          ln_g[i, 0], ln_b[i, 0], alpha, moe_w_router[i // 2] if is_moe else None)
            x1[w], hf[w] = res[0], res[1]
            if is_moe:
                gates_t[w] = res[2]

        nxt = [(mod(i + 1, w, 1), mod(i + 1, w, 0)) if not last else (mod(i, w, 1), mod(i, w, 0)) for w in range(2)]
        if is_moe:
            n_real = b * t + ns
            n_all = -(-n_real // MOE_TT) * MOE_TT
            h_all = jnp.concatenate([hf[0], hf[1], jnp.zeros((n_all - n_real, d), BF)], axis=0)
            g_all = jnp.concatenate([gates_t[0], gates_t[1],
                                     jnp.zeros((gates_t[0].shape[0], n_all - n_real), F32)], axis=1)
            f_all = _moe(g_all, h_all, moe_w_gu, moe_w_down, i // 2)
            for w in range(2):
                xs[w], hm[w] = _ln_residual(groups[w], f_all, w * b * t, x1[w], mod(i, w, 5), nxt[w][0], nxt[w][1],
                                            ln_g[i, 1], ln_b[i, 1], alpha, has_next=not last)
        else:
            for w in range(2):
                xs[w], hm[w] = _ffn(groups[w], hf[w], ffn_w_gu, ffn_w_down, i // 2, x1[w], mod(i, w, 5),
                                    nxt[w][0], nxt[w][1], ln_g[i, 1], ln_b[i, 1], alpha, has_next=not last)

    y_prompt = xs[0].reshape(b, t, d)
    y_sample = _from_time_major(xs[1], db)
    op, os_ = outs
    return (y_prompt, y_sample,
            jnp.transpose(kt_all.reshape(n_fox, b, nh, hd, t), (0, 1, 4, 2, 3)),
            jnp.transpose(vt_all.reshape(n_fox, b, nh, hd, t), (0, 1, 4, 2, 3)), jnp.stack(op['lf']),
            jnp.stack(op['cb']), jnp.stack(op['cc']), jnp.stack(op['h']),
            jnp.stack(os_['k']), jnp.stack(os_['v']), jnp.stack(os_['lf']),
            jnp.stack(os_['cb']), jnp.stack(os_['cc']), jnp.stack(os_['h']))
```
